```python
import jax, jax.numpy as jnp
from jax import lax
import numpy as np

D_MODEL = 2048
BATCH = 1
SEQ = 8192
DEPTH = 4

GRID_W = 64
CTX_LEN = 256
HEAD_DIM = 128
NORM_EPS = 1e-6

NA_HEADS = 8
NA_WIDTH = NA_HEADS * HEAD_DIM
NA_WIN_ROWS = 8
NA_WIN_COLS = 16

CONV_WIDTH = D_MODEL // 2
CONV_KERNEL = 31

EVEN_SPLITS = (NA_WIDTH, NA_WIDTH, NA_WIDTH, NA_WIDTH, CONV_WIDTH, CONV_WIDTH, CONV_WIDTH)
EVEN_IN = sum(EVEN_SPLITS)
EVEN_MIX = NA_WIDTH + CONV_WIDTH

GQA_HEADS = D_MODEL // HEAD_DIM
GQA_KV_HEADS = 4
GQA_GROUP = GQA_HEADS // GQA_KV_HEADS
GQA_WIDTH = GQA_HEADS * HEAD_DIM
GQA_KV_WIDTH = GQA_KV_HEADS * HEAD_DIM
ODD_SPLITS = (GQA_WIDTH, GQA_KV_WIDTH, GQA_KV_WIDTH, GQA_WIDTH)
ODD_IN = sum(ODD_SPLITS)
Q_BLOCK = 128
ROPE_THETA = 10000.0

N_EVEN = (DEPTH + 1) // 2
N_ODD = DEPTH // 2

kernel_name = "hybrid_natten_conformer_gqa_prefix_dit"


def rms_norm(x, g):
    xf = x.astype(jnp.float32)
    y = xf * lax.rsqrt(jnp.mean(xf * xf, axis=-1, keepdims=True) + NORM_EPS)
    return y.astype(x.dtype) * g


def layer_norm(x, g, b):
    xf = x.astype(jnp.float32)
    mu = jnp.mean(xf, axis=-1, keepdims=True)
    xc = xf - mu
    y = xc * lax.rsqrt(jnp.mean(xc * xc, axis=-1, keepdims=True) + NORM_EPS)
    return y.astype(x.dtype) * g + b


def split_cols(p, sizes):
    return jnp.split(p, np.cumsum(sizes)[:-1].tolist(), axis=-1)


def axial_rope_tables(n):
    t = jnp.arange(n, dtype=jnp.int32)
    row = (t // GRID_W).astype(jnp.float32)
    col = (t % GRID_W).astype(jnp.float32)
    n_freq = HEAD_DIM // 4
    inv = ROPE_THETA ** (-jnp.arange(n_freq, dtype=jnp.float32) / n_freq)
    ang = jnp.stack([row[:, None] * inv, col[:, None] * inv], axis=1)
    return jnp.cos(ang)[:, None], jnp.sin(ang)[:, None]


def apply_axial_rope(x, cos, sin):
    b, n, h, _ = x.shape
    xf = x.astype(jnp.float32).reshape(b, n, h, 2, 2, HEAD_DIM // 4)
    x1, x2 = xf[..., 0, :], xf[..., 1, :]
    out = jnp.stack([x1 * cos - x2 * sin, x2 * cos + x1 * sin], axis=-2)
    return out.reshape(b, n, h, HEAD_DIM).astype(x.dtype)


def gqa_attn(q, k, v):
    s = jnp.einsum('bqhgd,bshd->bhgqs', q, k) * (q.shape[-1] ** -0.5)
    p = jax.nn.softmax(s.astype(jnp.float32), axis=-1).astype(v.dtype)
    return jnp.einsum('bhgqs,bshd->bqhgd', p, v)


def neighbourhood_attn(q, k, v, k_ctx, v_ctx, rpb):
    b, n, h, d = q.shape
    rows = n // GRID_W
    wr = min(NA_WIN_ROWS, rows)
    wc = NA_WIN_COLS
    qg = q.reshape(b, rows, GRID_W, h, d)
    kg = k.reshape(b, rows, GRID_W, h, d)
    vg = v.reshape(b, rows, GRID_W, h, d)
    qcol = np.arange(GRID_W)
    col_start = np.clip(qcol - wc // 2, 0, GRID_W - wc)
    col_idx = (col_start[:, None] + np.arange(wc)[None]).astype(np.int32)
    col_off = (col_idx - qcol[:, None] + (wc - 1)).astype(np.int32)
    scale = d ** -0.5

    def row_block(r):
        rs = jnp.clip(r - wr // 2, 0, rows - wr)
        q_r = lax.dynamic_index_in_dim(qg, r, axis=1, keepdims=False)
        k_win = lax.dynamic_slice_in_dim(kg, rs, wr, axis=1)[:, :, col_idx]
        v_win = lax.dynamic_slice_in_dim(vg, rs, wr, axis=1)[:, :, col_idx]
        row_off = rs + jnp.arange(wr, dtype=jnp.int32) - r + (NA_WIN_ROWS - 1)
        bias = rpb[:, row_off[:, None, None], col_off[None]]
        bias = bias.transpose(0, 2, 1, 3).reshape(h, GRID_W, wr * wc)
        s_win = jnp.einsum('bqhd,bjqchd->bhqjc', q_r, k_win).reshape(b, h, GRID_W, wr * wc) * scale + bias
        s_ctx = jnp.einsum('bqhd,bshd->bhqs', q_r, k_ctx) * scale
        p = jax.nn.softmax(jnp.concatenate([s_win, s_ctx], axis=-1).astype(jnp.float32), axis=-1).astype(v.dtype)
        p_win = p[..., :wr * wc].reshape(b, h, GRID_W, wr, wc)
        p_ctx = p[..., wr * wc:]
        return (jnp.einsum('bhqjc,bjqchd->bqhd', p_win, v_win)
                + jnp.einsum('bhqs,bshd->bqhd', p_ctx, v_ctx))

    out = lax.map(row_block, jnp.arange(rows, dtype=jnp.int32))
    return out.transpose(1, 0, 2, 3, 4).reshape(b, n, h, d)


def conformer_conv(a, g_in, dw_w, dw_b, ln_g, ln_b):
    u = a * jax.nn.sigmoid(g_in)
    u = lax.conv_general_dilated(
        u, dw_w[:, None, :], window_strides=(1,),
        padding=[(CONV_KERNEL // 2, CONV_KERNEL // 2)],
        dimension_numbers=('NWC', 'WIO', 'NWC'),
        feature_group_count=u.shape[-1]) + dw_b
    return jax.nn.silu(layer_norm(u, ln_g, ln_b))


def even_layer(h_lat, h_ctx, w_in, rpb, dw_w, dw_b, ln_g, ln_b, w_out, update_ctx):
    b, n, _ = h_lat.shape
    heads = lambda t: t.reshape(*t.shape[:-1], NA_HEADS, HEAD_DIM)
    qa, ka, va, ga, ua, ub, gb = split_cols(h_lat @ w_in, EVEN_SPLITS)
    qa_c, ka_c, va_c, ga_c, ua_c, ub_c, gb_c = split_cols(h_ctx @ w_in, EVEN_SPLITS)
    k_ctx, v_ctx = heads(ka_c), heads(va_c)
    o_a = neighbourhood_attn(heads(qa), heads(ka), heads(va), k_ctx, v_ctx, rpb).reshape(b, n, NA_WIDTH)
    o_b = conformer_conv(ua, ub, dw_w, dw_b, ln_g, ln_b)
    y_lat = jnp.concatenate([jax.nn.silu(ga) * o_a, jax.nn.silu(gb) * o_b], axis=-1) @ w_out
    if not update_ctx:
        return y_lat, None
    lc = h_ctx.shape[1]
    o_a_c = gqa_attn(heads(qa_c)[:, :, :, None], k_ctx, v_ctx).reshape(b, lc, NA_WIDTH)
    o_b_c = conformer_conv(ua_c, ub_c, dw_w, dw_b, ln_g, ln_b)
    y_ctx = jnp.concatenate([jax.nn.silu(ga_c) * o_a_c, jax.nn.silu(gb_c) * o_b_c], axis=-1) @ w_out
    return y_lat, y_ctx


def odd_layer(h_lat, h_ctx, w_in, q_norm, k_norm, w_out, cos, sin, update_ctx):
    b, n, _ = h_lat.shape
    lc = h_ctx.shape[1]
    q, k, v, g = split_cols(h_lat @ w_in, ODD_SPLITS)
    q_c, k_c, v_c, g_c = split_cols(h_ctx @ w_in, ODD_SPLITS)
    q = apply_axial_rope(rms_norm(q.reshape(b, n, GQA_HEADS, HEAD_DIM), q_norm), cos, sin)
    k = apply_axial_rope(rms_norm(k.reshape(b, n, GQA_KV_HEADS, HEAD_DIM), k_norm), cos, sin)
    v = v.reshape(b, n, GQA_KV_HEADS, HEAD_DIM)
    k_c = rms_norm(k_c.reshape(b, lc, GQA_KV_HEADS, HEAD_DIM), k_norm)
    v_c = v_c.reshape(b, lc, GQA_KV_HEADS, HEAD_DIM)
    k_all = jnp.concatenate([k, k_c], axis=1)
    v_all = jnp.concatenate([v, v_c], axis=1)
    q_blocks = q.reshape(b, n // Q_BLOCK, Q_BLOCK, GQA_KV_HEADS, GQA_GROUP, HEAD_DIM).swapaxes(0, 1)
    o = lax.map(lambda qb: gqa_attn(qb, k_all, v_all), q_blocks)
    o = o.swapaxes(0, 1).reshape(b, n, GQA_WIDTH)
    y_lat = (jax.nn.silu(g) * o) @ w_out
    if not update_ctx:
        return y_lat, None
    q_c = rms_norm(q_c.reshape(b, lc, GQA_HEADS, HEAD_DIM), q_norm).reshape(b, lc, GQA_KV_HEADS, GQA_GROUP, HEAD_DIM)
    o_c = gqa_attn(q_c, k_c, v_c).reshape(b, lc, GQA_WIDTH)
    y_ctx = (jax.nn.silu(g_c) * o_c) @ w_out
    return y_lat, y_ctx


def setup_inputs(seed: int = 0) -> dict:
    key = jax.random.key(seed)
    ks = jax.random.split(key, 24)
    f32 = jnp.float32
    nrm = lambda k, shape, s: jax.random.normal(k, shape, f32) * s
    D = D_MODEL
    return {
        "x": nrm(ks[0], (BATCH, SEQ, D), 1.0),
        "c": nrm(ks[1], (BATCH, D), 1.0),
        "ctx": nrm(ks[2], (BATCH, CTX_LEN, D), 1.0),
        "c_ctx": nrm(ks[3], (D,), 1.0),
        "norm_g": 1.0 + nrm(ks[4], (DEPTH, D), 0.02),
        "w_mod": nrm(ks[5], (DEPTH, D, 3 * D), 0.5 * D ** -0.5),
        "b_mod": nrm(ks[6], (DEPTH, 3 * D), 0.02),
        "e_w_in": nrm(ks[7], (N_EVEN, D, EVEN_IN), D ** -0.5),
        "e_rpb": nrm(ks[8], (N_EVEN, NA_HEADS, 2 * NA_WIN_ROWS - 1, 2 * NA_WIN_COLS - 1), 0.02),
        "e_dw_w": nrm(ks[9], (N_EVEN, CONV_KERNEL, CONV_WIDTH), CONV_KERNEL ** -0.5),
        "e_dw_b": nrm(ks[10], (N_EVEN, CONV_WIDTH), 0.02),
        "e_ln_g": 1.0 + nrm(ks[11], (N_EVEN, CONV_WIDTH), 0.02),
        "e_ln_b": nrm(ks[12], (N_EVEN, CONV_WIDTH), 0.02),
        "e_w_out": nrm(ks[13], (N_EVEN, EVEN_MIX, D), EVEN_MIX ** -0.5),
        "o_w_in": nrm(ks[14], (N_ODD, D, ODD_IN), D ** -0.5),
        "o_q_norm": 1.0 + nrm(ks[15], (N_ODD, HEAD_DIM), 0.02),
        "o_k_norm": 1.0 + nrm(ks[16], (N_ODD, HEAD_DIM), 0.02),
        "o_w_out": nrm(ks[17], (N_ODD, GQA_WIDTH, D), GQA_WIDTH ** -0.5),
        "final_norm_g": 1.0 + nrm(ks[18], (D,), 0.02),
    }


def reference(x, c, ctx, c_ctx, norm_g, w_mod, b_mod, e_w_in, e_rpb, e_dw_w, e_dw_b, e_ln_g, e_ln_b,
              e_w_out, o_w_in, o_q_norm, o_k_norm, o_w_out, final_norm_g):
    n = x.shape[1]
    cos, sin = axial_rope_tables(n)
    s_lat = jax.nn.silu(c)
    s_ctx = jax.nn.silu(c_ctx)
    for l in range(DEPTH):
        i = l // 2
        update_ctx = l < DEPTH - 1
        sh, sc, gt = jnp.split((s_lat @ w_mod[l] + b_mod[l])[:, None, :], 3, axis=-1)
        sh_c, sc_c, gt_c = jnp.split(s_ctx @ w_mod[l] + b_mod[l], 3, axis=-1)
        h_lat = rms_norm(x, norm_g[l]) * (1.0 + sc) + sh
        h_ctx = rms_norm(ctx, norm_g[l]) * (1.0 + sc_c) + sh_c
        if l % 2 == 0:
            y_lat, y_ctx = even_layer(h_lat, h_ctx, e_w_in[i], e_rpb[i], e_dw_w[i], e_dw_b[i],
                                      e_ln_g[i], e_ln_b[i], e_w_out[i], update_ctx)
        else:
            y_lat, y_ctx = odd_layer(h_lat, h_ctx, o_w_in[i], o_q_norm[i], o_k_norm[i], o_w_out[i],
                                     cos, sin, update_ctx)
        x = x + gt * y_lat
        if update_ctx:
            ctx = ctx + gt_c * y_ctx
    return rms_norm(x, final_norm_g)
```

```python
import functools

import numpy as np
import jax
import jax.numpy as jnp
from jax import lax
from jax.experimental import pallas as pl
from jax.experimental.pallas import tpu as pltpu

F32 = jnp.float32
BF16 = jnp.bfloat16

GRID_W = 64
HEAD_DIM = 128
NORM_EPS = 1e-6
NA_HEADS = 8
NA_WIN_ROWS = 8
NA_WIN_COLS = 16
CONV_KERNEL = 31
GQA_HEADS = 16
GQA_KV_HEADS = 4
GQA_GROUP = GQA_HEADS // GQA_KV_HEADS
ROPE_THETA = 10000.0
LOG2E = 1.4426950408889634

LANES = 128
V7X_VMEM_BYTES = 64 * 1024 * 1024
VMEM_LIMIT = V7X_VMEM_BYTES - 8 * 1024 * 1024

TM_PROJ = 384
TM_OUT = 256
NA_ROWS = 4
NA_SLAB = NA_WIN_ROWS + NA_ROWS
NA_TQ = NA_ROWS * GRID_W
CONV_T = 128
CONV_HALO = 16
CONV_CH = 32
GQA_TQ = 256
GQA_TK = 512
NEG_BIG = -1e30


def _cparams(n_axes):
    return pltpu.CompilerParams(dimension_semantics=("arbitrary",) * n_axes, vmem_limit_bytes=VMEM_LIMIT)


def _sigmoid(x):
    return 1.0 / (1.0 + jnp.exp(-x))


def _silu(x):
    return x * _sigmoid(x)


def _dot(a, b):
    return jnp.dot(a, b, preferred_element_type=F32)


def _dot_nt(a, b):
    return lax.dot_general(a, b, (((1,), (1,)), ((), ())), preferred_element_type=F32)


def _row_is_ctx(tile_idx, tm, n_lat):
    rows = tile_idx * tm + lax.broadcasted_iota(jnp.int32, (tm, 1), 0)
    return rows >= n_lat


def _mod_rows(mod_ref, part, is_ctx, d):
    lat = mod_ref[0:1, part * d:(part + 1) * d]
    ctx = mod_ref[1:2, part * d:(part + 1) * d]
    return jnp.where(is_ctx, ctx, lat)


def _prenorm(x_ref, g_ref, mod_ref, tile_idx, tm, n_lat):
    x = x_ref[...]
    d = x.shape[-1]
    ms = jnp.mean(x * x, axis=-1, keepdims=True)
    xn = x * lax.rsqrt(ms + NORM_EPS) * g_ref[...]
    is_ctx = _row_is_ctx(tile_idx, tm, n_lat)
    sh = _mod_rows(mod_ref, 0, is_ctx, d)
    sc = _mod_rows(mod_ref, 1, is_ctx, d)
    return (xn * (1.0 + sc) + sh).astype(BF16)


def _mod_kernel(c_ref, w_ref, b_ref, o_ref):
    s = _silu(c_ref[...]).astype(BF16)
    o_ref[...] = _dot(s, w_ref[...].astype(BF16)) + b_ref[...]


def _modulation(cvec, w_mod, b_mod):
    depth, d, d3 = w_mod.shape
    tn = 768
    return pl.pallas_call(
        _mod_kernel,
        grid=(depth, d3 // tn),
        in_specs=[
            pl.BlockSpec((8, d), lambda l, j: (0, 0)),
            pl.BlockSpec((None, d, tn), lambda l, j: (l, 0, j)),
            pl.BlockSpec((None, 1, tn), lambda l, j: (l, 0, j)),
        ],
        out_specs=pl.BlockSpec((None, 8, tn), lambda l, j: (l, 0, j)),
        out_shape=jax.ShapeDtypeStruct((depth, 8, d3), F32),
        compiler_params=_cparams(2),
        name="modulation",
    )(cvec, w_mod, b_mod.reshape(depth, 1, d3))


def _even_proj_kernel(x_ref, g_ref, mod_ref, w_ref, qkv_ref, ga_ref, u_ref, gb_ref, h_scr, *, n_lat, tm):
    i = pl.program_id(0)
    j = pl.program_id(1)

    @pl.when(j == 0)
    def _():
        h_scr[...] = _prenorm(x_ref, g_ref, mod_ref, i, tm, n_lat)

    acc = _dot(h_scr[...], w_ref[...])

    @pl.when(j < 3)
    def _():
        qkv_ref[...] = acc.astype(BF16)

    @pl.when(j == 3)
    def _():
        ga_ref[...] = _silu(acc).astype(BF16)

    @pl.when(jnp.logical_or(j == 4, j == 5))
    def _():
        half = acc.shape[-1] // 2
        u_ref[...] = acc[:, :half] * _sigmoid(acc[:, half:])

    @pl.when(j == 6)
    def _():
        gb_ref[...] = _silu(acc).astype(BF16)


def _even_proj(xs, g, mod, w_perm, n_lat):
    nt, d = xs.shape
    tm, tn = TM_PROJ, 1024
    n_steps = w_perm.shape[1] // tn
    assert n_steps == 7 and nt % tm == 0
    kern = functools.partial(_even_proj_kernel, n_lat=n_lat, tm=tm)
    return pl.pallas_call(
        kern,
        grid=(nt // tm, n_steps),
        in_specs=[
            pl.BlockSpec((tm, d), lambda i, j: (i, 0)),
            pl.BlockSpec((1, d), lambda i, j: (0, 0)),
            pl.BlockSpec((8, 3 * d), lambda i, j: (0, 0)),
            pl.BlockSpec((d, tn), lambda i, j: (0, j)),
        ],
        out_specs=[
            pl.BlockSpec((tm, tn), lambda i, j: (i, jnp.minimum(j, 2))),
            pl.BlockSpec((tm, tn), lambda i, j: (i, 0)),
            pl.BlockSpec((tm, tn // 2), lambda i, j: (i, jnp.clip(j - 4, 0, 1))),
            pl.BlockSpec((tm, tn), lambda i, j: (i, 0)),
        ],
        out_shape=[
            jax.ShapeDtypeStruct((nt, 3 * tn), BF16),
            jax.ShapeDtypeStruct((nt, tn), BF16),
            jax.ShapeDtypeStruct((nt, tn), F32),
            jax.ShapeDtypeStruct((nt, tn), BF16),
        ],
        scratch_shapes=[pltpu.VMEM((tm, d), BF16)],
        compiler_params=_cparams(2),
        name="even_proj",
    )(xs, g, mod, w_perm)


def _na_bias_table(rpb, rows):
    pats = []
    for r0 in (0, 2 * NA_ROWS, rows - NA_ROWS):
        ks = int(np.clip(r0 - NA_WIN_ROWS // 2, 0, rows - NA_SLAB))
        qr = r0 + np.arange(NA_ROWS)[:, None, None, None]
        qc = np.arange(GRID_W)[None, :, None, None]
        kr = ks + np.arange(NA_SLAB)[None, None, :, None]
        kc = np.arange(GRID_W)[None, None, None, :]
        rs = np.clip(qr - NA_WIN_ROWS // 2, 0, rows - NA_WIN_ROWS)
        cs = np.clip(qc - NA_WIN_COLS // 2, 0, GRID_W - NA_WIN_COLS)
        valid = (kr >= rs) & (kr < rs + NA_WIN_ROWS) & (kc >= cs) & (kc < cs + NA_WIN_COLS)
        ro = np.clip(kr - qr + (NA_WIN_ROWS - 1), 0, 2 * NA_WIN_ROWS - 2)
        co = np.clip(kc - qc + (NA_WIN_COLS - 1), 0, 2 * NA_WIN_COLS - 2)
        shape = (NA_ROWS, GRID_W, NA_SLAB, GRID_W)
        ro = np.broadcast_to(ro, shape).reshape(NA_TQ, NA_SLAB * GRID_W)
        co = np.broadcast_to(co, shape).reshape(NA_TQ, NA_SLAB * GRID_W)
        valid = np.broadcast_to(valid, shape).reshape(NA_TQ, NA_SLAB * GRID_W)
        pats.append(jnp.where(valid[None], rpb[:, ro, co], NEG_BIG))
    return jnp.stack(pats).astype(F32)


def _na_kernel(q_ref, k_ref, v_ref, bias_ref, ga_ref, o_ref, *, n_lat, nc, rows):
    b = pl.program_id(1)
    nb_lat = rows // NA_ROWS
    scale = HEAD_DIM ** -0.5
    q = q_ref[...]
    gate = ga_ref[...].astype(F32)
    k_ctx = k_ref[n_lat:n_lat + nc, :]
    v_ctx = v_ref[n_lat:n_lat + nc, :]
    s_ctx = _dot_nt(q, k_ctx) * scale
    m_ctx = jnp.max(s_ctx, axis=-1, keepdims=True)

    @pl.when(b < nb_lat)
    def _():
        ks = jnp.clip(b * NA_ROWS - NA_WIN_ROWS // 2, 0, rows - NA_SLAB)
        start = pl.multiple_of(ks * GRID_W, GRID_W)
        k_win = k_ref[pl.ds(start, NA_SLAB * GRID_W), :]
        v_win = v_ref[pl.ds(start, NA_SLAB * GRID_W), :]
        s_win = _dot_nt(q, k_win) * scale + bias_ref[...]
        m = jnp.maximum(jnp.max(s_win, axis=-1, keepdims=True), m_ctx)
        p_win = jnp.exp(s_win - m)
        p_ctx = jnp.exp(s_ctx - m)
        denom = jnp.sum(p_win, axis=-1, keepdims=True) + jnp.sum(p_ctx, axis=-1, keepdims=True)
        o = _dot(p_win.astype(BF16), v_win) + _dot(p_ctx.astype(BF16), v_ctx)
        o_ref[...] = (o / denom * gate).astype(BF16)

    @pl.when(b >= nb_lat)
    def _():
        p_ctx = jnp.exp(s_ctx - m_ctx)
        denom = jnp.sum(p_ctx, axis=-1, keepdims=True)
        o = _dot(p_ctx.astype(BF16), v_ctx)
        o_ref[...] = (o / denom * gate).astype(BF16)


def _na_attention(qkv, ga, bias_tab, n_lat):
    nt = qkv.shape[0]
    nc = nt - n_lat
    rows = n_lat // GRID_W
    assert nc == NA_TQ and rows % NA_ROWS == 0 and rows >= NA_SLAB
    nb = nt // NA_TQ
    nb_lat = rows // NA_ROWS
    slab = NA_SLAB * GRID_W

    def pat(b):
        return jnp.where(b == 0, 0, jnp.where(b >= nb_lat - 1, 2, 1))

    kern = functools.partial(_na_kernel, n_lat=n_lat, nc=nc, rows=rows)
    return pl.pallas_call(
        kern,
        grid=(NA_HEADS, nb),
        in_specs=[
            pl.BlockSpec((NA_TQ, HEAD_DIM), lambda h, b: (b, h)),
            pl.BlockSpec((nt, HEAD_DIM), lambda h, b: (0, NA_HEADS + h)),
            pl.BlockSpec((nt, HEAD_DIM), lambda h, b: (0, 2 * NA_HEADS + h)),
            pl.BlockSpec((None, None, NA_TQ, slab), lambda h, b: (pat(b), h, 0, 0)),
            pl.BlockSpec((NA_TQ, HEAD_DIM), lambda h, b: (b, h)),
        ],
        out_specs=pl.BlockSpec((NA_TQ, HEAD_DIM), lambda h, b: (b, h)),
        out_shape=jax.ShapeDtypeStruct((nt, NA_HEADS * HEAD_DIM), BF16),
        compiler_params=_cparams(2),
        name="na_attention",
    )(qkv, qkv, qkv, bias_tab, ga)


def _conv_kernel(um_ref, up_ref, un_ref, gb_ref, w_ref, b_ref, lg_ref, lb_ref, o_ref, ubuf, *, n_lat, nt):
    t = pl.program_id(0)
    start = t * CONV_T
    prev_ok = jnp.logical_and(start != 0, start != n_lat)
    next_ok = jnp.logical_and(start + CONV_T != n_lat, start + CONV_T != nt)
    ubuf[0:CONV_HALO, :] = jnp.where(prev_ok, up_ref[...], 0.0)
    ubuf[CONV_HALO:CONV_HALO + CONV_T, :] = um_ref[...]
    ubuf[CONV_HALO + CONV_T:, :] = jnp.where(next_ok, un_ref[...], 0.0)
    off = CONV_HALO - CONV_KERNEL // 2
    for r0 in range(0, CONV_T, CONV_CH):
        acc = jnp.zeros((CONV_CH, um_ref.shape[-1]), F32) + b_ref[...]
        for j in range(CONV_KERNEL):
            acc = acc + w_ref[j:j + 1, :] * ubuf[r0 + off + j:r0 + off + j + CONV_CH, :]
        mu = jnp.mean(acc, axis=-1, keepdims=True)
        xc = acc - mu
        var = jnp.mean(xc * xc, axis=-1, keepdims=True)
        y = xc * lax.rsqrt(var + NORM_EPS) * lg_ref[...] + lb_ref[...]
        y = _silu(y) * gb_ref[r0:r0 + CONV_CH, :].astype(F32)
        o_ref[r0:r0 + CONV_CH, :] = y.astype(BF16)


def _conformer_conv(u, gb, dw_w, dw_b, ln_g, ln_b, n_lat):
    nt, cw = u.shape
    assert nt % CONV_T == 0 and n_lat % CONV_T == 0
    hb = CONV_T // CONV_HALO
    n_halo_blocks = nt // CONV_HALO
    w_pad = jnp.zeros((32, cw), F32).at[:CONV_KERNEL].set(dw_w)
    kern = functools.partial(_conv_kernel, n_lat=n_lat, nt=nt)
    vec = lambda: pl.BlockSpec((1, cw), lambda t: (0, 0))
    return pl.pallas_call(
        kern,
        grid=(nt // CONV_T,),
        in_specs=[
            pl.BlockSpec((CONV_T, cw), lambda t: (t, 0)),
            pl.BlockSpec((CONV_HALO, cw), lambda t: (jnp.maximum(t * hb - 1, 0), 0)),
            pl.BlockSpec((CONV_HALO, cw), lambda t: (jnp.minimum((t + 1) * hb, n_halo_blocks - 1), 0)),
            pl.BlockSpec((CONV_T, cw), lambda t: (t, 0)),
            pl.BlockSpec((32, cw), lambda t: (0, 0)),
            vec(), vec(), vec(),
        ],
        out_specs=pl.BlockSpec((CONV_T, cw), lambda t: (t, 0)),
        out_shape=jax.ShapeDtypeStruct((nt, cw), BF16),
        scratch_shapes=[pltpu.VMEM((CONV_T + 2 * CONV_HALO, cw), F32)],
        compiler_params=_cparams(1),
        name="conformer_conv",
    )(u, u, u, gb, w_pad, dw_b.reshape(1, cw), ln_g.reshape(1, cw), ln_b.reshape(1, cw))


def _out_proj_kernel(a_ref, b_ref, w_ref, x_ref, mod_ref, fg_ref, o_ref, *, n_lat, tm, final):
    i = pl.program_id(0)
    half = a_ref.shape[-1]
    y = _dot(a_ref[...], w_ref[0:half, :]) + _dot(b_ref[...], w_ref[half:, :])
    d = y.shape[-1]
    gt = _mod_rows(mod_ref, 2, _row_is_ctx(i, tm, n_lat), d)
    xn = x_ref[...] + gt * y
    if final:
        ms = jnp.mean(xn * xn, axis=-1, keepdims=True)
        xn = xn * lax.rsqrt(ms + NORM_EPS) * fg_ref[...]
    o_ref[...] = xn


def _out_proj(mix_a, mix_b, cols, w, xs, mod, fg, n_lat, final):
    nt, d = xs.shape
    tm = TM_OUT
    half = w.shape[0] // 2
    kern = functools.partial(_out_proj_kernel, n_lat=n_lat, tm=tm, final=final)
    return pl.pallas_call(
        kern,
        grid=(nt // tm,),
        in_specs=[
            pl.BlockSpec((tm, half), lambda i: (i, cols[0])),
            pl.BlockSpec((tm, half), lambda i: (i, cols[1])),
            pl.BlockSpec((2 * half, d), lambda i: (0, 0)),
            pl.BlockSpec((tm, d), lambda i: (i, 0)),
            pl.BlockSpec((8, 3 * d), lambda i: (0, 0)),
            pl.BlockSpec((1, d), lambda i: (0, 0)),
        ],
        out_specs=pl.BlockSpec((tm, d), lambda i: (i, 0)),
        out_shape=jax.ShapeDtypeStruct((nt, d), F32),
        compiler_params=_cparams(1),
        name="out_proj_final" if final else "out_proj",
    )(mix_a, mix_b, w, xs, mod, fg)


def _rope_tables(n_lat, nc):
    t = jnp.arange(n_lat, dtype=jnp.int32)
    row = (t // GRID_W).astype(F32)
    col = (t % GRID_W).astype(F32)
    n_freq = HEAD_DIM // 4
    inv = ROPE_THETA ** (-jnp.arange(n_freq, dtype=F32) / n_freq)
    cr, sr = jnp.cos(row[:, None] * inv), jnp.sin(row[:, None] * inv)
    cc, sc = jnp.cos(col[:, None] * inv), jnp.sin(col[:, None] * inv)
    cos_t = jnp.concatenate([cr, cr, cc, cc], axis=-1)
    sin_t = jnp.concatenate([-sr, sr, -sc, sc], axis=-1)
    cos_t = jnp.concatenate([cos_t, jnp.ones((nc, HEAD_DIM), F32)], axis=0)
    sin_t = jnp.concatenate([sin_t, jnp.zeros((nc, HEAD_DIM), F32)], axis=0)
    return cos_t, sin_t


def _norm_rope(x, gain, cos_t, sin_t, first_half):
    ms = jnp.mean(x * x, axis=-1, keepdims=True)
    xn = x * lax.rsqrt(ms + NORM_EPS) * gain
    partner = jnp.where(first_half, pltpu.roll(xn, 3 * HEAD_DIM // 4, 1), pltpu.roll(xn, HEAD_DIM // 4, 1))
    return xn * cos_t + partner * sin_t


def _odd_proj_kernel(x_ref, g_ref, mod_ref, w_ref, cos_ref, sin_ref, qn_ref, kn_ref,
                     q_ref, kv_ref, gs_ref, h_scr, *, n_lat, tm):
    i = pl.program_id(0)
    j = pl.program_id(1)

    @pl.when(j == 0)
    def _():
        h_scr[...] = _prenorm(x_ref, g_ref, mod_ref, i, tm, n_lat)

    acc = _dot(h_scr[...], w_ref[...])
    n_heads = acc.shape[-1] // HEAD_DIM

    def rotated(gain_ref, out_ref, scale):
        cos_t = cos_ref[...]
        sin_t = sin_ref[...]
        lane = lax.broadcasted_iota(jnp.int32, (1, HEAD_DIM), 1)
        first_half = (lane % (HEAD_DIM // 2)) < (HEAD_DIM // 4)
        for hh in range(n_heads):
            sl = slice(hh * HEAD_DIM, (hh + 1) * HEAD_DIM)
            y = _norm_rope(acc[:, sl], gain_ref[...], cos_t, sin_t, first_half)
            out_ref[:, sl] = (y * scale).astype(BF16)

    @pl.when(j < 4)
    def _():
        rotated(qn_ref, q_ref, (HEAD_DIM ** -0.5) * LOG2E)

    @pl.when(j == 4)
    def _():
        rotated(kn_ref, kv_ref, 1.0)

    @pl.when(j == 5)
    def _():
        kv_ref[...] = acc.astype(BF16)

    @pl.when(j >= 6)
    def _():
        gs_ref[...] = _silu(acc).astype(BF16)


def _odd_proj(xs, g, mod, w, cos_t, sin_t, q_norm, k_norm, n_lat):
    nt, d = xs.shape
    tm, tn = TM_PROJ, GQA_KV_HEADS * HEAD_DIM
    n_steps = w.shape[1] // tn
    assert n_steps == 10 and nt % tm == 0
    kern = functools.partial(_odd_proj_kernel, n_lat=n_lat, tm=tm)
    return pl.pallas_call(
        kern,
        grid=(nt // tm, n_steps),
        in_specs=[
            pl.BlockSpec((tm, d), lambda i, j: (i, 0)),
            pl.BlockSpec((1, d), lambda i, j: (0, 0)),
            pl.BlockSpec((8, 3 * d), lambda i, j: (0, 0)),
            pl.BlockSpec((d, tn), lambda i, j: (0, j)),
            pl.BlockSpec((tm, HEAD_DIM), lambda i, j: (i, 0)),
            pl.BlockSpec((tm, HEAD_DIM), lambda i, j: (i, 0)),
            pl.BlockSpec((1, HEAD_DIM), lambda i, j: (0, 0)),
            pl.BlockSpec((1, HEAD_DIM), lambda i, j: (0, 0)),
        ],
        out_specs=[
            pl.BlockSpec((tm, tn), lambda i, j: (i, jnp.minimum(j, 3))),
            pl.BlockSpec((tm, tn), lambda i, j: (i, jnp.clip(j - 4, 0, 1))),
            pl.BlockSpec((tm, tn), lambda i, j: (i, jnp.clip(j - 6, 0, 3))),
        ],
        out_shape=[
            jax.ShapeDtypeStruct((nt, 4 * tn), BF16),
            jax.ShapeDtypeStruct((nt, 2 * tn), BF16),
            jax.ShapeDtypeStruct((nt, 4 * tn), BF16),
        ],
        scratch_shapes=[pltpu.VMEM((tm, d), BF16)],
        compiler_params=_cparams(2),
        name="odd_proj",
    )(xs, g, mod, w, cos_t, sin_t, q_norm.reshape(1, HEAD_DIM), k_norm.reshape(1, HEAD_DIM))


def _gqa_kernel(q_ref, k_ref, v_ref, g_ref, o_ref, *, n_lat, nc):
    b = pl.program_id(1)
    tq = q_ref.shape[0]
    q = jnp.concatenate([q_ref[:, h * HEAD_DIM:(h + 1) * HEAD_DIM] for h in range(GQA_GROUP)], axis=0)

    def step(carry, k_chunk, v_chunk):
        m, l, acc = carry
        s = _dot_nt(q, k_chunk)
        m_new = jnp.maximum(m, jnp.max(s, axis=-1, keepdims=True))
        alpha = jnp.exp2(m - m_new)
        p = jnp.exp2(s - m_new)
        l = alpha * l + jnp.sum(p, axis=-1, keepdims=True)
        acc = alpha * acc + _dot(p.astype(BF16), v_chunk)
        return m_new, l, acc

    rows = GQA_GROUP * tq
    init = (jnp.full((rows, 1), NEG_BIG, F32), jnp.zeros((rows, 1), F32), jnp.zeros((rows, HEAD_DIM), F32))
    carry = step(init, k_ref[n_lat:n_lat + nc, :], v_ref[n_lat:n_lat + nc, :])

    def body(c, carry):
        start = pl.multiple_of(c * GQA_TK, GQA_TK)
        return step(carry, k_ref[pl.ds(start, GQA_TK), :], v_ref[pl.ds(start, GQA_TK), :])

    n_chunks = jnp.where(b * tq < n_lat, n_lat // GQA_TK, 0)
    m, l, acc = lax.fori_loop(0, n_chunks, body, carry)
    o = acc / l
    for h in range(GQA_GROUP):
        sl = slice(h * HEAD_DIM, (h + 1) * HEAD_DIM)
        o_ref[:, sl] = (o[h * tq:(h + 1) * tq, :] * g_ref[:, sl].astype(F32)).astype(BF16)


def _gqa_attention(q, kv, gs, n_lat):
    nt = q.shape[0]
    nc = nt - n_lat
    tq = GQA_TQ
    gw = GQA_GROUP * HEAD_DIM
    assert nt % tq == 0 and n_lat % tq == 0 and n_lat % GQA_TK == 0
    kern = functools.partial(_gqa_kernel, n_lat=n_lat, nc=nc)
    return pl.pallas_call(
        kern,
        grid=(GQA_KV_HEADS, nt // tq),
        in_specs=[
            pl.BlockSpec((tq, gw), lambda h, b: (b, h)),
            pl.BlockSpec((nt, HEAD_DIM), lambda h, b: (0, h)),
            pl.BlockSpec((nt, HEAD_DIM), lambda h, b: (0, GQA_KV_HEADS + h)),
            pl.BlockSpec((tq, gw), lambda h, b: (b, h)),
        ],
        out_specs=pl.BlockSpec((tq, gw), lambda h, b: (b, h)),
        out_shape=jax.ShapeDtypeStruct((nt, GQA_HEADS * HEAD_DIM), BF16),
        compiler_params=_cparams(2),
        name="gqa_attention",
    )(q, kv, kv, gs)


def kernel(x, c, ctx, c_ctx, norm_g, w_mod, b_mod, e_w_in, e_rpb, e_dw_w, e_dw_b, e_ln_g, e_ln_b, e_w_out,
           o_w_in, o_q_norm, o_k_norm, o_w_out, final_norm_g):
    batch, n_lat, d = x.shape
    nc = ctx.shape[1]
    depth = norm_g.shape[0]
    assert batch == 1
    rows = n_lat // GRID_W

    xs = jnp.concatenate([x[0], ctx[0]], axis=0)
    cvec = jnp.zeros((8, d), F32).at[0].set(c[0]).at[1].set(c_ctx)
    mods = _modulation(cvec, w_mod, b_mod)
    cos_t, sin_t = _rope_tables(n_lat, nc)
    fg = final_norm_g.reshape(1, d)

    for l in range(depth):
        i = l // 2
        g = norm_g[l].reshape(1, d)
        mod = mods[l]
        final = l == depth - 1
        if l % 2 == 0:
            w = e_w_in[i]
            cw = w.shape[1] // 7
            hc = cw // 2
            w_perm = jnp.concatenate(
                [w[:, :4 * cw], w[:, 4 * cw:4 * cw + hc], w[:, 5 * cw:5 * cw + hc],
                 w[:, 4 * cw + hc:5 * cw], w[:, 5 * cw + hc:6 * cw], w[:, 6 * cw:]], axis=1).astype(BF16)
            qkv, ga, u, gb = _even_proj(xs, g, mod, w_perm, n_lat)
            mix_a = _na_attention(qkv, ga, _na_bias_table(e_rpb[i], rows), n_lat)
            mix_b = _conformer_conv(u, gb, e_dw_w[i], e_dw_b[i], e_ln_g[i], e_ln_b[i], n_lat)
            xs = _out_proj(mix_a, mix_b, (0, 0), e_w_out[i].astype(BF16), xs, mod, fg, n_lat, final)
        else:
            q, kv, gs = _odd_proj(xs, g, mod, o_w_in[i].astype(BF16), cos_t, sin_t,
                                  o_q_norm[i], o_k_norm[i], n_lat)
            mix = _gqa_attention(q, kv, gs, n_lat)
            xs = _out_proj(mix, mix, (0, 1), o_w_out[i].astype(BF16), xs, mod, fg, n_lat, final)
    return xs[:n_lat][None]
```

```python
import functools

import numpy as np
import jax
import jax.numpy as jnp
from jax import lax
from jax.experimental import pallas as pl
from jax.experimental.pallas import tpu as pltpu

F32 = jnp.float32
BF16 = jnp.bfloat16

GRID_W = 64
HEAD_DIM = 128
NORM_EPS = 1e-6
NA_HEADS = 8
NA_WIN_ROWS = 8
NA_WIN_COLS = 16
CONV_KERNEL = 31
GQA_HEADS = 16
GQA_KV_HEADS = 4
GQA_GROUP = GQA_HEADS // GQA_KV_HEADS
ROPE_THETA = 10000.0
LOG2E = 1.4426950408889634

LANES = 128
V7X_VMEM_BYTES = 64 * 1024 * 1024
VMEM_LIMIT = V7X_VMEM_BYTES - 8 * 1024 * 1024

TM_PROJ = 384
TM_OUT = 256
NA_ROWS = 4
NA_SLAB = NA_WIN_ROWS + NA_ROWS
NA_TQ = NA_ROWS * GRID_W
CONV_T = 128
CONV_HALO = 16
CONV_CH = 32
GQA_TQ = 256
GQA_TK = 2048
GQA_TK_ONLINE = 512
NEG_BIG = -1e30


def _cparams(n_axes):
    return pltpu.CompilerParams(dimension_semantics=("arbitrary",) * n_axes, vmem_limit_bytes=VMEM_LIMIT)


def _sigmoid(x):
    return 1.0 / (1.0 + jnp.exp(-x))


def _silu(x):
    return x * _sigmoid(x)


def _dot(a, b):
    return jnp.dot(a, b, preferred_element_type=F32)


def _dot_nt(a, b):
    return lax.dot_general(a, b, (((1,), (1,)), ((), ())), preferred_element_type=F32)


def _row_is_ctx(tile_idx, tm, n_lat):
    rows = tile_idx * tm + lax.broadcasted_iota(jnp.int32, (tm, 1), 0)
    return rows >= n_lat


def _mod_rows(mod_ref, part, is_ctx, d):
    lat = mod_ref[0:1, part * d:(part + 1) * d]
    ctx = mod_ref[1:2, part * d:(part + 1) * d]
    return jnp.where(is_ctx, ctx, lat)


def _prenorm(x_ref, g_ref, mod_ref, tile_idx, tm, n_lat):
    x = x_ref[...]
    d = x.shape[-1]
    ms = jnp.mean(x * x, axis=-1, keepdims=True)
    xn = x * lax.rsqrt(ms + NORM_EPS) * g_ref[...]
    is_ctx = _row_is_ctx(tile_idx, tm, n_lat)
    sh = _mod_rows(mod_ref, 0, is_ctx, d)
    sc = _mod_rows(mod_ref, 1, is_ctx, d)
    return (xn * (1.0 + sc) + sh).astype(BF16)


def _mod_kernel(c_ref, w_ref, b_ref, o_ref):
    s = _silu(c_ref[...]).astype(BF16)
    o_ref[...] = _dot(s, w_ref[...].astype(BF16)) + b_ref[...]


def _modulation(cvec, w_mod, b_mod):
    depth, d, d3 = w_mod.shape
    tn = 768
    return pl.pallas_call(
        _mod_kernel,
        grid=(depth, d3 // tn),
        in_specs=[
            pl.BlockSpec((8, d), lambda l, j: (0, 0)),
            pl.BlockSpec((None, d, tn), lambda l, j: (l, 0, j)),
            pl.BlockSpec((None, 1, tn), lambda l, j: (l, 0, j)),
        ],
        out_specs=pl.BlockSpec((None, 8, tn), lambda l, j: (l, 0, j)),
        out_shape=jax.ShapeDtypeStruct((depth, 8, d3), F32),
        compiler_params=_cparams(2),
        name="modulation",
    )(cvec, w_mod, b_mod.reshape(depth, 1, d3))


def _even_proj_kernel(x_ref, g_ref, mod_ref, w_ref, qkv_ref, ga_ref, u_ref, gb_ref, h_scr, *, n_lat, tm):
    i = pl.program_id(0)
    j = pl.program_id(1)

    @pl.when(j == 0)
    def _():
        h_scr[...] = _prenorm(x_ref, g_ref, mod_ref, i, tm, n_lat)

    acc = _dot(h_scr[...], w_ref[...])

    @pl.when(j < 3)
    def _():
        qkv_ref[...] = acc.astype(BF16)

    @pl.when(j == 3)
    def _():
        ga_ref[...] = _silu(acc).astype(BF16)

    @pl.when(jnp.logical_or(j == 4, j == 5))
    def _():
        half = acc.shape[-1] // 2
        u_ref[...] = acc[:, :half] * _sigmoid(acc[:, half:])

    @pl.when(j == 6)
    def _():
        gb_ref[...] = _silu(acc).astype(BF16)


def _even_proj(xs, g, mod, w_perm, n_lat):
    nt, d = xs.shape
    tm, tn = TM_PROJ, 1024
    n_steps = w_perm.shape[1] // tn
    assert n_steps == 7 and nt % tm == 0
    kern = functools.partial(_even_proj_kernel, n_lat=n_lat, tm=tm)
    return pl.pallas_call(
        kern,
        grid=(nt // tm, n_steps),
        in_specs=[
            pl.BlockSpec((tm, d), lambda i, j: (i, 0)),
            pl.BlockSpec((1, d), lambda i, j: (0, 0)),
            pl.BlockSpec((8, 3 * d), lambda i, j: (0, 0)),
            pl.BlockSpec((d, tn), lambda i, j: (0, j)),
        ],
        out_specs=[
            pl.BlockSpec((tm, tn), lambda i, j: (i, jnp.minimum(j, 2))),
            pl.BlockSpec((tm, tn), lambda i, j: (i, 0)),
            pl.BlockSpec((tm, tn // 2), lambda i, j: (i, jnp.clip(j - 4, 0, 1))),
            pl.BlockSpec((tm, tn), lambda i, j: (i, 0)),
        ],
        out_shape=[
            jax.ShapeDtypeStruct((nt, 3 * tn), BF16),
            jax.ShapeDtypeStruct((nt, tn), BF16),
            jax.ShapeDtypeStruct((nt, tn), F32),
            jax.ShapeDtypeStruct((nt, tn), BF16),
        ],
        scratch_shapes=[pltpu.VMEM((tm, d), BF16)],
        compiler_params=_cparams(2),
        name="even_proj",
    )(xs, g, mod, w_perm)


def _na_bias_table(rpb, rows):
    pad = GRID_W - NA_WIN_COLS
    rp = jnp.pad(rpb.astype(F32), ((0, 0), (0, 0), (pad, pad)))
    toe = jnp.stack([rp[:, :, GRID_W - 1 - qc:2 * GRID_W - 1 - qc] for qc in range(GRID_W)], axis=2)
    qc = np.arange(GRID_W)[:, None]
    kc = np.arange(GRID_W)[None, :]
    cs = np.clip(qc - NA_WIN_COLS // 2, 0, GRID_W - NA_WIN_COLS)
    col_ok = (kc >= cs) & (kc < cs + NA_WIN_COLS)
    toe = jnp.where(col_ok, toe, NEG_BIG)
    masked = jnp.full((rpb.shape[0], GRID_W, GRID_W), NEG_BIG, F32)
    pats = []
    for r0 in (0, 2 * NA_ROWS, rows - NA_ROWS):
        ks = int(np.clip(r0 - NA_WIN_ROWS // 2, 0, rows - NA_SLAB))
        q_blocks = []
        for r in range(r0, r0 + NA_ROWS):
            rs = int(np.clip(r - NA_WIN_ROWS // 2, 0, rows - NA_WIN_ROWS))
            k_blocks = [toe[:, kr - r + NA_WIN_ROWS - 1] if rs <= kr < rs + NA_WIN_ROWS else masked
                        for kr in range(ks, ks + NA_SLAB)]
            q_blocks.append(jnp.concatenate(k_blocks, axis=-1))
        pats.append(jnp.concatenate(q_blocks, axis=1))
    return jnp.stack(pats)


def _na_kernel(q_ref, k_ref, v_ref, bias_ref, ga_ref, o_ref, *, n_lat, nc, rows):
    b = pl.program_id(1)
    nb_lat = rows // NA_ROWS
    scale = HEAD_DIM ** -0.5
    q = q_ref[...]
    gate = ga_ref[...].astype(F32)
    k_ctx = k_ref[n_lat:n_lat + nc, :]
    v_ctx = v_ref[n_lat:n_lat + nc, :]
    s_ctx = _dot_nt(q, k_ctx) * scale
    m_ctx = jnp.max(s_ctx, axis=-1, keepdims=True)

    @pl.when(b < nb_lat)
    def _():
        ks = jnp.clip(b * NA_ROWS - NA_WIN_ROWS // 2, 0, rows - NA_SLAB)
        start = pl.multiple_of(ks * GRID_W, GRID_W)
        k_win = k_ref[pl.ds(start, NA_SLAB * GRID_W), :]
        v_win = v_ref[pl.ds(start, NA_SLAB * GRID_W), :]
        s_win = _dot_nt(q, k_win) * scale + bias_ref[...]
        m = jnp.maximum(jnp.max(s_win, axis=-1, keepdims=True), m_ctx)
        p_win = jnp.exp(s_win - m)
        p_ctx = jnp.exp(s_ctx - m)
        denom = jnp.sum(p_win, axis=-1, keepdims=True) + jnp.sum(p_ctx, axis=-1, keepdims=True)
        o = _dot(p_win.astype(BF16), v_win) + _dot(p_ctx.astype(BF16), v_ctx)
        o_ref[...] = (o / denom * gate).astype(BF16)

    @pl.when(b >= nb_lat)
    def _():
        p_ctx = jnp.exp(s_ctx - m_ctx)
        denom = jnp.sum(p_ctx, axis=-1, keepdims=True)
        o = _dot(p_ctx.astype(BF16), v_ctx)
        o_ref[...] = (o / denom * gate).astype(BF16)


def _na_attention(qkv, ga, bias_tab, n_lat):
    nt = qkv.shape[0]
    nc = nt - n_lat
    rows = n_lat // GRID_W
    assert nc == NA_TQ and rows % NA_ROWS == 0 and rows >= NA_SLAB
    nb = nt // NA_TQ
    nb_lat = rows // NA_ROWS
    slab = NA_SLAB * GRID_W

    def pat(b):
        return jnp.where(b == 0, 0, jnp.where(b >= nb_lat - 1, 2, 1))

    kern = functools.partial(_na_kernel, n_lat=n_lat, nc=nc, rows=rows)
    return pl.pallas_call(
        kern,
        grid=(NA_HEADS, nb),
        in_specs=[
            pl.BlockSpec((NA_TQ, HEAD_DIM), lambda h, b: (b, h)),
            pl.BlockSpec((nt, HEAD_DIM), lambda h, b: (0, NA_HEADS + h)),
            pl.BlockSpec((nt, HEAD_DIM), lambda h, b: (0, 2 * NA_HEADS + h)),
            pl.BlockSpec((None, None, NA_TQ, slab), lambda h, b: (pat(b), h, 0, 0)),
            pl.BlockSpec((NA_TQ, HEAD_DIM), lambda h, b: (b, h)),
        ],
        out_specs=pl.BlockSpec((NA_TQ, HEAD_DIM), lambda h, b: (b, h)),
        out_shape=jax.ShapeDtypeStruct((nt, NA_HEADS * HEAD_DIM), BF16),
        compiler_params=_cparams(2),
        name="na_attention",
    )(qkv, qkv, qkv, bias_tab, ga)


def _conv_kernel(um_ref, up_ref, un_ref, gb_ref, w_ref, b_ref, lg_ref, lb_ref, o_ref, ubuf, *, n_lat, nt):
    t = pl.program_id(0)
    start = t * CONV_T
    prev_ok = jnp.logical_and(start != 0, start != n_lat)
    next_ok = jnp.logical_and(start + CONV_T != n_lat, start + CONV_T != nt)
    ubuf[0:CONV_HALO, :] = jnp.where(prev_ok, up_ref[...], 0.0)
    ubuf[CONV_HALO:CONV_HALO + CONV_T, :] = um_ref[...]
    ubuf[CONV_HALO + CONV_T:, :] = jnp.where(next_ok, un_ref[...], 0.0)
    off = CONV_HALO - CONV_KERNEL // 2
    for r0 in range(0, CONV_T, CONV_CH):
        acc = jnp.zeros((CONV_CH, um_ref.shape[-1]), F32) + b_ref[...]
        for j in range(CONV_KERNEL):
            acc = acc + w_ref[j:j + 1, :] * ubuf[r0 + off + j:r0 + off + j + CONV_CH, :]
        mu = jnp.mean(acc, axis=-1, keepdims=True)
        xc = acc - mu
        var = jnp.mean(xc * xc, axis=-1, keepdims=True)
        y = xc * lax.rsqrt(var + NORM_EPS) * lg_ref[...] + lb_ref[...]
        y = _silu(y) * gb_ref[r0:r0 + CONV_CH, :].astype(F32)
        o_ref[r0:r0 + CONV_CH, :] = y.astype(BF16)


def _conformer_conv(u, gb, dw_w, dw_b, ln_g, ln_b, n_lat):
    nt, cw = u.shape
    assert nt % CONV_T == 0 and n_lat % CONV_T == 0
    hb = CONV_T // CONV_HALO
    n_halo_blocks = nt // CONV_HALO
    w_pad = jnp.zeros((32, cw), F32).at[:CONV_KERNEL].set(dw_w)
    kern = functools.partial(_conv_kernel, n_lat=n_lat, nt=nt)
    vec = lambda: pl.BlockSpec((1, cw), lambda t: (0, 0))
    return pl.pallas_call(
        kern,
        grid=(nt // CONV_T,),
        in_specs=[
            pl.BlockSpec((CONV_T, cw), lambda t: (t, 0)),
            pl.BlockSpec((CONV_HALO, cw), lambda t: (jnp.maximum(t * hb - 1, 0), 0)),
            pl.BlockSpec((CONV_HALO, cw), lambda t: (jnp.minimum((t + 1) * hb, n_halo_blocks - 1), 0)),
            pl.BlockSpec((CONV_T, cw), lambda t: (t, 0)),
            pl.BlockSpec((32, cw), lambda t: (0, 0)),
            vec(), vec(), vec(),
        ],
        out_specs=pl.BlockSpec((CONV_T, cw), lambda t: (t, 0)),
        out_shape=jax.ShapeDtypeStruct((nt, cw), BF16),
        scratch_shapes=[pltpu.VMEM((CONV_T + 2 * CONV_HALO, cw), F32)],
        compiler_params=_cparams(1),
        name="conformer_conv",
    )(u, u, u, gb, w_pad, dw_b.reshape(1, cw), ln_g.reshape(1, cw), ln_b.reshape(1, cw))


def _out_proj_kernel(a_ref, b_ref, w_ref, x_ref, mod_ref, fg_ref, o_ref, *, n_lat, tm, final):
    i = pl.program_id(0)
    half = a_ref.shape[-1]
    y = _dot(a_ref[...], w_ref[0:half, :]) + _dot(b_ref[...], w_ref[half:, :])
    d = y.shape[-1]
    gt = _mod_rows(mod_ref, 2, _row_is_ctx(i, tm, n_lat), d)
    xn = x_ref[...] + gt * y
    if final:
        ms = jnp.mean(xn * xn, axis=-1, keepdims=True)
        xn = xn * lax.rsqrt(ms + NORM_EPS) * fg_ref[...]
    o_ref[...] = xn


def _out_proj(mix_a, mix_b, cols, w, xs, mod, fg, n_lat, final):
    nt, d = xs.shape
    tm = TM_OUT
    half = w.shape[0] // 2
    kern = functools.partial(_out_proj_kernel, n_lat=n_lat, tm=tm, final=final)
    return pl.pallas_call(
        kern,
        grid=(nt // tm,),
        in_specs=[
            pl.BlockSpec((tm, half), lambda i: (i, cols[0])),
            pl.BlockSpec((tm, half), lambda i: (i, cols[1])),
            pl.BlockSpec((2 * half, d), lambda i: (0, 0)),
            pl.BlockSpec((tm, d), lambda i: (i, 0)),
            pl.BlockSpec((8, 3 * d), lambda i: (0, 0)),
            pl.BlockSpec((1, d), lambda i: (0, 0)),
        ],
        out_specs=pl.BlockSpec((tm, d), lambda i: (i, 0)),
        out_shape=jax.ShapeDtypeStruct((nt, d), F32),
        compiler_params=_cparams(1),
        name="out_proj_final" if final else "out_proj",
    )(mix_a, mix_b, w, xs, mod, fg)


def _rope_tables(n_lat, nc):
    t = jnp.arange(n_lat, dtype=jnp.int32)
    row = (t // GRID_W).astype(F32)
    col = (t % GRID_W).astype(F32)
    n_freq = HEAD_DIM // 4
    inv = ROPE_THETA ** (-jnp.arange(n_freq, dtype=F32) / n_freq)
    cr, sr = jnp.cos(row[:, None] * inv), jnp.sin(row[:, None] * inv)
    cc, sc = jnp.cos(col[:, None] * inv), jnp.sin(col[:, None] * inv)
    cos_t = jnp.concatenate([cr, cr, cc, cc], axis=-1)
    sin_t = jnp.concatenate([-sr, sr, -sc, sc], axis=-1)
    cos_t = jnp.concatenate([cos_t, jnp.ones((nc, HEAD_DIM), F32)], axis=0)
    sin_t = jnp.concatenate([sin_t, jnp.zeros((nc, HEAD_DIM), F32)], axis=0)
    return cos_t, sin_t


def _norm_rope(x, gain, cos_t, sin_t, first_half):
    ms = jnp.mean(x * x, axis=-1, keepdims=True)
    xn = x * lax.rsqrt(ms + NORM_EPS) * gain
    partner = jnp.where(first_half, pltpu.roll(xn, 3 * HEAD_DIM // 4, 1), pltpu.roll(xn, HEAD_DIM // 4, 1))
    return xn * cos_t + partner * sin_t


def _odd_proj_kernel(x_ref, g_ref, mod_ref, w_ref, cos_ref, sin_ref, qn_ref, kn_ref,
                     q_ref, kv_ref, gs_ref, h_scr, *, n_lat, tm):
    i = pl.program_id(0)
    j = pl.program_id(1)

    @pl.when(j == 0)
    def _():
        h_scr[...] = _prenorm(x_ref, g_ref, mod_ref, i, tm, n_lat)

    acc = _dot(h_scr[...], w_ref[...])
    n_heads = acc.shape[-1] // HEAD_DIM

    def rotated(gain_ref, out_ref, scale):
        cos_t = cos_ref[...]
        sin_t = sin_ref[...]
        lane = lax.broadcasted_iota(jnp.int32, (1, HEAD_DIM), 1)
        first_half = (lane % (HEAD_DIM // 2)) < (HEAD_DIM // 4)
        for hh in range(n_heads):
            sl = slice(hh * HEAD_DIM, (hh + 1) * HEAD_DIM)
            y = _norm_rope(acc[:, sl], gain_ref[...], cos_t, sin_t, first_half)
            out_ref[:, sl] = (y * scale).astype(BF16)

    @pl.when(j < 4)
    def _():
        rotated(qn_ref, q_ref, (HEAD_DIM ** -0.5) * LOG2E)

    @pl.when(j == 4)
    def _():
        rotated(kn_ref, kv_ref, 1.0)

    @pl.when(j == 5)
    def _():
        kv_ref[...] = acc.astype(BF16)

    @pl.when(j >= 6)
    def _():
        gs_ref[...] = _silu(acc).astype(BF16)


def _odd_proj(xs, g, mod, w, cos_t, sin_t, q_norm, k_norm, n_lat):
    nt, d = xs.shape
    tm, tn = TM_PROJ, GQA_KV_HEADS * HEAD_DIM
    n_steps = w.shape[1] // tn
    assert n_steps == 10 and nt % tm == 0
    kern = functools.partial(_odd_proj_kernel, n_lat=n_lat, tm=tm)
    return pl.pallas_call(
        kern,
        grid=(nt // tm, n_steps),
        in_specs=[
            pl.BlockSpec((tm, d), lambda i, j: (i, 0)),
            pl.BlockSpec((1, d), lambda i, j: (0, 0)),
            pl.BlockSpec((8, 3 * d), lambda i, j: (0, 0)),
            pl.BlockSpec((d, tn), lambda i, j: (0, j)),
            pl.BlockSpec((tm, HEAD_DIM), lambda i, j: (i, 0)),
            pl.BlockSpec((tm, HEAD_DIM), lambda i, j: (i, 0)),
            pl.BlockSpec((1, HEAD_DIM), lambda i, j: (0, 0)),
            pl.BlockSpec((1, HEAD_DIM), lambda i, j: (0, 0)),
        ],
        out_specs=[
            pl.BlockSpec((tm, tn), lambda i, j: (i, jnp.minimum(j, 3))),
            pl.BlockSpec((tm, tn), lambda i, j: (i, jnp.clip(j - 4, 0, 1))),
            pl.BlockSpec((tm, tn), lambda i, j: (i, jnp.clip(j - 6, 0, 3))),
        ],
        out_shape=[
            jax.ShapeDtypeStruct((nt, 4 * tn), BF16),
            jax.ShapeDtypeStruct((nt, 2 * tn), BF16),
            jax.ShapeDtypeStruct((nt, 4 * tn), BF16),
        ],
        scratch_shapes=[pltpu.VMEM((tm, d), BF16)],
        compiler_params=_cparams(2),
        name="odd_proj",
    )(xs, g, mod, w, cos_t, sin_t, q_norm.reshape(1, HEAD_DIM), k_norm.reshape(1, HEAD_DIM))


GQA_SAFE_BOUND = 60.0


def _gqa_kernel(bound_ref, q_ref, k_ref, v_ref, g_ref, o_ref, acc_ref, *, n_lat, nc):
    b = pl.program_id(1)
    tq = q_ref.shape[0]
    rows = GQA_GROUP * tq
    bound = bound_ref[0]
    is_lat = b * tq < n_lat
    q = jnp.concatenate([q_ref[:, h * HEAD_DIM:(h + 1) * HEAD_DIM] for h in range(GQA_GROUP)], axis=0)

    def with_ones(v_chunk):
        return jnp.concatenate([v_chunk, jnp.ones_like(v_chunk)], axis=1)

    def write_out(o):
        for h in range(GQA_GROUP):
            sl = slice(h * HEAD_DIM, (h + 1) * HEAD_DIM)
            o_ref[:, sl] = (o[h * tq:(h + 1) * tq, :] * g_ref[:, sl].astype(F32)).astype(BF16)

    @pl.when(bound <= GQA_SAFE_BOUND)
    def _():
        def chunk(k_chunk, v_chunk, first):
            p = jnp.exp2(_dot_nt(q, k_chunk) - bound).astype(BF16)
            pv = _dot(p, with_ones(v_chunk))
            if first:
                acc_ref[...] = pv
            else:
                acc_ref[...] += pv

        chunk(k_ref[n_lat:n_lat + nc, :], v_ref[n_lat:n_lat + nc, :], True)

        def body(c, _):
            start = pl.multiple_of(c * GQA_TK, GQA_TK)
            chunk(k_ref[pl.ds(start, GQA_TK), :], v_ref[pl.ds(start, GQA_TK), :], False)
            return 0

        lax.fori_loop(0, jnp.where(is_lat, n_lat // GQA_TK, 0), body, 0)
        acc = acc_ref[...]
        write_out(acc[:, :HEAD_DIM] / acc[:, HEAD_DIM:])

    @pl.when(bound > GQA_SAFE_BOUND)
    def _():
        def step(carry, k_chunk, v_chunk):
            m, l, acc = carry
            s = _dot_nt(q, k_chunk)
            m_new = jnp.maximum(m, jnp.max(s, axis=-1, keepdims=True))
            alpha = jnp.exp2(m - m_new)
            p = jnp.exp2(s - m_new)
            l = alpha * l + jnp.sum(p, axis=-1, keepdims=True)
            acc = alpha * acc + _dot(p.astype(BF16), v_chunk)
            return m_new, l, acc

        init = (jnp.full((rows, 1), NEG_BIG, F32), jnp.zeros((rows, 1), F32), jnp.zeros((rows, HEAD_DIM), F32))
        carry = step(init, k_ref[n_lat:n_lat + nc, :], v_ref[n_lat:n_lat + nc, :])

        def body(c, carry):
            start = pl.multiple_of(c * GQA_TK_ONLINE, GQA_TK_ONLINE)
            return step(carry, k_ref[pl.ds(start, GQA_TK_ONLINE), :], v_ref[pl.ds(start, GQA_TK_ONLINE), :])

        m, l, acc = lax.fori_loop(0, jnp.where(is_lat, n_lat // GQA_TK_ONLINE, 0), body, carry)
        write_out(acc / l)


def _gqa_attention(q, kv, gs, q_norm, k_norm, n_lat):
    nt = q.shape[0]
    nc = nt - n_lat
    tq = GQA_TQ
    gw = GQA_GROUP * HEAD_DIM
    assert nt % tq == 0 and n_lat % tq == 0 and n_lat % GQA_TK == 0 and n_lat % GQA_TK_ONLINE == 0
    bound = (1.01 * HEAD_DIM * (HEAD_DIM ** -0.5) * LOG2E) * jnp.max(jnp.abs(q_norm)) * jnp.max(jnp.abs(k_norm))
    kern = functools.partial(_gqa_kernel, n_lat=n_lat, nc=nc)
    return pl.pallas_call(
        kern,
        grid=(GQA_KV_HEADS, nt // tq),
        in_specs=[
            pl.BlockSpec(memory_space=pltpu.SMEM),
            pl.BlockSpec((tq, gw), lambda h, b: (b, h)),
            pl.BlockSpec((nt, HEAD_DIM), lambda h, b: (0, h)),
            pl.BlockSpec((nt, HEAD_DIM), lambda h, b: (0, GQA_KV_HEADS + h)),
            pl.BlockSpec((tq, gw), lambda h, b: (b, h)),
        ],
        out_specs=pl.BlockSpec((tq, gw), lambda h, b: (b, h)),
        out_shape=jax.ShapeDtypeStruct((nt, GQA_HEADS * HEAD_DIM), BF16),
        scratch_shapes=[pltpu.VMEM((GQA_GROUP * tq, 2 * HEAD_DIM), F32)],
        compiler_params=_cparams(2),
        name="gqa_attention",
    )(bound.reshape(1).astype(F32), q, kv, kv, gs)


def kernel(x, c, ctx, c_ctx, norm_g, w_mod, b_mod, e_w_in, e_rpb, e_dw_w, e_dw_b, e_ln_g, e_ln_b, e_w_out,
           o_w_in, o_q_norm, o_k_norm, o_w_out, final_norm_g):
    batch, n_lat, d = x.shape
    nc = ctx.shape[1]
    depth = norm_g.shape[0]
    assert batch == 1
    rows = n_lat // GRID_W

    xs = jnp.concatenate([x[0], ctx[0]], axis=0)
    cvec = jnp.zeros((8, d), F32).at[0].set(c[0]).at[1].set(c_ctx)
    mods = _modulation(cvec, w_mod, b_mod)
    cos_t, sin_t = _rope_tables(n_lat, nc)
    fg = final_norm_g.reshape(1, d)

    for l in range(depth):
        i = l // 2
        g = norm_g[l].reshape(1, d)
        mod = mods[l]
        final = l == depth - 1
        if l % 2 == 0:
            w = e_w_in[i]
            cw = w.shape[1] // 7
            hc = cw // 2
            w_perm = jnp.concatenate(
                [w[:, :4 * cw], w[:, 4 * cw:4 * cw + hc], w[:, 5 * cw:5 * cw + hc],
                 w[:, 4 * cw + hc:5 * cw], w[:, 5 * cw + hc:6 * cw], w[:, 6 * cw:]], axis=1).astype(BF16)
            qkv, ga, u, gb = _even_proj(xs, g, mod, w_perm, n_lat)
            mix_a = _na_attention(qkv, ga, _na_bias_table(e_rpb[i], rows), n_lat)
            mix_b = _conformer_conv(u, gb, e_dw_w[i], e_dw_b[i], e_ln_g[i], e_ln_b[i], n_lat)
            xs = _out_proj(mix_a, mix_b, (0, 0), e_w_out[i].astype(BF16), xs, mod, fg, n_lat, final)
        else:
            q, kv, gs = _odd_proj(xs, g, mod, o_w_in[i].astype(BF16), cos_t, sin_t,
                                  o_q_norm[i], o_k_norm[i], n_lat)
            mix = _gqa_attention(q, kv, gs, o_q_norm[i], o_k_norm[i], n_lat)
            xs = _out_proj(mix, mix, (0, 1), o_w_out[i].astype(BF16), xs, mod, fg, n_lat, final)
    return xs[:n_lat][None]
```

```python
import functools

import numpy as np
import jax
import jax.numpy as jnp
from jax import lax
from jax.experimental import pallas as pl
from jax.experimental.pallas import tpu as pltpu

F32 = jnp.float32
BF16 = jnp.bfloat16

GRID_W = 64
HEAD_DIM = 128
NORM_EPS = 1e-6
NA_HEADS = 8
NA_WIN_ROWS = 8
NA_WIN_COLS = 16
CONV_KERNEL = 31
GQA_HEADS = 16
GQA_KV_HEADS = 4
GQA_GROUP = GQA_HEADS // GQA_KV_HEADS
ROPE_THETA = 10000.0
LOG2E = 1.4426950408889634

LANES = 128
V7X_VMEM_BYTES = 64 * 1024 * 1024
VMEM_LIMIT = V7X_VMEM_BYTES - 8 * 1024 * 1024

TM_PROJ = 384
TM_OUT = 256
NA_ROWS = 4
NA_SLAB = NA_WIN_ROWS + NA_ROWS
NA_TQ = NA_ROWS * GRID_W
NA_HP = 2
CONV_T = 128
CONV_HALO = 16
CONV_CH = 32
GQA_TQ = 256
GQA_TK = 2048
GQA_TK_ONLINE = 512
NEG_BIG = -1e30


def _cparams(n_axes):
    return pltpu.CompilerParams(dimension_semantics=("arbitrary",) * n_axes, vmem_limit_bytes=VMEM_LIMIT)


def _sigmoid(x):
    return 1.0 / (1.0 + jnp.exp(-x))


def _silu(x):
    return x * _sigmoid(x)


def _dot(a, b):
    return jnp.dot(a, b, preferred_element_type=F32)


def _dot_nt(a, b):
    return lax.dot_general(a, b, (((1,), (1,)), ((), ())), preferred_element_type=F32)


def _row_is_ctx(tile_idx, tm, n_lat):
    rows = tile_idx * tm + lax.broadcasted_iota(jnp.int32, (tm, 1), 0)
    return rows >= n_lat


def _mod_rows(mod_ref, part, is_ctx, d):
    lat = mod_ref[0:1, part * d:(part + 1) * d]
    ctx = mod_ref[1:2, part * d:(part + 1) * d]
    return jnp.where(is_ctx, ctx, lat)


def _prenorm_into(h_ref, x_ref, g_ref, mod_ref, tile_idx, tm, n_lat):
    d = x_ref.shape[-1]

    def normed():
        x = x_ref[...]
        ms = jnp.mean(x * x, axis=-1, keepdims=True)
        return x * lax.rsqrt(ms + NORM_EPS)

    all_lat = (tile_idx + 1) * tm <= n_lat

    @pl.when(all_lat)
    def _():
        a = g_ref[...] * (1.0 + mod_ref[0:1, d:2 * d])
        h_ref[...] = (normed() * a + mod_ref[0:1, 0:d]).astype(BF16)

    @pl.when(jnp.logical_not(all_lat))
    def _():
        is_ctx = _row_is_ctx(tile_idx, tm, n_lat)
        sh = _mod_rows(mod_ref, 0, is_ctx, d)
        sc = _mod_rows(mod_ref, 1, is_ctx, d)
        h_ref[...] = (normed() * g_ref[...] * (1.0 + sc) + sh).astype(BF16)


def _mod_kernel(c_ref, w_ref, b_ref, o_ref):
    s = _silu(c_ref[...]).astype(BF16)
    o_ref[...] = _dot(s, w_ref[...].astype(BF16)) + b_ref[...]


def _modulation(cvec, w_mod, b_mod):
    depth, d, d3 = w_mod.shape
    tn = 768
    return pl.pallas_call(
        _mod_kernel,
        grid=(depth, d3 // tn),
        in_specs=[
            pl.BlockSpec((8, d), lambda l, j: (0, 0)),
            pl.BlockSpec((None, d, tn), lambda l, j: (l, 0, j)),
            pl.BlockSpec((None, 1, tn), lambda l, j: (l, 0, j)),
        ],
        out_specs=pl.BlockSpec((None, 8, tn), lambda l, j: (l, 0, j)),
        out_shape=jax.ShapeDtypeStruct((depth, 8, d3), F32),
        compiler_params=_cparams(2),
        name="modulation",
    )(cvec, w_mod, b_mod.reshape(depth, 1, d3))


def _even_proj_kernel(x_ref, g_ref, mod_ref, w_ref, qkv_ref, ga_ref, u_ref, gb_ref, h_scr, *, n_lat, tm):
    i = pl.program_id(0)
    j = pl.program_id(1)

    @pl.when(j == 0)
    def _():
        _prenorm_into(h_scr, x_ref, g_ref, mod_ref, i, tm, n_lat)

    def proj():
        return _dot(h_scr[...], w_ref[...])

    @pl.when(j < 3)
    def _():
        qkv_ref[...] = proj().astype(BF16)

    @pl.when(j == 3)
    def _():
        ga_ref[...] = _silu(proj()).astype(BF16)

    @pl.when(jnp.logical_or(j == 4, j == 5))
    def _():
        acc = proj()
        half = acc.shape[-1] // 2
        u_ref[...] = acc[:, :half] * _sigmoid(acc[:, half:])

    @pl.when(j == 6)
    def _():
        gb_ref[...] = _silu(proj()).astype(BF16)


def _even_proj(xs, g, mod, w_perm, n_lat):
    nt, d = xs.shape
    tm, tn = TM_PROJ, 1024
    n_steps = w_perm.shape[1] // tn
    assert n_steps == 7 and nt % tm == 0
    kern = functools.partial(_even_proj_kernel, n_lat=n_lat, tm=tm)
    return pl.pallas_call(
        kern,
        grid=(nt // tm, n_steps),
        in_specs=[
            pl.BlockSpec((tm, d), lambda i, j: (i, 0)),
            pl.BlockSpec((1, d), lambda i, j: (0, 0)),
            pl.BlockSpec((8, 3 * d), lambda i, j: (0, 0)),
            pl.BlockSpec((d, tn), lambda i, j: (0, j)),
        ],
        out_specs=[
            pl.BlockSpec((tm, tn), lambda i, j: (i, jnp.minimum(j, 2))),
            pl.BlockSpec((tm, tn), lambda i, j: (i, 0)),
            pl.BlockSpec((tm, tn // 2), lambda i, j: (i, jnp.clip(j - 4, 0, 1))),
            pl.BlockSpec((tm, tn), lambda i, j: (i, 0)),
        ],
        out_shape=[
            jax.ShapeDtypeStruct((nt, 3 * tn), BF16),
            jax.ShapeDtypeStruct((nt, tn), BF16),
            jax.ShapeDtypeStruct((nt, tn), F32),
            jax.ShapeDtypeStruct((nt, tn), BF16),
        ],
        scratch_shapes=[pltpu.VMEM((tm, d), BF16)],
        compiler_params=_cparams(2),
        name="even_proj",
    )(xs, g, mod, w_perm)


def _na_bias_table(rpb, rows, nc):
    pad = GRID_W - NA_WIN_COLS
    rp = jnp.pad(rpb.astype(F32), ((0, 0), (0, 0), (pad, pad)))
    toe = jnp.stack([rp[:, :, GRID_W - 1 - qc:2 * GRID_W - 1 - qc] for qc in range(GRID_W)], axis=2)
    qc = np.arange(GRID_W)[:, None]
    kc = np.arange(GRID_W)[None, :]
    cs = np.clip(qc - NA_WIN_COLS // 2, 0, GRID_W - NA_WIN_COLS)
    col_ok = (kc >= cs) & (kc < cs + NA_WIN_COLS)
    toe = jnp.where(col_ok, toe, NEG_BIG)
    masked = jnp.full((rpb.shape[0], GRID_W, GRID_W), NEG_BIG, F32)
    pats = []
    for r0 in (0, 2 * NA_ROWS, rows - NA_ROWS):
        ks = int(np.clip(r0 - NA_WIN_ROWS // 2, 0, rows - NA_SLAB))
        q_blocks = []
        for r in range(r0, r0 + NA_ROWS):
            rs = int(np.clip(r - NA_WIN_ROWS // 2, 0, rows - NA_WIN_ROWS))
            k_blocks = [toe[:, kr - r + NA_WIN_ROWS - 1] if rs <= kr < rs + NA_WIN_ROWS else masked
                        for kr in range(ks, ks + NA_SLAB)]
            q_blocks.append(jnp.concatenate(k_blocks, axis=-1))
        pat = jnp.concatenate(q_blocks, axis=1)
        pats.append(jnp.concatenate([pat, jnp.zeros(pat.shape[:2] + (nc,), F32)], axis=-1))
    return jnp.stack(pats)


def _na_kernel(q_ref, k_ref, v_ref, bias_ref, ga_ref, o_ref, *, n_lat, nc, rows):
    b = pl.program_id(1)
    nb_lat = rows // NA_ROWS
    scale = HEAD_DIM ** -0.5

    def attend(h, keys, values, bias):
        sl = slice(h * HEAD_DIM, (h + 1) * HEAD_DIM)
        s = _dot_nt(q_ref[:, sl], keys) * scale
        if bias is not None:
            s = s + bias
        p = jnp.exp(s - jnp.max(s, axis=-1, keepdims=True)).astype(BF16)
        o = _dot(p, jnp.concatenate([values, jnp.ones_like(values)], axis=1))
        gate = ga_ref[:, sl].astype(F32)
        o_ref[:, sl] = (o[:, :HEAD_DIM] / o[:, HEAD_DIM:] * gate).astype(BF16)

    @pl.when(b < nb_lat)
    def _():
        ks = jnp.clip(b * NA_ROWS - NA_WIN_ROWS // 2, 0, rows - NA_SLAB)
        start = pl.multiple_of(ks * GRID_W, GRID_W)
        for h in range(NA_HP):
            sl = slice(h * HEAD_DIM, (h + 1) * HEAD_DIM)
            keys = jnp.concatenate([k_ref[pl.ds(start, NA_SLAB * GRID_W), sl], k_ref[n_lat:n_lat + nc, sl]], axis=0)
            values = jnp.concatenate([v_ref[pl.ds(start, NA_SLAB * GRID_W), sl], v_ref[n_lat:n_lat + nc, sl]], axis=0)
            attend(h, keys, values, bias_ref[h])

    @pl.when(b >= nb_lat)
    def _():
        for h in range(NA_HP):
            sl = slice(h * HEAD_DIM, (h + 1) * HEAD_DIM)
            attend(h, k_ref[n_lat:n_lat + nc, sl], v_ref[n_lat:n_lat + nc, sl], None)


def _na_attention(qkv, ga, bias_tab, n_lat):
    nt = qkv.shape[0]
    nc = nt - n_lat
    rows = n_lat // GRID_W
    assert nc == NA_TQ and rows % NA_ROWS == 0 and rows >= NA_SLAB
    nb = nt // NA_TQ
    nb_lat = rows // NA_ROWS
    n_keys = NA_SLAB * GRID_W + nc
    hw = NA_HP * HEAD_DIM
    groups = NA_HEADS // NA_HP

    def pat(b):
        return jnp.where(b == 0, 0, jnp.where(b >= nb_lat - 1, 2, 1))

    kern = functools.partial(_na_kernel, n_lat=n_lat, nc=nc, rows=rows)
    return pl.pallas_call(
        kern,
        grid=(groups, nb),
        in_specs=[
            pl.BlockSpec((NA_TQ, hw), lambda h, b: (b, h)),
            pl.BlockSpec((nt, hw), lambda h, b: (0, groups + h)),
            pl.BlockSpec((nt, hw), lambda h, b: (0, 2 * groups + h)),
            pl.BlockSpec((None, NA_HP, NA_TQ, n_keys), lambda h, b: (pat(b), h, 0, 0)),
            pl.BlockSpec((NA_TQ, hw), lambda h, b: (b, h)),
        ],
        out_specs=pl.BlockSpec((NA_TQ, hw), lambda h, b: (b, h)),
        out_shape=jax.ShapeDtypeStruct((nt, NA_HEADS * HEAD_DIM), BF16),
        compiler_params=_cparams(2),
        name="na_attention",
    )(qkv, qkv, qkv, bias_tab, ga)


def _conv_kernel(um_ref, up_ref, un_ref, gb_ref, w_ref, b_ref, lg_ref, lb_ref, o_ref, ubuf, shifted, *, n_lat, nt):
    t = pl.program_id(0)
    start = t * CONV_T
    prev_ok = jnp.logical_and(start != 0, start != n_lat)
    next_ok = jnp.logical_and(start + CONV_T != n_lat, start + CONV_T != nt)
    ubuf[0:CONV_HALO, :] = jnp.where(prev_ok, up_ref[...], 0.0)
    ubuf[CONV_HALO:CONV_HALO + CONV_T, :] = um_ref[...]
    ubuf[CONV_HALO + CONV_T:, :] = jnp.where(next_ok, un_ref[...], 0.0)
    span = CONV_T + 2 * CONV_HALO - 8
    for b in range(1, 8):
        shifted[b - 1, :, :] = ubuf[b:b + span, :]
    off = CONV_HALO - CONV_KERNEL // 2
    for r0 in range(0, CONV_T, CONV_CH):
        acc = jnp.zeros((CONV_CH, um_ref.shape[-1]), F32) + b_ref[...]
        for j in range(CONV_KERNEL):
            a, b = divmod(off + j, 8)
            rows = slice(r0 + 8 * a, r0 + 8 * a + CONV_CH)
            tap = ubuf[rows, :] if b == 0 else shifted[b - 1, rows, :]
            acc = acc + w_ref[j:j + 1, :] * tap
        mu = jnp.mean(acc, axis=-1, keepdims=True)
        xc = acc - mu
        var = jnp.mean(xc * xc, axis=-1, keepdims=True)
        y = xc * lax.rsqrt(var + NORM_EPS) * lg_ref[...] + lb_ref[...]
        y = _silu(y) * gb_ref[r0:r0 + CONV_CH, :].astype(F32)
        o_ref[r0:r0 + CONV_CH, :] = y.astype(BF16)


def _conformer_conv(u, gb, dw_w, dw_b, ln_g, ln_b, n_lat):
    nt, cw = u.shape
    assert nt % CONV_T == 0 and n_lat % CONV_T == 0
    hb = CONV_T // CONV_HALO
    n_halo_blocks = nt // CONV_HALO
    w_pad = jnp.zeros((32, cw), F32).at[:CONV_KERNEL].set(dw_w)
    kern = functools.partial(_conv_kernel, n_lat=n_lat, nt=nt)
    vec = lambda: pl.BlockSpec((1, cw), lambda t: (0, 0))
    return pl.pallas_call(
        kern,
        grid=(nt // CONV_T,),
        in_specs=[
            pl.BlockSpec((CONV_T, cw), lambda t: (t, 0)),
            pl.BlockSpec((CONV_HALO, cw), lambda t: (jnp.maximum(t * hb - 1, 0), 0)),
            pl.BlockSpec((CONV_HALO, cw), lambda t: (jnp.minimum((t + 1) * hb, n_halo_blocks - 1), 0)),
            pl.BlockSpec((CONV_T, cw), lambda t: (t, 0)),
            pl.BlockSpec((32, cw), lambda t: (0, 0)),
            vec(), vec(), vec(),
        ],
        out_specs=pl.BlockSpec((CONV_T, cw), lambda t: (t, 0)),
        out_shape=jax.ShapeDtypeStruct((nt, cw), BF16),
        scratch_shapes=[pltpu.VMEM((CONV_T + 2 * CONV_HALO, cw), F32),
                        pltpu.VMEM((7, CONV_T + 2 * CONV_HALO - 8, cw), F32)],
        compiler_params=_cparams(1),
        name="conformer_conv",
    )(u, u, u, gb, w_pad, dw_b.reshape(1, cw), ln_g.reshape(1, cw), ln_b.reshape(1, cw))


def _out_proj_kernel(a_ref, b_ref, w_ref, x_ref, mod_ref, fg_ref, o_ref, *, n_lat, tm, final):
    i = pl.program_id(0)
    half = a_ref.shape[-1]
    y = _dot(a_ref[...], w_ref[0:half, :]) + _dot(b_ref[...], w_ref[half:, :])
    d = y.shape[-1]
    gt = _mod_rows(mod_ref, 2, _row_is_ctx(i, tm, n_lat), d)
    xn = x_ref[...] + gt * y
    if final:
        ms = jnp.mean(xn * xn, axis=-1, keepdims=True)
        xn = xn * lax.rsqrt(ms + NORM_EPS) * fg_ref[...]
    o_ref[...] = xn


def _out_proj(mix_a, mix_b, cols, w, xs, mod, fg, n_lat, final):
    nt, d = xs.shape
    tm = TM_OUT
    half = w.shape[0] // 2
    kern = functools.partial(_out_proj_kernel, n_lat=n_lat, tm=tm, final=final)
    return pl.pallas_call(
        kern,
        grid=(nt // tm,),
        in_specs=[
            pl.BlockSpec((tm, half), lambda i: (i, cols[0])),
            pl.BlockSpec((tm, half), lambda i: (i, cols[1])),
            pl.BlockSpec((2 * half, d), lambda i: (0, 0)),
            pl.BlockSpec((tm, d), lambda i: (i, 0)),
            pl.BlockSpec((8, 3 * d), lambda i: (0, 0)),
            pl.BlockSpec((1, d), lambda i: (0, 0)),
        ],
        out_specs=pl.BlockSpec((tm, d), lambda i: (i, 0)),
        out_shape=jax.ShapeDtypeStruct((nt, d), F32),
        compiler_params=_cparams(1),
        name="out_proj_final" if final else "out_proj",
    )(mix_a, mix_b, w, xs, mod, fg)


def _rope_tables(n_lat, nc):
    t = jnp.arange(n_lat, dtype=jnp.int32)
    row = (t // GRID_W).astype(F32)
    col = (t % GRID_W).astype(F32)
    n_freq = HEAD_DIM // 4
    inv = ROPE_THETA ** (-jnp.arange(n_freq, dtype=F32) / n_freq)
    cr, sr = jnp.cos(row[:, None] * inv), jnp.sin(row[:, None] * inv)
    cc, sc = jnp.cos(col[:, None] * inv), jnp.sin(col[:, None] * inv)
    cos_t = jnp.concatenate([cr, cr, cc, cc], axis=-1)
    sin_t = jnp.concatenate([-sr, sr, -sc, sc], axis=-1)
    cos_t = jnp.concatenate([cos_t, jnp.ones((nc, HEAD_DIM), F32)], axis=0)
    sin_t = jnp.concatenate([sin_t, jnp.zeros((nc, HEAD_DIM), F32)], axis=0)
    return cos_t, sin_t


def _norm_rope(x, gain, cos_t, sin_t, first_half):
    ms = jnp.mean(x * x, axis=-1, keepdims=True)
    xn = x * lax.rsqrt(ms + NORM_EPS) * gain
    partner = jnp.where(first_half, pltpu.roll(xn, 3 * HEAD_DIM // 4, 1), pltpu.roll(xn, HEAD_DIM // 4, 1))
    return xn * cos_t + partner * sin_t


def _odd_proj_kernel(x_ref, g_ref, mod_ref, w_ref, cos_ref, sin_ref, qn_ref, kn_ref,
                     q_ref, kv_ref, gs_ref, h_scr, *, n_lat, tm):
    i = pl.program_id(0)
    j = pl.program_id(1)

    @pl.when(j == 0)
    def _():
        _prenorm_into(h_scr, x_ref, g_ref, mod_ref, i, tm, n_lat)

    def proj():
        return _dot(h_scr[...], w_ref[...])

    def rotated(gain_ref, out_ref, scale):
        acc = proj()
        cos_t = cos_ref[...]
        sin_t = sin_ref[...]
        gain = gain_ref[...] * scale
        lane = lax.broadcasted_iota(jnp.int32, (1, HEAD_DIM), 1)
        first_half = (lane % (HEAD_DIM // 2)) < (HEAD_DIM // 4)
        for hh in range(acc.shape[-1] // HEAD_DIM):
            sl = slice(hh * HEAD_DIM, (hh + 1) * HEAD_DIM)
            out_ref[:, sl] = _norm_rope(acc[:, sl], gain, cos_t, sin_t, first_half).astype(BF16)

    @pl.when(j < 4)
    def _():
        rotated(qn_ref, q_ref, (HEAD_DIM ** -0.5) * LOG2E)

    @pl.when(j == 4)
    def _():
        rotated(kn_ref, kv_ref, 1.0)

    @pl.when(j == 5)
    def _():
        kv_ref[...] = proj().astype(BF16)

    @pl.when(j >= 6)
    def _():
        gs_ref[...] = _silu(proj()).astype(BF16)


def _odd_proj(xs, g, mod, w, cos_t, sin_t, q_norm, k_norm, n_lat):
    nt, d = xs.shape
    tm, tn = TM_PROJ, GQA_KV_HEADS * HEAD_DIM
    n_steps = w.shape[1] // tn
    assert n_steps == 10 and nt % tm == 0
    kern = functools.partial(_odd_proj_kernel, n_lat=n_lat, tm=tm)
    return pl.pallas_call(
        kern,
        grid=(nt // tm, n_steps),
        in_specs=[
            pl.BlockSpec((tm, d), lambda i, j: (i, 0)),
            pl.BlockSpec((1, d), lambda i, j: (0, 0)),
            pl.BlockSpec((8, 3 * d), lambda i, j: (0, 0)),
            pl.BlockSpec((d, tn), lambda i, j: (0, j)),
            pl.BlockSpec((tm, HEAD_DIM), lambda i, j: (i, 0)),
            pl.BlockSpec((tm, HEAD_DIM), lambda i, j: (i, 0)),
            pl.BlockSpec((1, HEAD_DIM), lambda i, j: (0, 0)),
            pl.BlockSpec((1, HEAD_DIM), lambda i, j: (0, 0)),
        ],
        out_specs=[
            pl.BlockSpec((tm, tn), lambda i, j: (i, jnp.minimum(j, 3))),
            pl.BlockSpec((tm, tn), lambda i, j: (i, jnp.clip(j - 4, 0, 1))),
            pl.BlockSpec((tm, tn), lambda i, j: (i, jnp.clip(j - 6, 0, 3))),
        ],
        out_shape=[
            jax.ShapeDtypeStruct((nt, 4 * tn), BF16),
            jax.ShapeDtypeStruct((nt, 2 * tn), BF16),
            jax.ShapeDtypeStruct((nt, 4 * tn), BF16),
        ],
        scratch_shapes=[pltpu.VMEM((tm, d), BF16)],
        compiler_params=_cparams(2),
        name="odd_proj",
    )(xs, g, mod, w, cos_t, sin_t, q_norm.reshape(1, HEAD_DIM), k_norm.reshape(1, HEAD_DIM))


GQA_SAFE_BOUND = 60.0


def _gqa_kernel(bound_ref, q_ref, k_ref, v_ref, g_ref, o_ref, acc_ref, *, n_lat, nc):
    b = pl.program_id(1)
    tq = q_ref.shape[0]
    rows = GQA_GROUP * tq
    bound = bound_ref[0]
    is_lat = b * tq < n_lat
    q = jnp.concatenate([q_ref[:, h * HEAD_DIM:(h + 1) * HEAD_DIM] for h in range(GQA_GROUP)], axis=0)

    def with_ones(v_chunk):
        return jnp.concatenate([v_chunk, jnp.ones_like(v_chunk)], axis=1)

    def write_out(o):
        for h in range(GQA_GROUP):
            sl = slice(h * HEAD_DIM, (h + 1) * HEAD_DIM)
            o_ref[:, sl] = (o[h * tq:(h + 1) * tq, :] * g_ref[:, sl].astype(F32)).astype(BF16)

    @pl.when(bound <= GQA_SAFE_BOUND)
    def _():
        def chunk(k_chunk, v_chunk, first):
            p = jnp.exp2(_dot_nt(q, k_chunk) - bound).astype(BF16)
            pv = _dot(p, with_ones(v_chunk))
            if first:
                acc_ref[...] = pv
            else:
                acc_ref[...] += pv

        chunk(k_ref[n_lat:n_lat + nc, :], v_ref[n_lat:n_lat + nc, :], True)

        def body(c, _):
            start = pl.multiple_of(c * GQA_TK, GQA_TK)
            chunk(k_ref[pl.ds(start, GQA_TK), :], v_ref[pl.ds(start, GQA_TK), :], False)
            return 0

        lax.fori_loop(0, jnp.where(is_lat, n_lat // GQA_TK, 0), body, 0)
        acc = acc_ref[...]
        write_out(acc[:, :HEAD_DIM] / acc[:, HEAD_DIM:])

    @pl.when(bound > GQA_SAFE_BOUND)
    def _():
        def step(carry, k_chunk, v_chunk):
            m, l, acc = carry
            s = _dot_nt(q, k_chunk)
            m_new = jnp.maximum(m, jnp.max(s, axis=-1, keepdims=True))
            alpha = jnp.exp2(m - m_new)
            p = jnp.exp2(s - m_new)
            l = alpha * l + jnp.sum(p, axis=-1, keepdims=True)
            acc = alpha * acc + _dot(p.astype(BF16), v_chunk)
            return m_new, l, acc

        init = (jnp.full((rows, 1), NEG_BIG, F32), jnp.zeros((rows, 1), F32), jnp.zeros((rows, HEAD_DIM), F32))
        carry = step(init, k_ref[n_lat:n_lat + nc, :], v_ref[n_lat:n_lat + nc, :])

        def body(c, carry):
            start = pl.multiple_of(c * GQA_TK_ONLINE, GQA_TK_ONLINE)
            return step(carry, k_ref[pl.ds(start, GQA_TK_ONLINE), :], v_ref[pl.ds(start, GQA_TK_ONLINE), :])

        m, l, acc = lax.fori_loop(0, jnp.where(is_lat, n_lat // GQA_TK_ONLINE, 0), body, carry)
        write_out(acc / l)


def _gqa_attention(q, kv, gs, q_norm, k_norm, n_lat):
    nt = q.shape[0]
    nc = nt - n_lat
    tq = GQA_TQ
    gw = GQA_GROUP * HEAD_DIM
    assert nt % tq == 0 and n_lat % tq == 0 and n_lat % GQA_TK == 0 and n_lat % GQA_TK_ONLINE == 0
    bound = (1.01 * HEAD_DIM * (HEAD_DIM ** -0.5) * LOG2E) * jnp.max(jnp.abs(q_norm)) * jnp.max(jnp.abs(k_norm))
    kern = functools.partial(_gqa_kernel, n_lat=n_lat, nc=nc)
    return pl.pallas_call(
        kern,
        grid=(GQA_KV_HEADS, nt // tq),
        in_specs=[
            pl.BlockSpec(memory_space=pltpu.SMEM),
            pl.BlockSpec((tq, gw), lambda h, b: (b, h)),
            pl.BlockSpec((nt, HEAD_DIM), lambda h, b: (0, h)),
            pl.BlockSpec((nt, HEAD_DIM), lambda h, b: (0, GQA_KV_HEADS + h)),
            pl.BlockSpec((tq, gw), lambda h, b: (b, h)),
        ],
        out_specs=pl.BlockSpec((tq, gw), lambda h, b: (b, h)),
        out_shape=jax.ShapeDtypeStruct((nt, GQA_HEADS * HEAD_DIM), BF16),
        scratch_shapes=[pltpu.VMEM((GQA_GROUP * tq, 2 * HEAD_DIM), F32)],
        compiler_params=_cparams(2),
        name="gqa_attention",
    )(bound.reshape(1).astype(F32), q, kv, kv, gs)


def kernel(x, c, ctx, c_ctx, norm_g, w_mod, b_mod, e_w_in, e_rpb, e_dw_w, e_dw_b, e_ln_g, e_ln_b, e_w_out,
           o_w_in, o_q_norm, o_k_norm, o_w_out, final_norm_g):
    batch, n_lat, d = x.shape
    nc = ctx.shape[1]
    depth = norm_g.shape[0]
    assert batch == 1
    rows = n_lat // GRID_W

    xs = jnp.concatenate([x[0], ctx[0]], axis=0)
    cvec = jnp.zeros((8, d), F32).at[0].set(c[0]).at[1].set(c_ctx)
    mods = _modulation(cvec, w_mod, b_mod)
    cos_t, sin_t = _rope_tables(n_lat, nc)
    fg = final_norm_g.reshape(1, d)

    for l in range(depth):
        i = l // 2
        g = norm_g[l].reshape(1, d)
        mod = mods[l]
        final = l == depth - 1
        if l % 2 == 0:
            w = e_w_in[i]
            cw = w.shape[1] // 7
            hc = cw // 2
            w_perm = jnp.concatenate(
                [w[:, :4 * cw], w[:, 4 * cw:4 * cw + hc], w[:, 5 * cw:5 * cw + hc],
                 w[:, 4 * cw + hc:5 * cw], w[:, 5 * cw + hc:6 * cw], w[:, 6 * cw:]], axis=1).astype(BF16)
            qkv, ga, u, gb = _even_proj(xs, g, mod, w_perm, n_lat)
            mix_a = _na_attention(qkv, ga, _na_bias_table(e_rpb[i], rows, nc), n_lat)
            mix_b = _conformer_conv(u, gb, e_dw_w[i], e_dw_b[i], e_ln_g[i], e_ln_b[i], n_lat)
            xs = _out_proj(mix_a, mix_b, (0, 0), e_w_out[i].astype(BF16), xs, mod, fg, n_lat, final)
        else:
            q, kv, gs = _odd_proj(xs, g, mod, o_w_in[i].astype(BF16), cos_t, sin_t,
                                  o_q_norm[i], o_k_norm[i], n_lat)
            mix = _gqa_attention(q, kv, gs, o_q_norm[i], o_k_norm[i], n_lat)
            xs = _out_proj(mix, mix, (0, 1), o_w_out[i].astype(BF16), xs, mod, fg, n_lat, final)
    return xs[:n_lat][None]
```

```python
import functools

import numpy as np
import jax
import jax.numpy as jnp
from jax import lax
from jax.experimental import pallas as pl
from jax.experimental.pallas import tpu as pltpu

F32 = jnp.float32
BF16 = jnp.bfloat16

GRID_W = 64
HEAD_DIM = 128
NORM_EPS = 1e-6
NA_HEADS = 8
NA_WIN_ROWS = 8
NA_WIN_COLS = 16
CONV_KERNEL = 31
GQA_HEADS = 16
GQA_KV_HEADS = 4
GQA_GROUP = GQA_HEADS // GQA_KV_HEADS
ROPE_THETA = 10000.0
LOG2E = 1.4426950408889634

LANES = 128
V7X_VMEM_BYTES = 64 * 1024 * 1024
VMEM_LIMIT = V7X_VMEM_BYTES - 8 * 1024 * 1024

TM_PROJ = 384
TM_OUT = 256
NA_ROWS = 4
NA_SLAB = NA_WIN_ROWS + NA_ROWS
NA_TQ = NA_ROWS * GRID_W
NA_HP = 2
CONV_T = 128
CONV_HALO = 16
CONV_CH = 32
GQA_TQ = 256
GQA_TK = 2048
GQA_TK_ONLINE = 512
NEG_BIG = -1e30


def _cparams(n_axes):
    return pltpu.CompilerParams(dimension_semantics=("arbitrary",) * n_axes, vmem_limit_bytes=VMEM_LIMIT)


def _sigmoid(x):
    return 1.0 / (1.0 + jnp.exp(-x))


def _silu(x):
    return x * _sigmoid(x)


def _dot(a, b):
    return jnp.dot(a, b, preferred_element_type=F32)


def _dot_nt(a, b):
    return lax.dot_general(a, b, (((1,), (1,)), ((), ())), preferred_element_type=F32)


def _row_is_ctx(tile_idx, tm, n_lat):
    rows = tile_idx * tm + lax.broadcasted_iota(jnp.int32, (tm, 1), 0)
    return rows >= n_lat


def _mod_rows(mod_ref, part, is_ctx, d):
    lat = mod_ref[0:1, part * d:(part + 1) * d]
    ctx = mod_ref[1:2, part * d:(part + 1) * d]
    return jnp.where(is_ctx, ctx, lat)


def _prenorm_into(h_ref, x_ref, g_ref, mod_ref, tile_idx, tm, n_lat):
    d = x_ref.shape[-1]

    def normed():
        x = x_ref[...]
        ms = jnp.mean(x * x, axis=-1, keepdims=True)
        return x * lax.rsqrt(ms + NORM_EPS)

    all_lat = (tile_idx + 1) * tm <= n_lat

    @pl.when(all_lat)
    def _():
        a = g_ref[...] * (1.0 + mod_ref[0:1, d:2 * d])
        h_ref[...] = (normed() * a + mod_ref[0:1, 0:d]).astype(BF16)

    @pl.when(jnp.logical_not(all_lat))
    def _():
        is_ctx = _row_is_ctx(tile_idx, tm, n_lat)
        sh = _mod_rows(mod_ref, 0, is_ctx, d)
        sc = _mod_rows(mod_ref, 1, is_ctx, d)
        h_ref[...] = (normed() * g_ref[...] * (1.0 + sc) + sh).astype(BF16)


def _mod_kernel(c_ref, w_ref, b_ref, o_ref):
    s = _silu(c_ref[...]).astype(BF16)
    o_ref[...] = _dot(s, w_ref[...].astype(BF16)) + b_ref[...]


def _modulation(cvec, w_mod, b_mod):
    depth, d, d3 = w_mod.shape
    tn = 768
    return pl.pallas_call(
        _mod_kernel,
        grid=(depth, d3 // tn),
        in_specs=[
            pl.BlockSpec((8, d), lambda l, j: (0, 0)),
            pl.BlockSpec((None, d, tn), lambda l, j: (l, 0, j)),
            pl.BlockSpec((None, 1, tn), lambda l, j: (l, 0, j)),
        ],
        out_specs=pl.BlockSpec((None, 8, tn), lambda l, j: (l, 0, j)),
        out_shape=jax.ShapeDtypeStruct((depth, 8, d3), F32),
        compiler_params=_cparams(2),
        name="modulation",
    )(cvec, w_mod, b_mod.reshape(depth, 1, d3))


def _even_proj_kernel(x_ref, g_ref, mod_ref, w_ref, qkv_ref, ga_ref, u_ref, gb_ref, h_scr, *, n_lat, tm):
    i = pl.program_id(0)
    j = pl.program_id(1)

    @pl.when(j == 0)
    def _():
        _prenorm_into(h_scr, x_ref, g_ref, mod_ref, i, tm, n_lat)

    def proj():
        return _dot(h_scr[...], w_ref[...])

    @pl.when(j < 3)
    def _():
        qkv_ref[...] = proj().astype(BF16)

    @pl.when(j == 3)
    def _():
        ga_ref[...] = _silu(proj()).astype(BF16)

    @pl.when(jnp.logical_or(j == 4, j == 5))
    def _():
        acc = proj()
        half = acc.shape[-1] // 2
        u_ref[...] = acc[:, :half] * _sigmoid(acc[:, half:])

    @pl.when(j == 6)
    def _():
        gb_ref[...] = _silu(proj()).astype(BF16)


def _column_tiles(w, tn):
    d, n = w.shape
    return w.astype(BF16).reshape(d, n // tn, tn).transpose(1, 0, 2)


def _even_proj(xs, g, mod, w_perm, n_lat):
    nt, d = xs.shape
    tm, tn = TM_PROJ, 1024
    n_steps = w_perm.shape[1] // tn
    assert n_steps == 7 and nt % tm == 0
    w_perm = _column_tiles(w_perm, tn)
    kern = functools.partial(_even_proj_kernel, n_lat=n_lat, tm=tm)
    return pl.pallas_call(
        kern,
        grid=(nt // tm, n_steps),
        in_specs=[
            pl.BlockSpec((tm, d), lambda i, j: (i, 0)),
            pl.BlockSpec((1, d), lambda i, j: (0, 0)),
            pl.BlockSpec((8, 3 * d), lambda i, j: (0, 0)),
            pl.BlockSpec((None, d, tn), lambda i, j: (j, 0, 0)),
        ],
        out_specs=[
            pl.BlockSpec((tm, tn), lambda i, j: (i, jnp.minimum(j, 2))),
            pl.BlockSpec((tm, tn), lambda i, j: (i, 0)),
            pl.BlockSpec((tm, tn // 2), lambda i, j: (i, jnp.clip(j - 4, 0, 1))),
            pl.BlockSpec((tm, tn), lambda i, j: (i, 0)),
        ],
        out_shape=[
            jax.ShapeDtypeStruct((nt, 3 * tn), BF16),
            jax.ShapeDtypeStruct((nt, tn), BF16),
            jax.ShapeDtypeStruct((nt, tn), F32),
            jax.ShapeDtypeStruct((nt, tn), BF16),
        ],
        scratch_shapes=[pltpu.VMEM((tm, d), BF16)],
        compiler_params=_cparams(2),
        name="even_proj",
    )(xs, g, mod, w_perm)


def _na_bias_table(rpb, rows, nc):
    pad = GRID_W - NA_WIN_COLS
    rp = jnp.pad(rpb.astype(F32), ((0, 0), (0, 0), (pad, pad)))
    toe = jnp.stack([rp[:, :, GRID_W - 1 - qc:2 * GRID_W - 1 - qc] for qc in range(GRID_W)], axis=2)
    qc = np.arange(GRID_W)[:, None]
    kc = np.arange(GRID_W)[None, :]
    cs = np.clip(qc - NA_WIN_COLS // 2, 0, GRID_W - NA_WIN_COLS)
    col_ok = (kc >= cs) & (kc < cs + NA_WIN_COLS)
    toe = jnp.where(col_ok, toe, NEG_BIG)
    masked = jnp.full((rpb.shape[0], GRID_W, GRID_W), NEG_BIG, F32)
    pats = []
    for r0 in (0, 2 * NA_ROWS, rows - NA_ROWS):
        ks = int(np.clip(r0 - NA_WIN_ROWS // 2, 0, rows - NA_SLAB))
        q_blocks = []
        for r in range(r0, r0 + NA_ROWS):
            rs = int(np.clip(r - NA_WIN_ROWS // 2, 0, rows - NA_WIN_ROWS))
            k_blocks = [toe[:, kr - r + NA_WIN_ROWS - 1] if rs <= kr < rs + NA_WIN_ROWS else masked
                        for kr in range(ks, ks + NA_SLAB)]
            q_blocks.append(jnp.concatenate(k_blocks, axis=-1))
        pat = jnp.concatenate(q_blocks, axis=1)
        pats.append(jnp.concatenate([pat, jnp.zeros(pat.shape[:2] + (nc,), F32)], axis=-1))
    return jnp.stack(pats)


def _na_kernel(q_ref, k_ref, v_ref, bias_ref, ga_ref, o_ref, *, n_lat, nc, rows):
    b = pl.program_id(1)
    nb_lat = rows // NA_ROWS
    scale = HEAD_DIM ** -0.5

    def attend(h, keys, values, bias):
        sl = slice(h * HEAD_DIM, (h + 1) * HEAD_DIM)
        s = _dot_nt(q_ref[:, sl], keys) * scale
        if bias is not None:
            s = s + bias
        p = jnp.exp(s - jnp.max(s, axis=-1, keepdims=True)).astype(BF16)
        o = _dot(p, jnp.concatenate([values, jnp.ones_like(values)], axis=1))
        gate = ga_ref[:, sl].astype(F32)
        o_ref[:, sl] = (o[:, :HEAD_DIM] / o[:, HEAD_DIM:] * gate).astype(BF16)

    @pl.when(b < nb_lat)
    def _():
        ks = jnp.clip(b * NA_ROWS - NA_WIN_ROWS // 2, 0, rows - NA_SLAB)
        start = pl.multiple_of(ks * GRID_W, GRID_W)
        for h in range(NA_HP):
            sl = slice(h * HEAD_DIM, (h + 1) * HEAD_DIM)
            keys = jnp.concatenate([k_ref[pl.ds(start, NA_SLAB * GRID_W), sl], k_ref[n_lat:n_lat + nc, sl]], axis=0)
            values = jnp.concatenate([v_ref[pl.ds(start, NA_SLAB * GRID_W), sl], v_ref[n_lat:n_lat + nc, sl]], axis=0)
            attend(h, keys, values, bias_ref[h])

    @pl.when(b >= nb_lat)
    def _():
        for h in range(NA_HP):
            sl = slice(h * HEAD_DIM, (h + 1) * HEAD_DIM)
            attend(h, k_ref[n_lat:n_lat + nc, sl], v_ref[n_lat:n_lat + nc, sl], None)


def _na_attention(qkv, ga, bias_tab, n_lat):
    nt = qkv.shape[0]
    nc = nt - n_lat
    rows = n_lat // GRID_W
    assert nc == NA_TQ and rows % NA_ROWS == 0 and rows >= NA_SLAB
    nb = nt // NA_TQ
    nb_lat = rows // NA_ROWS
    n_keys = NA_SLAB * GRID_W + nc
    hw = NA_HP * HEAD_DIM
    groups = NA_HEADS // NA_HP

    def pat(b):
        return jnp.where(b == 0, 0, jnp.where(b >= nb_lat - 1, 2, 1))

    kern = functools.partial(_na_kernel, n_lat=n_lat, nc=nc, rows=rows)
    return pl.pallas_call(
        kern,
        grid=(groups, nb),
        in_specs=[
            pl.BlockSpec((NA_TQ, hw), lambda h, b: (b, h)),
            pl.BlockSpec((nt, hw), lambda h, b: (0, groups + h)),
            pl.BlockSpec((nt, hw), lambda h, b: (0, 2 * groups + h)),
            pl.BlockSpec((None, NA_HP, NA_TQ, n_keys), lambda h, b: (pat(b), h, 0, 0)),
            pl.BlockSpec((NA_TQ, hw), lambda h, b: (b, h)),
        ],
        out_specs=pl.BlockSpec((NA_TQ, hw), lambda h, b: (b, h)),
        out_shape=jax.ShapeDtypeStruct((nt, NA_HEADS * HEAD_DIM), BF16),
        compiler_params=_cparams(2),
        name="na_attention",
    )(qkv, qkv, qkv, bias_tab, ga)


def _conv_kernel(um_ref, up_ref, un_ref, gb_ref, w_ref, b_ref, lg_ref, lb_ref, o_ref, ubuf, shifted, *, n_lat, nt):
    t = pl.program_id(0)
    start = t * CONV_T
    prev_ok = jnp.logical_and(start != 0, start != n_lat)
    next_ok = jnp.logical_and(start + CONV_T != n_lat, start + CONV_T != nt)
    ubuf[0:CONV_HALO, :] = jnp.where(prev_ok, up_ref[...], 0.0)
    ubuf[CONV_HALO:CONV_HALO + CONV_T, :] = um_ref[...]
    ubuf[CONV_HALO + CONV_T:, :] = jnp.where(next_ok, un_ref[...], 0.0)
    span = CONV_T + 2 * CONV_HALO - 8
    for b in range(1, 8):
        shifted[b - 1, :, :] = ubuf[b:b + span, :]
    off = CONV_HALO - CONV_KERNEL // 2
    for r0 in range(0, CONV_T, CONV_CH):
        acc = jnp.zeros((CONV_CH, um_ref.shape[-1]), F32) + b_ref[...]
        for j in range(CONV_KERNEL):
            a, b = divmod(off + j, 8)
            rows = slice(r0 + 8 * a, r0 + 8 * a + CONV_CH)
            tap = ubuf[rows, :] if b == 0 else shifted[b - 1, rows, :]
            acc = acc + w_ref[j:j + 1, :] * tap
        mu = jnp.mean(acc, axis=-1, keepdims=True)
        xc = acc - mu
        var = jnp.mean(xc * xc, axis=-1, keepdims=True)
        y = xc * lax.rsqrt(var + NORM_EPS) * lg_ref[...] + lb_ref[...]
        y = _silu(y) * gb_ref[r0:r0 + CONV_CH, :].astype(F32)
        o_ref[r0:r0 + CONV_CH, :] = y.astype(BF16)


def _conformer_conv(u, gb, dw_w, dw_b, ln_g, ln_b, n_lat):
    nt, cw = u.shape
    assert nt % CONV_T == 0 and n_lat % CONV_T == 0
    hb = CONV_T // CONV_HALO
    n_halo_blocks = nt // CONV_HALO
    w_pad = jnp.zeros((32, cw), F32).at[:CONV_KERNEL].set(dw_w)
    kern = functools.partial(_conv_kernel, n_lat=n_lat, nt=nt)
    vec = lambda: pl.BlockSpec((1, cw), lambda t: (0, 0))
    return pl.pallas_call(
        kern,
        grid=(nt // CONV_T,),
        in_specs=[
            pl.BlockSpec((CONV_T, cw), lambda t: (t, 0)),
            pl.BlockSpec((CONV_HALO, cw), lambda t: (jnp.maximum(t * hb - 1, 0), 0)),
            pl.BlockSpec((CONV_HALO, cw), lambda t: (jnp.minimum((t + 1) * hb, n_halo_blocks - 1), 0)),
            pl.BlockSpec((CONV_T, cw), lambda t: (t, 0)),
            pl.BlockSpec((32, cw), lambda t: (0, 0)),
            vec(), vec(), vec(),
        ],
        out_specs=pl.BlockSpec((CONV_T, cw), lambda t: (t, 0)),
        out_shape=jax.ShapeDtypeStruct((nt, cw), BF16),
        scratch_shapes=[pltpu.VMEM((CONV_T + 2 * CONV_HALO, cw), F32),
                        pltpu.VMEM((7, CONV_T + 2 * CONV_HALO - 8, cw), F32)],
        compiler_params=_cparams(1),
        name="conformer_conv",
    )(u, u, u, gb, w_pad, dw_b.reshape(1, cw), ln_g.reshape(1, cw), ln_b.reshape(1, cw))


def _out_proj_kernel(a_ref, b_ref, w_ref, x_ref, mod_ref, fg_ref, o_ref, *, n_lat, tm, final):
    i = pl.program_id(0)
    half = a_ref.shape[-1]
    y = _dot(a_ref[...], w_ref[0:half, :]) + _dot(b_ref[...], w_ref[half:, :])
    d = y.shape[-1]
    gt = _mod_rows(mod_ref, 2, _row_is_ctx(i, tm, n_lat), d)
    xn = x_ref[...] + gt * y
    if final:
        ms = jnp.mean(xn * xn, axis=-1, keepdims=True)
        xn = xn * lax.rsqrt(ms + NORM_EPS) * fg_ref[...]
    o_ref[...] = xn


def _out_proj(mix_a, mix_b, cols, w, xs, mod, fg, n_lat, final):
    nt, d = xs.shape
    tm = TM_OUT
    half = w.shape[0] // 2
    kern = functools.partial(_out_proj_kernel, n_lat=n_lat, tm=tm, final=final)
    return pl.pallas_call(
        kern,
        grid=(nt // tm,),
        in_specs=[
            pl.BlockSpec((tm, half), lambda i: (i, cols[0])),
            pl.BlockSpec((tm, half), lambda i: (i, cols[1])),
            pl.BlockSpec((2 * half, d), lambda i: (0, 0)),
            pl.BlockSpec((tm, d), lambda i: (i, 0)),
            pl.BlockSpec((8, 3 * d), lambda i: (0, 0)),
            pl.BlockSpec((1, d), lambda i: (0, 0)),
        ],
        out_specs=pl.BlockSpec((tm, d), lambda i: (i, 0)),
        out_shape=jax.ShapeDtypeStruct((nt, d), F32),
        compiler_params=_cparams(1),
        name="out_proj_final" if final else "out_proj",
    )(mix_a, mix_b, w, xs, mod, fg)


def _rope_tables(n_lat, nc):
    t = jnp.arange(n_lat, dtype=jnp.int32)
    row = (t // GRID_W).astype(F32)
    col = (t % GRID_W).astype(F32)
    n_freq = HEAD_DIM // 4
    inv = ROPE_THETA ** (-jnp.arange(n_freq, dtype=F32) / n_freq)
    cr, sr = jnp.cos(row[:, None] * inv), jnp.sin(row[:, None] * inv)
    cc, sc = jnp.cos(col[:, None] * inv), jnp.sin(col[:, None] * inv)
    cos_t = jnp.concatenate([cr, cr, cc, cc], axis=-1)
    sin_t = jnp.concatenate([-sr, sr, -sc, sc], axis=-1)
    cos_t = jnp.concatenate([cos_t, jnp.ones((nc, HEAD_DIM), F32)], axis=0)
    sin_t = jnp.concatenate([sin_t, jnp.zeros((nc, HEAD_DIM), F32)], axis=0)
    return cos_t, sin_t


def _norm_rope(x, gain, cos_t, sin_t, first_half):
    ms = jnp.mean(x * x, axis=-1, keepdims=True)
    xn = x * lax.rsqrt(ms + NORM_EPS) * gain
    partner = jnp.where(first_half, pltpu.roll(xn, 3 * HEAD_DIM // 4, 1), pltpu.roll(xn, HEAD_DIM // 4, 1))
    return xn * cos_t + partner * sin_t


def _odd_proj_kernel(x_ref, g_ref, mod_ref, w_ref, cos_ref, sin_ref, qn_ref, kn_ref,
                     q_ref, kv_ref, gs_ref, h_scr, *, n_lat, tm):
    i = pl.program_id(0)
    j = pl.program_id(1)

    @pl.when(j == 0)
    def _():
        _prenorm_into(h_scr, x_ref, g_ref, mod_ref, i, tm, n_lat)

    def proj():
        return _dot(h_scr[...], w_ref[...])

    def rotated(gain_ref, out_ref, scale):
        acc = proj()
        cos_t = cos_ref[...]
        sin_t = sin_ref[...]
        gain = gain_ref[...] * scale
        lane = lax.broadcasted_iota(jnp.int32, (1, HEAD_DIM), 1)
        first_half = (lane % (HEAD_DIM // 2)) < (HEAD_DIM // 4)
        for hh in range(acc.shape[-1] // HEAD_DIM):
            sl = slice(hh * HEAD_DIM, (hh + 1) * HEAD_DIM)
            out_ref[:, sl] = _norm_rope(acc[:, sl], gain, cos_t, sin_t, first_half).astype(BF16)

    @pl.when(j < 4)
    def _():
        rotated(qn_ref, q_ref, (HEAD_DIM ** -0.5) * LOG2E)

    @pl.when(j == 4)
    def _():
        rotated(kn_ref, kv_ref, 1.0)

    @pl.when(j == 5)
    def _():
        kv_ref[...] = proj().astype(BF16)

    @pl.when(j >= 6)
    def _():
        gs_ref[...] = _silu(proj()).astype(BF16)


def _odd_proj(xs, g, mod, w, cos_t, sin_t, q_norm, k_norm, n_lat):
    nt, d = xs.shape
    tm, tn = TM_PROJ, GQA_KV_HEADS * HEAD_DIM
    n_steps = w.shape[1] // tn
    assert n_steps == 10 and nt % tm == 0
    w = _column_tiles(w, tn)
    kern = functools.partial(_odd_proj_kernel, n_lat=n_lat, tm=tm)
    return pl.pallas_call(
        kern,
        grid=(nt // tm, n_steps),
        in_specs=[
            pl.BlockSpec((tm, d), lambda i, j: (i, 0)),
            pl.BlockSpec((1, d), lambda i, j: (0, 0)),
            pl.BlockSpec((8, 3 * d), lambda i, j: (0, 0)),
            pl.BlockSpec((None, d, tn), lambda i, j: (j, 0, 0)),
            pl.BlockSpec((tm, HEAD_DIM), lambda i, j: (i, 0)),
            pl.BlockSpec((tm, HEAD_DIM), lambda i, j: (i, 0)),
            pl.BlockSpec((1, HEAD_DIM), lambda i, j: (0, 0)),
            pl.BlockSpec((1, HEAD_DIM), lambda i, j: (0, 0)),
        ],
        out_specs=[
            pl.BlockSpec((tm, tn), lambda i, j: (i, jnp.minimum(j, 3))),
            pl.BlockSpec((tm, tn), lambda i, j: (i, jnp.clip(j - 4, 0, 1))),
            pl.BlockSpec((tm, tn), lambda i, j: (i, jnp.clip(j - 6, 0, 3))),
        ],
        out_shape=[
            jax.ShapeDtypeStruct((nt, 4 * tn), BF16),
            jax.ShapeDtypeStruct((nt, 2 * tn), BF16),
            jax.ShapeDtypeStruct((nt, 4 * tn), BF16),
        ],
        scratch_shapes=[pltpu.VMEM((tm, d), BF16)],
        compiler_params=_cparams(2),
        name="odd_proj",
    )(xs, g, mod, w, cos_t, sin_t, q_norm.reshape(1, HEAD_DIM), k_norm.reshape(1, HEAD_DIM))


GQA_SAFE_BOUND = 60.0


def _gqa_kernel(bound_ref, q_ref, k_ref, v_ref, g_ref, o_ref, *, n_lat, nc):
    b = pl.program_id(1)
    tq = q_ref.shape[0]
    rows = GQA_GROUP * tq
    bound = bound_ref[0]
    is_lat = b * tq < n_lat
    q = jnp.concatenate([q_ref[:, h * HEAD_DIM:(h + 1) * HEAD_DIM] for h in range(GQA_GROUP)], axis=0)

    def with_ones(v_chunk):
        return jnp.concatenate([v_chunk, jnp.ones_like(v_chunk)], axis=1)

    def write_out(o):
        for h in range(GQA_GROUP):
            sl = slice(h * HEAD_DIM, (h + 1) * HEAD_DIM)
            o_ref[:, sl] = (o[h * tq:(h + 1) * tq, :] * g_ref[:, sl].astype(F32)).astype(BF16)

    def fixed_shift_chunk(k_chunk, v_chunk):
        p = jnp.exp2(_dot_nt(q, k_chunk) - bound).astype(BF16)
        return _dot(p, with_ones(v_chunk))

    def ctx_chunk():
        return fixed_shift_chunk(k_ref[n_lat:n_lat + nc, :], v_ref[n_lat:n_lat + nc, :])

    def finish(acc):
        write_out(acc[:, :HEAD_DIM] / acc[:, HEAD_DIM:])

    small = bound <= GQA_SAFE_BOUND

    @pl.when(jnp.logical_and(small, is_lat))
    def _():
        acc = ctx_chunk()
        for c in range(n_lat // GQA_TK):
            rows_c = slice(c * GQA_TK, (c + 1) * GQA_TK)
            acc = acc + fixed_shift_chunk(k_ref[rows_c, :], v_ref[rows_c, :])
        finish(acc)

    @pl.when(jnp.logical_and(small, jnp.logical_not(is_lat)))
    def _():
        finish(ctx_chunk())

    @pl.when(bound > GQA_SAFE_BOUND)
    def _():
        def step(carry, k_chunk, v_chunk):
            m, l, acc = carry
            s = _dot_nt(q, k_chunk)
            m_new = jnp.maximum(m, jnp.max(s, axis=-1, keepdims=True))
            alpha = jnp.exp2(m - m_new)
            p = jnp.exp2(s - m_new)
            l = alpha * l + jnp.sum(p, axis=-1, keepdims=True)
            acc = alpha * acc + _dot(p.astype(BF16), v_chunk)
            return m_new, l, acc

        init = (jnp.full((rows, 1), NEG_BIG, F32), jnp.zeros((rows, 1), F32), jnp.zeros((rows, HEAD_DIM), F32))
        carry = step(init, k_ref[n_lat:n_lat + nc, :], v_ref[n_lat:n_lat + nc, :])

        def body(c, carry):
            start = pl.multiple_of(c * GQA_TK_ONLINE, GQA_TK_ONLINE)
            return step(carry, k_ref[pl.ds(start, GQA_TK_ONLINE), :], v_ref[pl.ds(start, GQA_TK_ONLINE), :])

        m, l, acc = lax.fori_loop(0, jnp.where(is_lat, n_lat // GQA_TK_ONLINE, 0), body, carry)
        write_out(acc / l)


def _gqa_attention(q, kv, gs, q_norm, k_norm, n_lat):
    nt = q.shape[0]
    nc = nt - n_lat
    tq = GQA_TQ
    gw = GQA_GROUP * HEAD_DIM
    assert nt % tq == 0 and n_lat % tq == 0 and n_lat % GQA_TK == 0 and n_lat % GQA_TK_ONLINE == 0
    bound = (1.01 * HEAD_DIM * (HEAD_DIM ** -0.5) * LOG2E) * jnp.max(jnp.abs(q_norm)) * jnp.max(jnp.abs(k_norm))
    kern = functools.partial(_gqa_kernel, n_lat=n_lat, nc=nc)
    return pl.pallas_call(
        kern,
        grid=(GQA_KV_HEADS, nt // tq),
        in_specs=[
            pl.BlockSpec(memory_space=pltpu.SMEM),
            pl.BlockSpec((tq, gw), lambda h, b: (b, h)),
            pl.BlockSpec((nt, HEAD_DIM), lambda h, b: (0, h)),
            pl.BlockSpec((nt, HEAD_DIM), lambda h, b: (0, GQA_KV_HEADS + h)),
            pl.BlockSpec((tq, gw), lambda h, b: (b, h)),
        ],
        out_specs=pl.BlockSpec((tq, gw), lambda h, b: (b, h)),
        out_shape=jax.ShapeDtypeStruct((nt, GQA_HEADS * HEAD_DIM), BF16),
        compiler_params=_cparams(2),
        name="gqa_attention",
    )(bound.reshape(1).astype(F32), q, kv, kv, gs)


def kernel(x, c, ctx, c_ctx, norm_g, w_mod, b_mod, e_w_in, e_rpb, e_dw_w, e_dw_b, e_ln_g, e_ln_b, e_w_out,
           o_w_in, o_q_norm, o_k_norm, o_w_out, final_norm_g):
    batch, n_lat, d = x.shape
    nc = ctx.shape[1]
    depth = norm_g.shape[0]
    assert batch == 1
    rows = n_lat // GRID_W

    xs = jnp.concatenate([x[0], ctx[0]], axis=0)
    cvec = jnp.zeros((8, d), F32).at[0].set(c[0]).at[1].set(c_ctx)
    mods = _modulation(cvec, w_mod, b_mod)
    cos_t, sin_t = _rope_tables(n_lat, nc)
    fg = final_norm_g.reshape(1, d)

    for l in range(depth):
        i = l // 2
        g = norm_g[l].reshape(1, d)
        mod = mods[l]
        final = l == depth - 1
        if l % 2 == 0:
            w = e_w_in[i]
            cw = w.shape[1] // 7
            hc = cw // 2
            w_perm = jnp.concatenate(
                [w[:, :4 * cw], w[:, 4 * cw:4 * cw + hc], w[:, 5 * cw:5 * cw + hc],
                 w[:, 4 * cw + hc:5 * cw], w[:, 5 * cw + hc:6 * cw], w[:, 6 * cw:]], axis=1).astype(BF16)
            qkv, ga, u, gb = _even_proj(xs, g, mod, w_perm, n_lat)
            mix_a = _na_attention(qkv, ga, _na_bias_table(e_rpb[i], rows, nc), n_lat)
            mix_b = _conformer_conv(u, gb, e_dw_w[i], e_dw_b[i], e_ln_g[i], e_ln_b[i], n_lat)
            xs = _out_proj(mix_a, mix_b, (0, 0), e_w_out[i].astype(BF16), xs, mod, fg, n_lat, final)
        else:
            q, kv, gs = _odd_proj(xs, g, mod, o_w_in[i].astype(BF16), cos_t, sin_t,
                                  o_q_norm[i], o_k_norm[i], n_lat)
            mix = _gqa_attention(q, kv, gs, o_q_norm[i], o_k_norm[i], n_lat)
            xs = _out_proj(mix, mix, (0, 1), o_w_out[i].astype(BF16), xs, mod, fg, n_lat, final)
    return xs[:n_lat][None]
```

```python
import functools

import numpy as np
import jax
import jax.numpy as jnp
from jax import lax
from jax.experimental import pallas as pl
from jax.experimental.pallas import tpu as pltpu

F32 = jnp.float32
BF16 = jnp.bfloat16

GRID_W = 64
HEAD_DIM = 128
NORM_EPS = 1e-6
NA_HEADS = 8
NA_WIN_ROWS = 8
NA_WIN_COLS = 16
CONV_KERNEL = 31
GQA_HEADS = 16
GQA_KV_HEADS = 4
GQA_GROUP = GQA_HEADS // GQA_KV_HEADS
ROPE_THETA = 10000.0
LOG2E = 1.4426950408889634

LANES = 128
V7X_VMEM_BYTES = 64 * 1024 * 1024
VMEM_LIMIT = V7X_VMEM_BYTES - 8 * 1024 * 1024

TM_PROJ = 256
TM_OUT = 256
NA_ROWS = 4
NA_SLAB = NA_WIN_ROWS + NA_ROWS
NA_TQ = NA_ROWS * GRID_W
NA_HP = 2
CONV_T = 128
CONV_HALO = 16
CONV_CH = 32
GQA_TQ = 256
GQA_TK = 2048
GQA_TK_ONLINE = 512
NEG_BIG = -1e30


def _cparams(n_axes):
    return pltpu.CompilerParams(dimension_semantics=("arbitrary",) * n_axes, vmem_limit_bytes=VMEM_LIMIT)


def _sigmoid(x):
    return 1.0 / (1.0 + jnp.exp(-x))


def _silu(x):
    return x * _sigmoid(x)


def _dot(a, b):
    return jnp.dot(a, b, preferred_element_type=F32)


def _dot_nt(a, b):
    return lax.dot_general(a, b, (((1,), (1,)), ((), ())), preferred_element_type=F32)


def _row_is_ctx(tile_idx, tm, n_lat):
    rows = tile_idx * tm + lax.broadcasted_iota(jnp.int32, (tm, 1), 0)
    return rows >= n_lat


def _mod_rows(mod_ref, part, is_ctx, d):
    lat = mod_ref[0:1, part * d:(part + 1) * d]
    ctx = mod_ref[1:2, part * d:(part + 1) * d]
    return jnp.where(is_ctx, ctx, lat)


def _prenorm(x_ref, g_ref, mod_ref, tile_idx, tm, n_lat):
    d = x_ref.shape[-1]
    x = x_ref[...]
    ms = jnp.mean(x * x, axis=-1, keepdims=True)
    row = (tile_idx * tm >= n_lat).astype(jnp.int32)
    mod = mod_ref[pl.ds(row, 1), :]
    a = g_ref[...] * (1.0 + mod[:, d:2 * d])
    return (x * lax.rsqrt(ms + NORM_EPS) * a + mod[:, 0:d]).astype(BF16)


def _mod_kernel(c_ref, w_ref, b_ref, o_ref):
    s = _silu(c_ref[...]).astype(BF16)
    o_ref[...] = _dot(s, w_ref[...].astype(BF16)) + b_ref[...]


def _modulation(cvec, w_mod, b_mod):
    depth, d, d3 = w_mod.shape
    tn = 768
    return pl.pallas_call(
        _mod_kernel,
        grid=(depth, d3 // tn),
        in_specs=[
            pl.BlockSpec((8, d), lambda l, j: (0, 0)),
            pl.BlockSpec((None, d, tn), lambda l, j: (l, 0, j)),
            pl.BlockSpec((None, 1, tn), lambda l, j: (l, 0, j)),
        ],
        out_specs=pl.BlockSpec((None, 8, tn), lambda l, j: (l, 0, j)),
        out_shape=jax.ShapeDtypeStruct((depth, 8, d3), F32),
        compiler_params=_cparams(2),
        name="modulation",
    )(cvec, w_mod, b_mod.reshape(depth, 1, d3))


def _even_proj_kernel(x_ref, g_ref, mod_ref, w_ref, qkv_ref, ga_ref, u_ref, gb_ref, *, n_lat, tm):
    h = _prenorm(x_ref, g_ref, mod_ref, pl.program_id(0), tm, n_lat)
    cw = ga_ref.shape[-1]

    def proj(seg):
        return _dot(h, w_ref[:, seg * cw:(seg + 1) * cw])

    for seg in range(3):
        qkv_ref[:, seg * cw:(seg + 1) * cw] = proj(seg).astype(BF16)
    ga_ref[...] = _silu(proj(3)).astype(BF16)
    u_ref[...] = proj(4) * _sigmoid(proj(5))
    gb_ref[...] = _silu(proj(6)).astype(BF16)


def _even_proj(xs, g, mod, w, n_lat):
    nt, d = xs.shape
    tm = TM_PROJ
    cw = w.shape[1] // 7
    assert nt % tm == 0 and n_lat % tm == 0
    kern = functools.partial(_even_proj_kernel, n_lat=n_lat, tm=tm)
    rows = lambda width: pl.BlockSpec((tm, width), lambda i: (i, 0))
    return pl.pallas_call(
        kern,
        grid=(nt // tm,),
        in_specs=[
            rows(d),
            pl.BlockSpec((1, d), lambda i: (0, 0)),
            pl.BlockSpec((8, 3 * d), lambda i: (0, 0)),
            pl.BlockSpec(memory_space=pltpu.VMEM),
        ],
        out_specs=[rows(3 * cw), rows(cw), rows(cw), rows(cw)],
        out_shape=[
            jax.ShapeDtypeStruct((nt, 3 * cw), BF16),
            jax.ShapeDtypeStruct((nt, cw), BF16),
            jax.ShapeDtypeStruct((nt, cw), F32),
            jax.ShapeDtypeStruct((nt, cw), BF16),
        ],
        compiler_params=_cparams(1),
        name="even_proj",
    )(xs, g, mod, w)


def _na_bias_table(rpb, rows, nc):
    pad = GRID_W - NA_WIN_COLS
    rp = jnp.pad(rpb.astype(F32), ((0, 0), (0, 0), (pad, pad)))
    toe = jnp.stack([rp[:, :, GRID_W - 1 - qc:2 * GRID_W - 1 - qc] for qc in range(GRID_W)], axis=2)
    qc = np.arange(GRID_W)[:, None]
    kc = np.arange(GRID_W)[None, :]
    cs = np.clip(qc - NA_WIN_COLS // 2, 0, GRID_W - NA_WIN_COLS)
    col_ok = (kc >= cs) & (kc < cs + NA_WIN_COLS)
    toe = jnp.where(col_ok, toe, NEG_BIG)
    masked = jnp.full((rpb.shape[0], GRID_W, GRID_W), NEG_BIG, F32)
    pats = []
    for r0 in (0, 2 * NA_ROWS, rows - NA_ROWS):
        ks = int(np.clip(r0 - NA_WIN_ROWS // 2, 0, rows - NA_SLAB))
        q_blocks = []
        for r in range(r0, r0 + NA_ROWS):
            rs = int(np.clip(r - NA_WIN_ROWS // 2, 0, rows - NA_WIN_ROWS))
            k_blocks = [toe[:, kr - r + NA_WIN_ROWS - 1] if rs <= kr < rs + NA_WIN_ROWS else masked
                        for kr in range(ks, ks + NA_SLAB)]
            q_blocks.append(jnp.concatenate(k_blocks, axis=-1))
        pat = jnp.concatenate(q_blocks, axis=1)
        pats.append(jnp.concatenate([pat, jnp.zeros(pat.shape[:2] + (nc,), F32)], axis=-1))
    return jnp.stack(pats)


def _na_kernel(q_ref, k_ref, v_ref, bias_ref, ga_ref, o_ref, *, n_lat, nc, rows):
    b = pl.program_id(1)
    nb_lat = rows // NA_ROWS
    scale = HEAD_DIM ** -0.5

    def attend(h, keys, values, bias):
        sl = slice(h * HEAD_DIM, (h + 1) * HEAD_DIM)
        s = _dot_nt(q_ref[:, sl], keys) * scale
        if bias is not None:
            s = s + bias
        p = jnp.exp(s - jnp.max(s, axis=-1, keepdims=True)).astype(BF16)
        o = _dot(p, jnp.concatenate([values, jnp.ones_like(values)], axis=1))
        gate = ga_ref[:, sl].astype(F32)
        o_ref[:, sl] = (o[:, :HEAD_DIM] / o[:, HEAD_DIM:] * gate).astype(BF16)

    @pl.when(b < nb_lat)
    def _():
        ks = jnp.clip(b * NA_ROWS - NA_WIN_ROWS // 2, 0, rows - NA_SLAB)
        start = pl.multiple_of(ks * GRID_W, GRID_W)
        for h in range(NA_HP):
            sl = slice(h * HEAD_DIM, (h + 1) * HEAD_DIM)
            keys = jnp.concatenate([k_ref[pl.ds(start, NA_SLAB * GRID_W), sl], k_ref[n_lat:n_lat + nc, sl]], axis=0)
            values = jnp.concatenate([v_ref[pl.ds(start, NA_SLAB * GRID_W), sl], v_ref[n_lat:n_lat + nc, sl]], axis=0)
            attend(h, keys, values, bias_ref[h])

    @pl.when(b >= nb_lat)
    def _():
        for h in range(NA_HP):
            sl = slice(h * HEAD_DIM, (h + 1) * HEAD_DIM)
            attend(h, k_ref[n_lat:n_lat + nc, sl], v_ref[n_lat:n_lat + nc, sl], None)


def _na_attention(qkv, ga, bias_tab, n_lat):
    nt = qkv.shape[0]
    nc = nt - n_lat
    rows = n_lat // GRID_W
    assert nc == NA_TQ and rows % NA_ROWS == 0 and rows >= NA_SLAB
    nb = nt // NA_TQ
    nb_lat = rows // NA_ROWS
    n_keys = NA_SLAB * GRID_W + nc
    hw = NA_HP * HEAD_DIM
    groups = NA_HEADS // NA_HP

    def pat(b):
        return jnp.where(b == 0, 0, jnp.where(b >= nb_lat - 1, 2, 1))

    kern = functools.partial(_na_kernel, n_lat=n_lat, nc=nc, rows=rows)
    return pl.pallas_call(
        kern,
        grid=(groups, nb),
        in_specs=[
            pl.BlockSpec((NA_TQ, hw), lambda h, b: (b, h)),
            pl.BlockSpec((nt, hw), lambda h, b: (0, groups + h)),
            pl.BlockSpec((nt, hw), lambda h, b: (0, 2 * groups + h)),
            pl.BlockSpec((None, NA_HP, NA_TQ, n_keys), lambda h, b: (pat(b), h, 0, 0)),
            pl.BlockSpec((NA_TQ, hw), lambda h, b: (b, h)),
        ],
        out_specs=pl.BlockSpec((NA_TQ, hw), lambda h, b: (b, h)),
        out_shape=jax.ShapeDtypeStruct((nt, NA_HEADS * HEAD_DIM), BF16),
        compiler_params=_cparams(2),
        name="na_attention",
    )(qkv, qkv, qkv, bias_tab, ga)


def _conv_kernel(um_ref, up_ref, un_ref, gb_ref, w_ref, b_ref, lg_ref, lb_ref, o_ref, ubuf, shifted, *, n_lat, nt):
    t = pl.program_id(0)
    start = t * CONV_T
    prev_ok = jnp.logical_and(start != 0, start != n_lat)
    next_ok = jnp.logical_and(start + CONV_T != n_lat, start + CONV_T != nt)
    ubuf[0:CONV_HALO, :] = jnp.where(prev_ok, up_ref[...], 0.0)
    ubuf[CONV_HALO:CONV_HALO + CONV_T, :] = um_ref[...]
    ubuf[CONV_HALO + CONV_T:, :] = jnp.where(next_ok, un_ref[...], 0.0)
    span = CONV_T + 2 * CONV_HALO - 8
    for b in range(1, 8):
        shifted[b - 1, :, :] = ubuf[b:b + span, :]
    off = CONV_HALO - CONV_KERNEL // 2
    for r0 in range(0, CONV_T, CONV_CH):
        acc = jnp.zeros((CONV_CH, um_ref.shape[-1]), F32) + b_ref[...]
        for j in range(CONV_KERNEL):
            a, b = divmod(off + j, 8)
            rows = slice(r0 + 8 * a, r0 + 8 * a + CONV_CH)
            tap = ubuf[rows, :] if b == 0 else shifted[b - 1, rows, :]
            acc = acc + w_ref[j:j + 1, :] * tap
        mu = jnp.mean(acc, axis=-1, keepdims=True)
        xc = acc - mu
        var = jnp.mean(xc * xc, axis=-1, keepdims=True)
        y = xc * lax.rsqrt(var + NORM_EPS) * lg_ref[...] + lb_ref[...]
        y = _silu(y) * gb_ref[r0:r0 + CONV_CH, :].astype(F32)
        o_ref[r0:r0 + CONV_CH, :] = y.astype(BF16)


def _conformer_conv(u, gb, dw_w, dw_b, ln_g, ln_b, n_lat):
    nt, cw = u.shape
    assert nt % CONV_T == 0 and n_lat % CONV_T == 0
    hb = CONV_T // CONV_HALO
    n_halo_blocks = nt // CONV_HALO
    w_pad = jnp.zeros((32, cw), F32).at[:CONV_KERNEL].set(dw_w)
    kern = functools.partial(_conv_kernel, n_lat=n_lat, nt=nt)
    vec = lambda: pl.BlockSpec((1, cw), lambda t: (0, 0))
    return pl.pallas_call(
        kern,
        grid=(nt // CONV_T,),
        in_specs=[
            pl.BlockSpec((CONV_T, cw), lambda t: (t, 0)),
            pl.BlockSpec((CONV_HALO, cw), lambda t: (jnp.maximum(t * hb - 1, 0), 0)),
            pl.BlockSpec((CONV_HALO, cw), lambda t: (jnp.minimum((t + 1) * hb, n_halo_blocks - 1), 0)),
            pl.BlockSpec((CONV_T, cw), lambda t: (t, 0)),
            pl.BlockSpec((32, cw), lambda t: (0, 0)),
            vec(), vec(), vec(),
        ],
        out_specs=pl.BlockSpec((CONV_T, cw), lambda t: (t, 0)),
        out_shape=jax.ShapeDtypeStruct((nt, cw), BF16),
        scratch_shapes=[pltpu.VMEM((CONV_T + 2 * CONV_HALO, cw), F32),
                        pltpu.VMEM((7, CONV_T + 2 * CONV_HALO - 8, cw), F32)],
        compiler_params=_cparams(1),
        name="conformer_conv",
    )(u, u, u, gb, w_pad, dw_b.reshape(1, cw), ln_g.reshape(1, cw), ln_b.reshape(1, cw))


def _out_proj_kernel(a_ref, b_ref, w_ref, x_ref, mod_ref, fg_ref, o_ref, *, n_lat, tm, final):
    i = pl.program_id(0)
    half = a_ref.shape[-1]
    y = _dot(a_ref[...], w_ref[0:half, :]) + _dot(b_ref[...], w_ref[half:, :])
    d = y.shape[-1]
    gt = _mod_rows(mod_ref, 2, _row_is_ctx(i, tm, n_lat), d)
    xn = x_ref[...] + gt * y
    if final:
        ms = jnp.mean(xn * xn, axis=-1, keepdims=True)
        xn = xn * lax.rsqrt(ms + NORM_EPS) * fg_ref[...]
    o_ref[...] = xn


def _out_proj(mix_a, mix_b, cols, w, xs, mod, fg, n_lat, final):
    nt, d = xs.shape
    tm = TM_OUT
    half = w.shape[0] // 2
    kern = functools.partial(_out_proj_kernel, n_lat=n_lat, tm=tm, final=final)
    return pl.pallas_call(
        kern,
        grid=(nt // tm,),
        in_specs=[
            pl.BlockSpec((tm, half), lambda i: (i, cols[0])),
            pl.BlockSpec((tm, half), lambda i: (i, cols[1])),
            pl.BlockSpec((2 * half, d), lambda i: (0, 0)),
            pl.BlockSpec((tm, d), lambda i: (i, 0)),
            pl.BlockSpec((8, 3 * d), lambda i: (0, 0)),
            pl.BlockSpec((1, d), lambda i: (0, 0)),
        ],
        out_specs=pl.BlockSpec((tm, d), lambda i: (i, 0)),
        out_shape=jax.ShapeDtypeStruct((nt, d), F32),
        compiler_params=_cparams(1),
        name="out_proj_final" if final else "out_proj",
    )(mix_a, mix_b, w, xs, mod, fg)


def _rope_tables(n_lat, nc):
    t = jnp.arange(n_lat, dtype=jnp.int32)
    row = (t // GRID_W).astype(F32)
    col = (t % GRID_W).astype(F32)
    n_freq = HEAD_DIM // 4
    inv = ROPE_THETA ** (-jnp.arange(n_freq, dtype=F32) / n_freq)
    cr, sr = jnp.cos(row[:, None] * inv), jnp.sin(row[:, None] * inv)
    cc, sc = jnp.cos(col[:, None] * inv), jnp.sin(col[:, None] * inv)
    cos_t = jnp.concatenate([cr, cr, cc, cc], axis=-1)
    sin_t = jnp.concatenate([-sr, sr, -sc, sc], axis=-1)
    cos_t = jnp.concatenate([cos_t, jnp.ones((nc, HEAD_DIM), F32)], axis=0)
    sin_t = jnp.concatenate([sin_t, jnp.zeros((nc, HEAD_DIM), F32)], axis=0)
    return cos_t, sin_t


def _norm_rope(x, gain, cos_t, sin_t, first_half):
    ms = jnp.mean(x * x, axis=-1, keepdims=True)
    xn = x * lax.rsqrt(ms + NORM_EPS) * gain
    partner = jnp.where(first_half, pltpu.roll(xn, 3 * HEAD_DIM // 4, 1), pltpu.roll(xn, HEAD_DIM // 4, 1))
    return xn * cos_t + partner * sin_t


def _odd_proj_kernel(x_ref, g_ref, mod_ref, w_ref, cos_ref, sin_ref, qn_ref, kn_ref,
                     q_ref, kv_ref, gs_ref, *, n_lat, tm):
    h = _prenorm(x_ref, g_ref, mod_ref, pl.program_id(0), tm, n_lat)
    tn = kv_ref.shape[-1] // 2
    cos_t = cos_ref[...]
    sin_t = sin_ref[...]
    lane = lax.broadcasted_iota(jnp.int32, (1, HEAD_DIM), 1)
    first_half = (lane % (HEAD_DIM // 2)) < (HEAD_DIM // 4)

    def proj(seg):
        return _dot(h, w_ref[:, seg * tn:(seg + 1) * tn])

    def rotated(seg, gain, out_ref, col0):
        acc = proj(seg)
        for hh in range(tn // HEAD_DIM):
            y = _norm_rope(acc[:, hh * HEAD_DIM:(hh + 1) * HEAD_DIM], gain, cos_t, sin_t, first_half)
            out_ref[:, col0 + hh * HEAD_DIM:col0 + (hh + 1) * HEAD_DIM] = y.astype(BF16)

    n_q = q_ref.shape[-1] // tn
    q_gain = qn_ref[...] * ((HEAD_DIM ** -0.5) * LOG2E)
    for seg in range(n_q):
        rotated(seg, q_gain, q_ref, seg * tn)
    rotated(n_q, kn_ref[...], kv_ref, 0)
    kv_ref[:, tn:] = proj(n_q + 1).astype(BF16)
    for seg in range(n_q):
        gs_ref[:, seg * tn:(seg + 1) * tn] = _silu(proj(n_q + 2 + seg)).astype(BF16)


def _odd_proj(xs, g, mod, w, cos_t, sin_t, q_norm, k_norm, n_lat):
    nt, d = xs.shape
    tm = TM_PROJ
    qw = GQA_HEADS * HEAD_DIM
    kw = GQA_KV_HEADS * HEAD_DIM
    assert w.shape[1] == 2 * qw + 2 * kw and nt % tm == 0 and n_lat % tm == 0
    kern = functools.partial(_odd_proj_kernel, n_lat=n_lat, tm=tm)
    rows = lambda width: pl.BlockSpec((tm, width), lambda i: (i, 0))
    vec = lambda width: pl.BlockSpec((1, width), lambda i: (0, 0))
    return pl.pallas_call(
        kern,
        grid=(nt // tm,),
        in_specs=[
            rows(d), vec(d),
            pl.BlockSpec((8, 3 * d), lambda i: (0, 0)),
            pl.BlockSpec(memory_space=pltpu.VMEM),
            rows(HEAD_DIM), rows(HEAD_DIM), vec(HEAD_DIM), vec(HEAD_DIM),
        ],
        out_specs=[rows(qw), rows(2 * kw), rows(qw)],
        out_shape=[
            jax.ShapeDtypeStruct((nt, qw), BF16),
            jax.ShapeDtypeStruct((nt, 2 * kw), BF16),
            jax.ShapeDtypeStruct((nt, qw), BF16),
        ],
        compiler_params=_cparams(1),
        name="odd_proj",
    )(xs, g, mod, w, cos_t, sin_t, q_norm.reshape(1, HEAD_DIM), k_norm.reshape(1, HEAD_DIM))


GQA_SAFE_BOUND = 60.0


def _gqa_kernel(bound_ref, q_ref, k_ref, v_ref, g_ref, o_ref, *, n_lat, nc):
    b = pl.program_id(1)
    tq = q_ref.shape[0]
    rows = GQA_GROUP * tq
    bound = bound_ref[0]
    is_lat = b * tq < n_lat
    q = jnp.concatenate([q_ref[:, h * HEAD_DIM:(h + 1) * HEAD_DIM] for h in range(GQA_GROUP)], axis=0)

    def with_ones(v_chunk):
        return jnp.concatenate([v_chunk, jnp.ones_like(v_chunk)], axis=1)

    def write_out(o):
        for h in range(GQA_GROUP):
            sl = slice(h * HEAD_DIM, (h + 1) * HEAD_DIM)
            o_ref[:, sl] = (o[h * tq:(h + 1) * tq, :] * g_ref[:, sl].astype(F32)).astype(BF16)

    def fixed_shift_chunk(k_chunk, v_chunk):
        p = jnp.exp2(_dot_nt(q, k_chunk) - bound).astype(BF16)
        return _dot(p, with_ones(v_chunk))

    def ctx_chunk():
        return fixed_shift_chunk(k_ref[n_lat:n_lat + nc, :], v_ref[n_lat:n_lat + nc, :])

    def finish(acc):
        write_out(acc[:, :HEAD_DIM] / acc[:, HEAD_DIM:])

    small = bound <= GQA_SAFE_BOUND

    @pl.when(jnp.logical_and(small, is_lat))
    def _():
        acc = ctx_chunk()
        for c in range(n_lat // GQA_TK):
            rows_c = slice(c * GQA_TK, (c + 1) * GQA_TK)
            acc = acc + fixed_shift_chunk(k_ref[rows_c, :], v_ref[rows_c, :])
        finish(acc)

    @pl.when(jnp.logical_and(small, jnp.logical_not(is_lat)))
    def _():
        finish(ctx_chunk())

    @pl.when(bound > GQA_SAFE_BOUND)
    def _():
        def step(carry, k_chunk, v_chunk):
            m, l, acc = carry
            s = _dot_nt(q, k_chunk)
            m_new = jnp.maximum(m, jnp.max(s, axis=-1, keepdims=True))
            alpha = jnp.exp2(m - m_new)
            p = jnp.exp2(s - m_new)
            l = alpha * l + jnp.sum(p, axis=-1, keepdims=True)
            acc = alpha * acc + _dot(p.astype(BF16), v_chunk)
            return m_new, l, acc

        init = (jnp.full((rows, 1), NEG_BIG, F32), jnp.zeros((rows, 1), F32), jnp.zeros((rows, HEAD_DIM), F32))
        carry = step(init, k_ref[n_lat:n_lat + nc, :], v_ref[n_lat:n_lat + nc, :])

        def body(c, carry):
            start = pl.multiple_of(c * GQA_TK_ONLINE, GQA_TK_ONLINE)
            return step(carry, k_ref[pl.ds(start, GQA_TK_ONLINE), :], v_ref[pl.ds(start, GQA_TK_ONLINE), :])

        m, l, acc = lax.fori_loop(0, jnp.where(is_lat, n_lat // GQA_TK_ONLINE, 0), body, carry)
        write_out(acc / l)


def _gqa_attention(q, kv, gs, q_norm, k_norm, n_lat):
    nt = q.shape[0]
    nc = nt - n_lat
    tq = GQA_TQ
    gw = GQA_GROUP * HEAD_DIM
    assert nt % tq == 0 and n_lat % tq == 0 and n_lat % GQA_TK == 0 and n_lat % GQA_TK_ONLINE == 0
    bound = (1.01 * HEAD_DIM * (HEAD_DIM ** -0.5) * LOG2E) * jnp.max(jnp.abs(q_norm)) * jnp.max(jnp.abs(k_norm))
    kern = functools.partial(_gqa_kernel, n_lat=n_lat, nc=nc)
    return pl.pallas_call(
        kern,
        grid=(GQA_KV_HEADS, nt // tq),
        in_specs=[
            pl.BlockSpec(memory_space=pltpu.SMEM),
            pl.BlockSpec((tq, gw), lambda h, b: (b, h)),
            pl.BlockSpec((nt, HEAD_DIM), lambda h, b: (0, h)),
            pl.BlockSpec((nt, HEAD_DIM), lambda h, b: (0, GQA_KV_HEADS + h)),
            pl.BlockSpec((tq, gw), lambda h, b: (b, h)),
        ],
        out_specs=pl.BlockSpec((tq, gw), lambda h, b: (b, h)),
        out_shape=jax.ShapeDtypeStruct((nt, GQA_HEADS * HEAD_DIM), BF16),
        compiler_params=_cparams(2),
        name="gqa_attention",
    )(bound.reshape(1).astype(F32), q, kv, kv, gs)


def kernel(x, c, ctx, c_ctx, norm_g, w_mod, b_mod, e_w_in, e_rpb, e_dw_w, e_dw_b, e_ln_g, e_ln_b, e_w_out,
           o_w_in, o_q_norm, o_k_norm, o_w_out, final_norm_g):
    batch, n_lat, d = x.shape
    nc = ctx.shape[1]
    depth = norm_g.shape[0]
    assert batch == 1
    rows = n_lat // GRID_W

    xs = jnp.concatenate([x[0], ctx[0]], axis=0)
    cvec = jnp.zeros((8, d), F32).at[0].set(c[0]).at[1].set(c_ctx)
    mods = _modulation(cvec, w_mod, b_mod)
    cos_t, sin_t = _rope_tables(n_lat, nc)
    fg = final_norm_g.reshape(1, d)

    for l in range(depth):
        i = l // 2
        g = norm_g[l].reshape(1, d)
        mod = mods[l]
        final = l == depth - 1
        if l % 2 == 0:
            qkv, ga, u, gb = _even_proj(xs, g, mod, e_w_in[i].astype(BF16), n_lat)
            mix_a = _na_attention(qkv, ga, _na_bias_table(e_rpb[i], rows, nc), n_lat)
            mix_b = _conformer_conv(u, gb, e_dw_w[i], e_dw_b[i], e_ln_g[i], e_ln_b[i], n_lat)
            xs = _out_proj(mix_a, mix_b, (0, 0), e_w_out[i].astype(BF16), xs, mod, fg, n_lat, final)
        else:
            q, kv, gs = _odd_proj(xs, g, mod, o_w_in[i].astype(BF16), cos_t, sin_t,
                                  o_q_norm[i], o_k_norm[i], n_lat)
            mix = _gqa_attention(q, kv, gs, o_q_norm[i], o_k_norm[i], n_lat)
            xs = _out_proj(mix, mix, (0, 1), o_w_out[i].astype(BF16), xs, mod, fg, n_lat, final)
    return xs[:n_lat][None]
```

```python
import functools

import numpy as np
import jax
import jax.numpy as jnp
from jax import lax
from jax.experimental import pallas as pl
from jax.experimental.pallas import tpu as pltpu

F32 = jnp.float32
BF16 = jnp.bfloat16

GRID_W = 64
HEAD_DIM = 128
NORM_EPS = 1e-6
NA_HEADS = 8
NA_WIN_ROWS = 8
NA_WIN_COLS = 16
CONV_KERNEL = 31
GQA_HEADS = 16
GQA_KV_HEADS = 4
GQA_GROUP = GQA_HEADS // GQA_KV_HEADS
ROPE_THETA = 10000.0
LOG2E = 1.4426950408889634

LANES = 128
V7X_VMEM_BYTES = 64 * 1024 * 1024
VMEM_LIMIT = V7X_VMEM_BYTES - 8 * 1024 * 1024

TM_PROJ = 256
TM_OUT = 256
NA_ROWS = 4
NA_SLAB = NA_WIN_ROWS + NA_ROWS
NA_TQ = NA_ROWS * GRID_W
NA_HP = 2
CONV_T = 128
CONV_HALO = 16
CONV_CH = 32
GQA_TQ = 256
GQA_TK = 2048
GQA_TK_ONLINE = 512
NEG_BIG = -1e30


def _cparams(n_axes):
    return pltpu.CompilerParams(dimension_semantics=("arbitrary",) * n_axes, vmem_limit_bytes=VMEM_LIMIT)


def _sigmoid(x):
    return 1.0 / (1.0 + jnp.exp(-x))


def _silu(x):
    return x * _sigmoid(x)


def _dot(a, b):
    return jnp.dot(a, b, preferred_element_type=F32)


def _dot_nt(a, b):
    return lax.dot_general(a, b, (((1,), (1,)), ((), ())), preferred_element_type=F32)


def _row_is_ctx(tile_idx, tm, n_lat):
    rows = tile_idx * tm + lax.broadcasted_iota(jnp.int32, (tm, 1), 0)
    return rows >= n_lat


def _mod_rows(mod_ref, part, is_ctx, d):
    lat = mod_ref[0:1, part * d:(part + 1) * d]
    ctx = mod_ref[1:2, part * d:(part + 1) * d]
    return jnp.where(is_ctx, ctx, lat)


def _prenorm(x_ref, g_ref, mod_ref, tile_idx, tm, n_lat):
    d = x_ref.shape[-1]
    x = x_ref[...]
    ms = jnp.mean(x * x, axis=-1, keepdims=True)
    row = (tile_idx * tm >= n_lat).astype(jnp.int32)
    mod = mod_ref[pl.ds(row, 1), :]
    a = g_ref[...] * (1.0 + mod[:, d:2 * d])
    return (x * lax.rsqrt(ms + NORM_EPS) * a + mod[:, 0:d]).astype(BF16)


def _mod_kernel(c_ref, w_ref, b_ref, o_ref):
    s = _silu(c_ref[...]).astype(BF16)
    o_ref[...] = _dot(s, w_ref[...].astype(BF16)) + b_ref[...]


def _modulation(cvec, w_mod, b_mod):
    depth, d, d3 = w_mod.shape
    tn = 768
    return pl.pallas_call(
        _mod_kernel,
        grid=(depth, d3 // tn),
        in_specs=[
            pl.BlockSpec((8, d), lambda l, j: (0, 0)),
            pl.BlockSpec((None, d, tn), lambda l, j: (l, 0, j)),
            pl.BlockSpec((None, 1, tn), lambda l, j: (l, 0, j)),
        ],
        out_specs=pl.BlockSpec((None, 8, tn), lambda l, j: (l, 0, j)),
        out_shape=jax.ShapeDtypeStruct((depth, 8, d3), F32),
        compiler_params=_cparams(2),
        name="modulation",
    )(cvec, w_mod, b_mod.reshape(depth, 1, d3))


def _even_proj_kernel(x_ref, g_ref, mod_ref, w_ref, qkv_ref, ga_ref, u_ref, gb_ref, *, n_lat, tm):
    h = _prenorm(x_ref, g_ref, mod_ref, pl.program_id(0), tm, n_lat)
    cw = ga_ref.shape[-1]

    def proj(seg):
        return _dot(h, w_ref[:, seg * cw:(seg + 1) * cw])

    for seg in range(3):
        qkv_ref[:, seg * cw:(seg + 1) * cw] = proj(seg).astype(BF16)
    ga_ref[...] = _silu(proj(3)).astype(BF16)
    u_ref[...] = proj(4) * _sigmoid(proj(5))
    gb_ref[...] = _silu(proj(6)).astype(BF16)


def _resident_weight_spec(w, layer):
    return pl.BlockSpec((None,) + w.shape[1:], lambda i: (layer, 0, 0), pipeline_mode=pl.Buffered(1))


def _even_proj(xs, g, mod, w, layer, n_lat):
    nt, d = xs.shape
    tm = TM_PROJ
    cw = w.shape[2] // 7
    assert nt % tm == 0 and n_lat % tm == 0
    kern = functools.partial(_even_proj_kernel, n_lat=n_lat, tm=tm)
    rows = lambda width: pl.BlockSpec((tm, width), lambda i: (i, 0))
    return pl.pallas_call(
        kern,
        grid=(nt // tm,),
        in_specs=[
            rows(d),
            pl.BlockSpec((1, d), lambda i: (0, 0)),
            pl.BlockSpec((8, 3 * d), lambda i: (0, 0)),
            _resident_weight_spec(w, layer),
        ],
        out_specs=[rows(3 * cw), rows(cw), rows(cw), rows(cw)],
        out_shape=[
            jax.ShapeDtypeStruct((nt, 3 * cw), BF16),
            jax.ShapeDtypeStruct((nt, cw), BF16),
            jax.ShapeDtypeStruct((nt, cw), F32),
            jax.ShapeDtypeStruct((nt, cw), BF16),
        ],
        compiler_params=_cparams(1),
        name="even_proj",
    )(xs, g, mod, w)


def _na_bias_table(rpb, rows, nc):
    pad = GRID_W - NA_WIN_COLS
    rp = jnp.pad(rpb.astype(F32), ((0, 0), (0, 0), (pad, pad)))
    toe = jnp.stack([rp[:, :, GRID_W - 1 - qc:2 * GRID_W - 1 - qc] for qc in range(GRID_W)], axis=2)
    qc = np.arange(GRID_W)[:, None]
    kc = np.arange(GRID_W)[None, :]
    cs = np.clip(qc - NA_WIN_COLS // 2, 0, GRID_W - NA_WIN_COLS)
    col_ok = (kc >= cs) & (kc < cs + NA_WIN_COLS)
    toe = jnp.where(col_ok, toe, NEG_BIG)
    masked = jnp.full((rpb.shape[0], GRID_W, GRID_W), NEG_BIG, F32)
    pats = []
    for r0 in (0, 2 * NA_ROWS, rows - NA_ROWS):
        ks = int(np.clip(r0 - NA_WIN_ROWS // 2, 0, rows - NA_SLAB))
        q_blocks = []
        for r in range(r0, r0 + NA_ROWS):
            rs = int(np.clip(r - NA_WIN_ROWS // 2, 0, rows - NA_WIN_ROWS))
            k_blocks = [toe[:, kr - r + NA_WIN_ROWS - 1] if rs <= kr < rs + NA_WIN_ROWS else masked
                        for kr in range(ks, ks + NA_SLAB)]
            q_blocks.append(jnp.concatenate(k_blocks, axis=-1))
        pat = jnp.concatenate(q_blocks, axis=1)
        pats.append(jnp.concatenate([pat, jnp.zeros(pat.shape[:2] + (nc,), F32)], axis=-1))
    return jnp.stack(pats)


def _na_kernel(q_ref, k_ref, v_ref, bias_ref, ga_ref, o_ref, *, n_lat, nc, rows):
    b = pl.program_id(1)
    nb_lat = rows // NA_ROWS
    scale = HEAD_DIM ** -0.5

    def attend(h, keys, values, bias):
        sl = slice(h * HEAD_DIM, (h + 1) * HEAD_DIM)
        s = _dot_nt(q_ref[:, sl], keys) * scale
        if bias is not None:
            s = s + bias
        p = jnp.exp(s - jnp.max(s, axis=-1, keepdims=True)).astype(BF16)
        o = _dot(p, jnp.concatenate([values, jnp.ones_like(values)], axis=1))
        gate = ga_ref[:, sl].astype(F32)
        o_ref[:, sl] = (o[:, :HEAD_DIM] / o[:, HEAD_DIM:] * gate).astype(BF16)

    @pl.when(b < nb_lat)
    def _():
        ks = jnp.clip(b * NA_ROWS - NA_WIN_ROWS // 2, 0, rows - NA_SLAB)
        start = pl.multiple_of(ks * GRID_W, GRID_W)
        for h in range(NA_HP):
            sl = slice(h * HEAD_DIM, (h + 1) * HEAD_DIM)
            keys = jnp.concatenate([k_ref[pl.ds(start, NA_SLAB * GRID_W), sl], k_ref[n_lat:n_lat + nc, sl]], axis=0)
            values = jnp.concatenate([v_ref[pl.ds(start, NA_SLAB * GRID_W), sl], v_ref[n_lat:n_lat + nc, sl]], axis=0)
            attend(h, keys, values, bias_ref[h])

    @pl.when(b >= nb_lat)
    def _():
        for h in range(NA_HP):
            sl = slice(h * HEAD_DIM, (h + 1) * HEAD_DIM)
            attend(h, k_ref[n_lat:n_lat + nc, sl], v_ref[n_lat:n_lat + nc, sl], None)


def _na_attention(qkv, ga, bias_tab, n_lat):
    nt = qkv.shape[0]
    nc = nt - n_lat
    rows = n_lat // GRID_W
    assert nc == NA_TQ and rows % NA_ROWS == 0 and rows >= NA_SLAB
    nb = nt // NA_TQ
    nb_lat = rows // NA_ROWS
    n_keys = NA_SLAB * GRID_W + nc
    hw = NA_HP * HEAD_DIM
    groups = NA_HEADS // NA_HP

    def pat(b):
        return jnp.where(b == 0, 0, jnp.where(b >= nb_lat - 1, 2, 1))

    kern = functools.partial(_na_kernel, n_lat=n_lat, nc=nc, rows=rows)
    return pl.pallas_call(
        kern,
        grid=(groups, nb),
        in_specs=[
            pl.BlockSpec((NA_TQ, hw), lambda h, b: (b, h)),
            pl.BlockSpec((nt, hw), lambda h, b: (0, groups + h)),
            pl.BlockSpec((nt, hw), lambda h, b: (0, 2 * groups + h)),
            pl.BlockSpec((None, NA_HP, NA_TQ, n_keys), lambda h, b: (pat(b), h, 0, 0)),
            pl.BlockSpec((NA_TQ, hw), lambda h, b: (b, h)),
        ],
        out_specs=pl.BlockSpec((NA_TQ, hw), lambda h, b: (b, h)),
        out_shape=jax.ShapeDtypeStruct((nt, NA_HEADS * HEAD_DIM), BF16),
        compiler_params=_cparams(2),
        name="na_attention",
    )(qkv, qkv, qkv, bias_tab, ga)


def _conv_kernel(um_ref, up_ref, un_ref, gb_ref, w_ref, b_ref, lg_ref, lb_ref, o_ref, ubuf, shifted, *, n_lat, nt):
    t = pl.program_id(0)
    start = t * CONV_T
    prev_ok = jnp.logical_and(start != 0, start != n_lat)
    next_ok = jnp.logical_and(start + CONV_T != n_lat, start + CONV_T != nt)
    ubuf[0:CONV_HALO, :] = jnp.where(prev_ok, up_ref[...], 0.0)
    ubuf[CONV_HALO:CONV_HALO + CONV_T, :] = um_ref[...]
    ubuf[CONV_HALO + CONV_T:, :] = jnp.where(next_ok, un_ref[...], 0.0)
    span = CONV_T + 2 * CONV_HALO - 8
    for b in range(1, 8):
        shifted[b - 1, :, :] = ubuf[b:b + span, :]
    off = CONV_HALO - CONV_KERNEL // 2
    for r0 in range(0, CONV_T, CONV_CH):
        acc = jnp.zeros((CONV_CH, um_ref.shape[-1]), F32) + b_ref[...]
        for j in range(CONV_KERNEL):
            a, b = divmod(off + j, 8)
            rows = slice(r0 + 8 * a, r0 + 8 * a + CONV_CH)
            tap = ubuf[rows, :] if b == 0 else shifted[b - 1, rows, :]
            acc = acc + w_ref[j:j + 1, :] * tap
        mu = jnp.mean(acc, axis=-1, keepdims=True)
        xc = acc - mu
        var = jnp.mean(xc * xc, axis=-1, keepdims=True)
        y = xc * lax.rsqrt(var + NORM_EPS) * lg_ref[...] + lb_ref[...]
        y = _silu(y) * gb_ref[r0:r0 + CONV_CH, :].astype(F32)
        o_ref[r0:r0 + CONV_CH, :] = y.astype(BF16)


def _conformer_conv(u, gb, dw_w, dw_b, ln_g, ln_b, n_lat):
    nt, cw = u.shape
    assert nt % CONV_T == 0 and n_lat % CONV_T == 0
    hb = CONV_T // CONV_HALO
    n_halo_blocks = nt // CONV_HALO
    w_pad = jnp.zeros((32, cw), F32).at[:CONV_KERNEL].set(dw_w)
    kern = functools.partial(_conv_kernel, n_lat=n_lat, nt=nt)
    vec = lambda: pl.BlockSpec((1, cw), lambda t: (0, 0))
    return pl.pallas_call(
        kern,
        grid=(nt // CONV_T,),
        in_specs=[
            pl.BlockSpec((CONV_T, cw), lambda t: (t, 0)),
            pl.BlockSpec((CONV_HALO, cw), lambda t: (jnp.maximum(t * hb - 1, 0), 0)),
            pl.BlockSpec((CONV_HALO, cw), lambda t: (jnp.minimum((t + 1) * hb, n_halo_blocks - 1), 0)),
            pl.BlockSpec((CONV_T, cw), lambda t: (t, 0)),
            pl.BlockSpec((32, cw), lambda t: (0, 0)),
            vec(), vec(), vec(),
        ],
        out_specs=pl.BlockSpec((CONV_T, cw), lambda t: (t, 0)),
        out_shape=jax.ShapeDtypeStruct((nt, cw), BF16),
        scratch_shapes=[pltpu.VMEM((CONV_T + 2 * CONV_HALO, cw), F32),
                        pltpu.VMEM((7, CONV_T + 2 * CONV_HALO - 8, cw), F32)],
        compiler_params=_cparams(1),
        name="conformer_conv",
    )(u, u, u, gb, w_pad, dw_b.reshape(1, cw), ln_g.reshape(1, cw), ln_b.reshape(1, cw))


def _out_proj_kernel(a_ref, b_ref, w_ref, x_ref, mod_ref, fg_ref, o_ref, *, n_lat, tm, final):
    i = pl.program_id(0)
    half = a_ref.shape[-1]
    y = _dot(a_ref[...], w_ref[0:half, :]) + _dot(b_ref[...], w_ref[half:, :])
    d = y.shape[-1]
    gt = _mod_rows(mod_ref, 2, _row_is_ctx(i, tm, n_lat), d)
    xn = x_ref[...] + gt * y
    if final:
        ms = jnp.mean(xn * xn, axis=-1, keepdims=True)
        xn = xn * lax.rsqrt(ms + NORM_EPS) * fg_ref[...]
    o_ref[...] = xn


def _out_proj(mix_a, mix_b, cols, w, xs, mod, fg, n_lat, final):
    nt, d = xs.shape
    tm = TM_OUT
    half = w.shape[0] // 2
    out_rows = n_lat if final else nt
    assert out_rows % tm == 0
    kern = functools.partial(_out_proj_kernel, n_lat=n_lat, tm=tm, final=final)
    return pl.pallas_call(
        kern,
        grid=(out_rows // tm,),
        in_specs=[
            pl.BlockSpec((tm, half), lambda i: (i, cols[0])),
            pl.BlockSpec((tm, half), lambda i: (i, cols[1])),
            pl.BlockSpec((2 * half, d), lambda i: (0, 0)),
            pl.BlockSpec((tm, d), lambda i: (i, 0)),
            pl.BlockSpec((8, 3 * d), lambda i: (0, 0)),
            pl.BlockSpec((1, d), lambda i: (0, 0)),
        ],
        out_specs=pl.BlockSpec((tm, d), lambda i: (i, 0)),
        out_shape=jax.ShapeDtypeStruct((out_rows, d), F32),
        compiler_params=_cparams(1),
        name="out_proj_final" if final else "out_proj",
    )(mix_a, mix_b, w, xs, mod, fg)


def _rope_tables(n_lat, nc):
    rows = n_lat // GRID_W
    n_freq = HEAD_DIM // 4
    inv = ROPE_THETA ** (-jnp.arange(n_freq, dtype=F32) / n_freq)
    row_ang = jnp.arange(rows, dtype=jnp.int32).astype(F32)[:, None] * inv
    col_ang = jnp.arange(GRID_W, dtype=jnp.int32).astype(F32)[:, None] * inv
    per_row = lambda a: jnp.repeat(a, GRID_W, axis=0)
    per_col = lambda a: jnp.tile(a, (rows, 1))
    cr, sr = per_row(jnp.cos(row_ang)), per_row(jnp.sin(row_ang))
    cc, sc = per_col(jnp.cos(col_ang)), per_col(jnp.sin(col_ang))
    cos_t = jnp.concatenate([cr, cr, cc, cc], axis=-1)
    sin_t = jnp.concatenate([-sr, sr, -sc, sc], axis=-1)
    cos_t = jnp.concatenate([cos_t, jnp.ones((nc, HEAD_DIM), F32)], axis=0)
    sin_t = jnp.concatenate([sin_t, jnp.zeros((nc, HEAD_DIM), F32)], axis=0)
    return cos_t, sin_t


def _norm_rope(x, gain, cos_t, sin_t, first_half):
    ms = jnp.mean(x * x, axis=-1, keepdims=True)
    xn = x * lax.rsqrt(ms + NORM_EPS) * gain
    partner = jnp.where(first_half, pltpu.roll(xn, 3 * HEAD_DIM // 4, 1), pltpu.roll(xn, HEAD_DIM // 4, 1))
    return xn * cos_t + partner * sin_t


def _odd_proj_kernel(x_ref, g_ref, mod_ref, w_ref, wvt_ref, cos_ref, sin_ref, qn_ref, kn_ref,
                     q_ref, k_ref, vt_ref, gs_ref, *, n_lat, tm):
    h = _prenorm(x_ref, g_ref, mod_ref, pl.program_id(0), tm, n_lat)
    tn = k_ref.shape[-1]
    cos_t = cos_ref[...]
    sin_t = sin_ref[...]
    lane = lax.broadcasted_iota(jnp.int32, (1, HEAD_DIM), 1)
    first_half = (lane % (HEAD_DIM // 2)) < (HEAD_DIM // 4)

    def proj(seg):
        return _dot(h, w_ref[:, seg * tn:(seg + 1) * tn])

    def rotated(seg, gain, out_ref, col0):
        acc = proj(seg)
        for hh in range(tn // HEAD_DIM):
            y = _norm_rope(acc[:, hh * HEAD_DIM:(hh + 1) * HEAD_DIM], gain, cos_t, sin_t, first_half)
            out_ref[:, col0 + hh * HEAD_DIM:col0 + (hh + 1) * HEAD_DIM] = y.astype(BF16)

    n_q = q_ref.shape[-1] // tn
    q_gain = qn_ref[...] * ((HEAD_DIM ** -0.5) * LOG2E)
    for seg in range(n_q):
        rotated(seg, q_gain, q_ref, seg * tn)
    rotated(n_q, kn_ref[...], k_ref, 0)
    v_t = _dot_nt(wvt_ref[...], h)
    vt_ref[...] = v_t.reshape(vt_ref.shape).astype(BF16)
    for seg in range(n_q):
        gs_ref[:, seg * tn:(seg + 1) * tn] = _silu(proj(n_q + 2 + seg)).astype(BF16)


def _odd_proj(xs, g, mod, w, layer, cos_t, sin_t, q_norm, k_norm, n_lat):
    nt, d = xs.shape
    tm = TM_PROJ
    qw = GQA_HEADS * HEAD_DIM
    kw = GQA_KV_HEADS * HEAD_DIM
    assert w.shape[2] == 2 * qw + 2 * kw and nt % tm == 0 and n_lat % tm == 0
    kern = functools.partial(_odd_proj_kernel, n_lat=n_lat, tm=tm)
    rows = lambda width: pl.BlockSpec((tm, width), lambda i: (i, 0))
    vec = lambda width: pl.BlockSpec((1, width), lambda i: (0, 0))
    return pl.pallas_call(
        kern,
        grid=(nt // tm,),
        in_specs=[
            rows(d), vec(d),
            pl.BlockSpec((8, 3 * d), lambda i: (0, 0)),
            _resident_weight_spec(w, layer),
            pl.BlockSpec((kw, d), lambda i: (0, 0)),
            rows(HEAD_DIM), rows(HEAD_DIM), vec(HEAD_DIM), vec(HEAD_DIM),
        ],
        out_specs=[rows(qw), rows(kw),
                   pl.BlockSpec((GQA_KV_HEADS, None, HEAD_DIM, tm), lambda i: (0, i, 0, 0)),
                   rows(qw)],
        out_shape=[
            jax.ShapeDtypeStruct((nt, qw), BF16),
            jax.ShapeDtypeStruct((nt, kw), BF16),
            jax.ShapeDtypeStruct((GQA_KV_HEADS, nt // tm, HEAD_DIM, tm), BF16),
            jax.ShapeDtypeStruct((nt, qw), BF16),
        ],
        compiler_params=_cparams(1),
        name="odd_proj",
    )(xs, g, mod, w, w[layer, :, qw + kw:qw + 2 * kw].T, cos_t, sin_t,
      q_norm.reshape(1, HEAD_DIM), k_norm.reshape(1, HEAD_DIM))


GQA_SAFE_BOUND = 60.0


GQA_ONES_ROWS = 16


def _gqa_kernel(bound_ref, q_ref, k_ref, vt_ref, g_ref, o_ref, *, n_lat, nc):
    b = pl.program_id(1)
    tq = q_ref.shape[0]
    vb = vt_ref.shape[-1]
    bound = bound_ref[0]
    is_lat = b * tq < n_lat
    q = jnp.concatenate([q_ref[:, h * HEAD_DIM:(h + 1) * HEAD_DIM] for h in range(GQA_GROUP)], axis=0)

    def vt_chunk(first_block, n_blocks):
        return jnp.concatenate([vt_ref[first_block + j] for j in range(n_blocks)], axis=1)

    def write_out(o_t):
        o = o_t.T
        for h in range(GQA_GROUP):
            sl = slice(h * HEAD_DIM, (h + 1) * HEAD_DIM)
            o_ref[:, sl] = (o[h * tq:(h + 1) * tq, :] * g_ref[:, sl].astype(F32)).astype(BF16)

    def fixed_shift_chunk(row0, n_rows):
        p = jnp.exp2(_dot_nt(k_ref[row0:row0 + n_rows, :], q) - bound).astype(BF16)
        vt = vt_chunk(row0 // vb, n_rows // vb)
        vt = jnp.concatenate([vt, jnp.ones((GQA_ONES_ROWS, n_rows), BF16)], axis=0)
        return _dot(vt, p)

    def finish(acc):
        write_out(acc[:HEAD_DIM, :] / acc[HEAD_DIM:HEAD_DIM + 1, :])

    small = bound <= GQA_SAFE_BOUND

    @pl.when(jnp.logical_and(small, is_lat))
    def _():
        acc = fixed_shift_chunk(n_lat, nc)
        for c in range(n_lat // GQA_TK):
            acc = acc + fixed_shift_chunk(c * GQA_TK, GQA_TK)
        finish(acc)

    @pl.when(jnp.logical_and(small, jnp.logical_not(is_lat)))
    def _():
        finish(fixed_shift_chunk(n_lat, nc))

    @pl.when(bound > GQA_SAFE_BOUND)
    def _():
        cols = GQA_GROUP * tq
        per = GQA_TK_ONLINE // vb

        def step(carry, k_chunk, vt):
            m, l, acc = carry
            s = _dot_nt(k_chunk, q)
            m_new = jnp.maximum(m, jnp.max(s, axis=0, keepdims=True))
            alpha = jnp.exp2(m - m_new)
            p = jnp.exp2(s - m_new)
            l = alpha * l + jnp.sum(p, axis=0, keepdims=True)
            acc = alpha * acc + _dot(vt, p.astype(BF16))
            return m_new, l, acc

        init = (jnp.full((1, cols), NEG_BIG, F32), jnp.zeros((1, cols), F32), jnp.zeros((HEAD_DIM, cols), F32))
        carry = step(init, k_ref[n_lat:n_lat + nc, :], vt_chunk(n_lat // vb, nc // vb))

        def body(c, carry):
            start = pl.multiple_of(c * GQA_TK_ONLINE, GQA_TK_ONLINE)
            vt = jnp.concatenate([vt_ref[c * per + j] for j in range(per)], axis=1)
            return step(carry, k_ref[pl.ds(start, GQA_TK_ONLINE), :], vt)

        m, l, acc = lax.fori_loop(0, jnp.where(is_lat, n_lat // GQA_TK_ONLINE, 0), body, carry)
        write_out(acc / l)


def _gqa_attention(q, k, vt, gs, q_norm, k_norm, n_lat):
    nt = q.shape[0]
    nc = nt - n_lat
    tq = GQA_TQ
    gw = GQA_GROUP * HEAD_DIM
    n_vb, vb = vt.shape[1], vt.shape[3]
    assert nt % tq == 0 and n_lat % tq == 0 and n_lat % GQA_TK == 0 and n_lat % GQA_TK_ONLINE == 0
    assert n_vb * vb == nt and n_lat % vb == 0 and nc % vb == 0 and GQA_TK_ONLINE % vb == 0
    bound = (1.01 * HEAD_DIM * (HEAD_DIM ** -0.5) * LOG2E) * jnp.max(jnp.abs(q_norm)) * jnp.max(jnp.abs(k_norm))
    kern = functools.partial(_gqa_kernel, n_lat=n_lat, nc=nc)
    return pl.pallas_call(
        kern,
        grid=(GQA_KV_HEADS, nt // tq),
        in_specs=[
            pl.BlockSpec(memory_space=pltpu.SMEM),
            pl.BlockSpec((tq, gw), lambda h, b: (b, h)),
            pl.BlockSpec((nt, HEAD_DIM), lambda h, b: (0, h)),
            pl.BlockSpec((None, n_vb, HEAD_DIM, vb), lambda h, b: (h, 0, 0, 0)),
            pl.BlockSpec((tq, gw), lambda h, b: (b, h)),
        ],
        out_specs=pl.BlockSpec((tq, gw), lambda h, b: (b, h)),
        out_shape=jax.ShapeDtypeStruct((nt, GQA_HEADS * HEAD_DIM), BF16),
        compiler_params=_cparams(2),
        name="gqa_attention",
    )(bound.reshape(1).astype(F32), q, k, vt, gs)


def kernel(x, c, ctx, c_ctx, norm_g, w_mod, b_mod, e_w_in, e_rpb, e_dw_w, e_dw_b, e_ln_g, e_ln_b, e_w_out,
           o_w_in, o_q_norm, o_k_norm, o_w_out, final_norm_g):
    batch, n_lat, d = x.shape
    nc = ctx.shape[1]
    depth = norm_g.shape[0]
    assert batch == 1
    rows = n_lat // GRID_W

    xs = jnp.concatenate([x[0], ctx[0]], axis=0)
    cvec = jnp.zeros((8, d), F32).at[0].set(c[0]).at[1].set(c_ctx)
    mods = _modulation(cvec, w_mod, b_mod)
    cos_t, sin_t = _rope_tables(n_lat, nc)
    fg = final_norm_g.reshape(1, d)
    e_w_in_b = e_w_in.astype(BF16)
    o_w_in_b = o_w_in.astype(BF16)

    for l in range(depth):
        i = l // 2
        g = norm_g[l].reshape(1, d)
        mod = mods[l]
        final = l == depth - 1
        if l % 2 == 0:
            qkv, ga, u, gb = _even_proj(xs, g, mod, e_w_in_b, i, n_lat)
            mix_a = _na_attention(qkv, ga, _na_bias_table(e_rpb[i], rows, nc), n_lat)
            mix_b = _conformer_conv(u, gb, e_dw_w[i], e_dw_b[i], e_ln_g[i], e_ln_b[i], n_lat)
            xs = _out_proj(mix_a, mix_b, (0, 0), e_w_out[i].astype(BF16), xs, mod, fg, n_lat, final)
        else:
            q, k, vt, gs = _odd_proj(xs, g, mod, o_w_in_b, i, cos_t, sin_t,
                                     o_q_norm[i], o_k_norm[i], n_lat)
            mix = _gqa_attention(q, k, vt, gs, o_q_norm[i], o_k_norm[i], n_lat)
            xs = _out_proj(mix, mix, (0, 1), o_w_out[i].astype(BF16), xs, mod, fg, n_lat, final)
    return xs[None]
```

```python
import functools

import numpy as np
import jax
import jax.numpy as jnp
from jax import lax
from jax.experimental import pallas as pl
from jax.experimental.pallas import tpu as pltpu

F32 = jnp.float32
BF16 = jnp.bfloat16

GRID_W = 64
HEAD_DIM = 128
NORM_EPS = 1e-6
NA_HEADS = 8
NA_WIN_ROWS = 8
NA_WIN_COLS = 16
CONV_KERNEL = 31
GQA_HEADS = 16
GQA_KV_HEADS = 4
GQA_GROUP = GQA_HEADS // GQA_KV_HEADS
ROPE_THETA = 10000.0
LOG2E = 1.4426950408889634

LANES = 128
V7X_VMEM_BYTES = 64 * 1024 * 1024
VMEM_LIMIT = V7X_VMEM_BYTES - 8 * 1024 * 1024

TM_PROJ = 256
TM_OUT = 256
NA_ROWS = 4
NA_SLAB = NA_WIN_ROWS + NA_ROWS
NA_TQ = NA_ROWS * GRID_W
NA_HP = 4
CONV_T = 128
CONV_HALO = 16
CONV_CH = 32
GQA_TQ = 256
GQA_TK = 2048
GQA_TK_ONLINE = 512
NEG_BIG = -1e30


def _cparams(n_axes):
    return pltpu.CompilerParams(dimension_semantics=("arbitrary",) * n_axes, vmem_limit_bytes=VMEM_LIMIT)


def _sigmoid(x):
    return 1.0 / (1.0 + jnp.exp(-x))


def _silu(x):
    return x * _sigmoid(x)


def _dot(a, b):
    return jnp.dot(a, b, preferred_element_type=F32)


def _dot_nt(a, b):
    return lax.dot_general(a, b, (((1,), (1,)), ((), ())), preferred_element_type=F32)


def _row_is_ctx(tile_idx, tm, n_lat):
    rows = tile_idx * tm + lax.broadcasted_iota(jnp.int32, (tm, 1), 0)
    return rows >= n_lat


def _mod_rows(mod_ref, part, is_ctx, d):
    lat = mod_ref[0:1, part * d:(part + 1) * d]
    ctx = mod_ref[1:2, part * d:(part + 1) * d]
    return jnp.where(is_ctx, ctx, lat)


def _prenorm(x_ref, g_ref, mod_ref, tile_idx, tm, n_lat):
    d = x_ref.shape[-1]
    x = x_ref[...]
    ms = jnp.mean(x * x, axis=-1, keepdims=True)
    row = (tile_idx * tm >= n_lat).astype(jnp.int32)
    mod = mod_ref[pl.ds(row, 1), :]
    a = g_ref[...] * (1.0 + mod[:, d:2 * d])
    return (x * lax.rsqrt(ms + NORM_EPS) * a + mod[:, 0:d]).astype(BF16)


def _mod_kernel(c_ref, w_ref, b_ref, o_ref):
    s = _silu(c_ref[...]).astype(BF16)
    o_ref[...] = _dot(s, w_ref[...].astype(BF16)) + b_ref[...]


def _modulation(cvec, w_mod, b_mod):
    depth, d, d3 = w_mod.shape
    tn = 768
    return pl.pallas_call(
        _mod_kernel,
        grid=(depth, d3 // tn),
        in_specs=[
            pl.BlockSpec((8, d), lambda l, j: (0, 0)),
            pl.BlockSpec((None, d, tn), lambda l, j: (l, 0, j)),
            pl.BlockSpec((None, 1, tn), lambda l, j: (l, 0, j)),
        ],
        out_specs=pl.BlockSpec((None, 8, tn), lambda l, j: (l, 0, j)),
        out_shape=jax.ShapeDtypeStruct((depth, 8, d3), F32),
        compiler_params=_cparams(2),
        name="modulation",
    )(cvec, w_mod, b_mod.reshape(depth, 1, d3))


def _even_proj_kernel(x_ref, g_ref, mod_ref, w_ref, qkv_ref, ga_ref, u_ref, gb_ref, *, n_lat, tm):
    h = _prenorm(x_ref, g_ref, mod_ref, pl.program_id(0), tm, n_lat)
    cw = ga_ref.shape[-1]

    def proj(seg):
        return _dot(h, w_ref[:, seg * cw:(seg + 1) * cw])

    for seg in range(3):
        qkv_ref[:, seg * cw:(seg + 1) * cw] = proj(seg).astype(BF16)
    ga_ref[...] = _silu(proj(3)).astype(BF16)
    u_ref[...] = proj(4) * _sigmoid(proj(5))
    gb_ref[...] = _silu(proj(6)).astype(BF16)


def _resident_weight_spec(w, layer):
    return pl.BlockSpec((None,) + w.shape[1:], lambda i: (layer, 0, 0), pipeline_mode=pl.Buffered(1))


def _even_proj(xs, g, mod, w, layer, n_lat):
    nt, d = xs.shape
    tm = TM_PROJ
    cw = w.shape[2] // 7
    assert nt % tm == 0 and n_lat % tm == 0
    kern = functools.partial(_even_proj_kernel, n_lat=n_lat, tm=tm)
    rows = lambda width: pl.BlockSpec((tm, width), lambda i: (i, 0))
    return pl.pallas_call(
        kern,
        grid=(nt // tm,),
        in_specs=[
            rows(d),
            pl.BlockSpec((1, d), lambda i: (0, 0)),
            pl.BlockSpec((8, 3 * d), lambda i: (0, 0)),
            _resident_weight_spec(w, layer),
        ],
        out_specs=[rows(3 * cw), rows(cw), rows(cw), rows(cw)],
        out_shape=[
            jax.ShapeDtypeStruct((nt, 3 * cw), BF16),
            jax.ShapeDtypeStruct((nt, cw), BF16),
            jax.ShapeDtypeStruct((nt, cw), F32),
            jax.ShapeDtypeStruct((nt, cw), BF16),
        ],
        compiler_params=_cparams(1),
        name="even_proj",
    )(xs, g, mod, w)


def _na_bias_table(rpb, rows, nc):
    pad = GRID_W - NA_WIN_COLS
    rp = jnp.pad(rpb.astype(F32), ((0, 0), (0, 0), (pad, pad)))
    toe = jnp.stack([rp[:, :, GRID_W - 1 - qc:2 * GRID_W - 1 - qc] for qc in range(GRID_W)], axis=2)
    qc = np.arange(GRID_W)[:, None]
    kc = np.arange(GRID_W)[None, :]
    cs = np.clip(qc - NA_WIN_COLS // 2, 0, GRID_W - NA_WIN_COLS)
    col_ok = (kc >= cs) & (kc < cs + NA_WIN_COLS)
    toe = jnp.where(col_ok, toe, NEG_BIG)
    masked = jnp.full((rpb.shape[0], GRID_W, GRID_W), NEG_BIG, F32)
    pats = []
    for r0 in (0, 2 * NA_ROWS, rows - NA_ROWS):
        ks = int(np.clip(r0 - NA_WIN_ROWS // 2, 0, rows - NA_SLAB))
        q_blocks = []
        for r in range(r0, r0 + NA_ROWS):
            rs = int(np.clip(r - NA_WIN_ROWS // 2, 0, rows - NA_WIN_ROWS))
            k_blocks = [toe[:, kr - r + NA_WIN_ROWS - 1] if rs <= kr < rs + NA_WIN_ROWS else masked
                        for kr in range(ks, ks + NA_SLAB)]
            q_blocks.append(jnp.concatenate(k_blocks, axis=-1))
        pat = jnp.concatenate(q_blocks, axis=1)
        pats.append(jnp.concatenate([pat, jnp.zeros(pat.shape[:2] + (nc,), F32)], axis=-1))
    return jnp.stack(pats)


def _na_kernel(q_ref, k_ref, v_ref, bias_ref, ga_ref, o_ref, *, n_lat, nc, rows):
    b = pl.program_id(1)
    nb_lat = rows // NA_ROWS
    scale = HEAD_DIM ** -0.5

    def attend(h, keys, values, bias):
        sl = slice(h * HEAD_DIM, (h + 1) * HEAD_DIM)
        s = _dot_nt(q_ref[:, sl], keys) * scale
        if bias is not None:
            s = s + bias
        p = jnp.exp(s - jnp.max(s, axis=-1, keepdims=True)).astype(BF16)
        o = _dot(p, jnp.concatenate([values, jnp.ones_like(values)], axis=1))
        gate = ga_ref[:, sl].astype(F32)
        o_ref[:, sl] = (o[:, :HEAD_DIM] / o[:, HEAD_DIM:] * gate).astype(BF16)

    @pl.when(b < nb_lat)
    def _():
        ks = jnp.clip(b * NA_ROWS - NA_WIN_ROWS // 2, 0, rows - NA_SLAB)
        start = pl.multiple_of(ks * GRID_W, GRID_W)
        for h in range(NA_HP):
            sl = slice(h * HEAD_DIM, (h + 1) * HEAD_DIM)
            keys = jnp.concatenate([k_ref[pl.ds(start, NA_SLAB * GRID_W), sl], k_ref[n_lat:n_lat + nc, sl]], axis=0)
            values = jnp.concatenate([v_ref[pl.ds(start, NA_SLAB * GRID_W), sl], v_ref[n_lat:n_lat + nc, sl]], axis=0)
            attend(h, keys, values, bias_ref[h])

    @pl.when(b >= nb_lat)
    def _():
        for h in range(NA_HP):
            sl = slice(h * HEAD_DIM, (h + 1) * HEAD_DIM)
            attend(h, k_ref[n_lat:n_lat + nc, sl], v_ref[n_lat:n_lat + nc, sl], None)


def _na_attention(qkv, ga, bias_tab, n_lat):
    nt = qkv.shape[0]
    nc = nt - n_lat
    rows = n_lat // GRID_W
    assert nc == NA_TQ and rows % NA_ROWS == 0 and rows >= NA_SLAB
    nb = nt // NA_TQ
    nb_lat = rows // NA_ROWS
    n_keys = NA_SLAB * GRID_W + nc
    hw = NA_HP * HEAD_DIM
    groups = NA_HEADS // NA_HP

    def pat(b):
        return jnp.where(b == 0, 0, jnp.where(b >= nb_lat - 1, 2, 1))

    kern = functools.partial(_na_kernel, n_lat=n_lat, nc=nc, rows=rows)
    return pl.pallas_call(
        kern,
        grid=(groups, nb),
        in_specs=[
            pl.BlockSpec((NA_TQ, hw), lambda h, b: (b, h)),
            pl.BlockSpec((nt, hw), lambda h, b: (0, groups + h)),
            pl.BlockSpec((nt, hw), lambda h, b: (0, 2 * groups + h)),
            pl.BlockSpec((None, NA_HP, NA_TQ, n_keys), lambda h, b: (pat(b), h, 0, 0)),
            pl.BlockSpec((NA_TQ, hw), lambda h, b: (b, h)),
        ],
        out_specs=pl.BlockSpec((NA_TQ, hw), lambda h, b: (b, h)),
        out_shape=jax.ShapeDtypeStruct((nt, NA_HEADS * HEAD_DIM), BF16),
        compiler_params=_cparams(2),
        name="na_attention",
    )(qkv, qkv, qkv, bias_tab, ga)


def _conv_kernel(um_ref, up_ref, un_ref, gb_ref, w_ref, b_ref, lg_ref, lb_ref, o_ref, ubuf, shifted, *, n_lat, nt):
    t = pl.program_id(0)
    start = t * CONV_T
    prev_ok = jnp.logical_and(start != 0, start != n_lat)
    next_ok = jnp.logical_and(start + CONV_T != n_lat, start + CONV_T != nt)
    ubuf[0:CONV_HALO, :] = jnp.where(prev_ok, up_ref[...], 0.0)
    ubuf[CONV_HALO:CONV_HALO + CONV_T, :] = um_ref[...]
    ubuf[CONV_HALO + CONV_T:, :] = jnp.where(next_ok, un_ref[...], 0.0)
    span = CONV_T + 2 * CONV_HALO - 8
    for b in range(1, 8):
        shifted[b - 1, :, :] = ubuf[b:b + span, :]
    off = CONV_HALO - CONV_KERNEL // 2
    for r0 in range(0, CONV_T, CONV_CH):
        acc = jnp.zeros((CONV_CH, um_ref.shape[-1]), F32) + b_ref[...]
        for j in range(CONV_KERNEL):
            a, b = divmod(off + j, 8)
            rows = slice(r0 + 8 * a, r0 + 8 * a + CONV_CH)
            tap = ubuf[rows, :] if b == 0 else shifted[b - 1, rows, :]
            acc = acc + w_ref[j:j + 1, :] * tap
        mu = jnp.mean(acc, axis=-1, keepdims=True)
        xc = acc - mu
        var = jnp.mean(xc * xc, axis=-1, keepdims=True)
        y = xc * lax.rsqrt(var + NORM_EPS) * lg_ref[...] + lb_ref[...]
        y = _silu(y) * gb_ref[r0:r0 + CONV_CH, :].astype(F32)
        o_ref[r0:r0 + CONV_CH, :] = y.astype(BF16)


def _conformer_conv(u, gb, dw_w, dw_b, ln_g, ln_b, n_lat):
    nt, cw = u.shape
    assert nt % CONV_T == 0 and n_lat % CONV_T == 0
    hb = CONV_T // CONV_HALO
    n_halo_blocks = nt // CONV_HALO
    w_pad = jnp.zeros((32, cw), F32).at[:CONV_KERNEL].set(dw_w)
    kern = functools.partial(_conv_kernel, n_lat=n_lat, nt=nt)
    vec = lambda: pl.BlockSpec((1, cw), lambda t: (0, 0))
    return pl.pallas_call(
        kern,
        grid=(nt // CONV_T,),
        in_specs=[
            pl.BlockSpec((CONV_T, cw), lambda t: (t, 0)),
            pl.BlockSpec((CONV_HALO, cw), lambda t: (jnp.maximum(t * hb - 1, 0), 0)),
            pl.BlockSpec((CONV_HALO, cw), lambda t: (jnp.minimum((t + 1) * hb, n_halo_blocks - 1), 0)),
            pl.BlockSpec((CONV_T, cw), lambda t: (t, 0)),
            pl.BlockSpec((32, cw), lambda t: (0, 0)),
            vec(), vec(), vec(),
        ],
        out_specs=pl.BlockSpec((CONV_T, cw), lambda t: (t, 0)),
        out_shape=jax.ShapeDtypeStruct((nt, cw), BF16),
        scratch_shapes=[pltpu.VMEM((CONV_T + 2 * CONV_HALO, cw), F32),
                        pltpu.VMEM((7, CONV_T + 2 * CONV_HALO - 8, cw), F32)],
        compiler_params=_cparams(1),
        name="conformer_conv",
    )(u, u, u, gb, w_pad, dw_b.reshape(1, cw), ln_g.reshape(1, cw), ln_b.reshape(1, cw))


def _out_proj_kernel(a_ref, b_ref, w_ref, x_ref, mod_ref, fg_ref, o_ref, *, n_lat, tm, final):
    i = pl.program_id(0)
    half = a_ref.shape[-1]
    y = _dot(a_ref[...], w_ref[0:half, :]) + _dot(b_ref[...], w_ref[half:, :])
    d = y.shape[-1]
    gt = _mod_rows(mod_ref, 2, _row_is_ctx(i, tm, n_lat), d)
    xn = x_ref[...] + gt * y
    if final:
        ms = jnp.mean(xn * xn, axis=-1, keepdims=True)
        xn = xn * lax.rsqrt(ms + NORM_EPS) * fg_ref[...]
    o_ref[...] = xn


def _out_proj(mix_a, mix_b, cols, w, xs, mod, fg, n_lat, final):
    nt, d = xs.shape
    tm = TM_OUT
    half = w.shape[0] // 2
    out_rows = n_lat if final else nt
    assert out_rows % tm == 0
    kern = functools.partial(_out_proj_kernel, n_lat=n_lat, tm=tm, final=final)
    return pl.pallas_call(
        kern,
        grid=(out_rows // tm,),
        in_specs=[
            pl.BlockSpec((tm, half), lambda i: (i, cols[0])),
            pl.BlockSpec((tm, half), lambda i: (i, cols[1])),
            pl.BlockSpec((2 * half, d), lambda i: (0, 0)),
            pl.BlockSpec((tm, d), lambda i: (i, 0)),
            pl.BlockSpec((8, 3 * d), lambda i: (0, 0)),
            pl.BlockSpec((1, d), lambda i: (0, 0)),
        ],
        out_specs=pl.BlockSpec((tm, d), lambda i: (i, 0)),
        out_shape=jax.ShapeDtypeStruct((out_rows, d), F32),
        compiler_params=_cparams(1),
        name="out_proj_final" if final else "out_proj",
    )(mix_a, mix_b, w, xs, mod, fg)


def _rope_tables(n_lat, nc):
    rows = n_lat // GRID_W
    n_freq = HEAD_DIM // 4
    inv = ROPE_THETA ** (-jnp.arange(n_freq, dtype=F32) / n_freq)
    row_ang = jnp.arange(rows, dtype=jnp.int32).astype(F32)[:, None] * inv
    col_ang = jnp.arange(GRID_W, dtype=jnp.int32).astype(F32)[:, None] * inv
    per_row = lambda a: jnp.repeat(a, GRID_W, axis=0)
    per_col = lambda a: jnp.tile(a, (rows, 1))
    cr, sr = per_row(jnp.cos(row_ang)), per_row(jnp.sin(row_ang))
    cc, sc = per_col(jnp.cos(col_ang)), per_col(jnp.sin(col_ang))
    cos_t = jnp.concatenate([cr, cr, cc, cc], axis=-1)
    sin_t = jnp.concatenate([-sr, sr, -sc, sc], axis=-1)
    cos_t = jnp.concatenate([cos_t, jnp.ones((nc, HEAD_DIM), F32)], axis=0)
    sin_t = jnp.concatenate([sin_t, jnp.zeros((nc, HEAD_DIM), F32)], axis=0)
    return cos_t, sin_t


def _norm_rope(x, gain, cos_t, sin_t, first_half):
    ms = jnp.mean(x * x, axis=-1, keepdims=True)
    xn = x * lax.rsqrt(ms + NORM_EPS) * gain
    partner = jnp.where(first_half, pltpu.roll(xn, 3 * HEAD_DIM // 4, 1), pltpu.roll(xn, HEAD_DIM // 4, 1))
    return xn * cos_t + partner * sin_t


def _odd_proj_kernel(x_ref, g_ref, mod_ref, w_ref, wvt_ref, cos_ref, sin_ref, qn_ref, kn_ref,
                     q_ref, k_ref, vt_ref, gs_ref, *, n_lat, tm):
    h = _prenorm(x_ref, g_ref, mod_ref, pl.program_id(0), tm, n_lat)
    tn = k_ref.shape[-1]
    cos_t = cos_ref[...]
    sin_t = sin_ref[...]
    lane = lax.broadcasted_iota(jnp.int32, (1, HEAD_DIM), 1)
    first_half = (lane % (HEAD_DIM // 2)) < (HEAD_DIM // 4)

    def proj(seg):
        return _dot(h, w_ref[:, seg * tn:(seg + 1) * tn])

    def rotated(seg, gain, out_ref, col0):
        acc = proj(seg)
        for hh in range(tn // HEAD_DIM):
            y = _norm_rope(acc[:, hh * HEAD_DIM:(hh + 1) * HEAD_DIM], gain, cos_t, sin_t, first_half)
            out_ref[:, col0 + hh * HEAD_DIM:col0 + (hh + 1) * HEAD_DIM] = y.astype(BF16)

    n_q = q_ref.shape[-1] // tn
    q_gain = qn_ref[...] * ((HEAD_DIM ** -0.5) * LOG2E)
    for seg in range(n_q):
        rotated(seg, q_gain, q_ref, seg * tn)
    rotated(n_q, kn_ref[...], k_ref, 0)
    v_t = _dot_nt(wvt_ref[...], h)
    vt_ref[...] = v_t.reshape(vt_ref.shape).astype(BF16)
    for seg in range(n_q):
        gs_ref[:, seg * tn:(seg + 1) * tn] = _silu(proj(n_q + 2 + seg)).astype(BF16)


def _odd_proj(xs, g, mod, w, w_vt, layer, cos_t, sin_t, q_norm, k_norm, n_lat):
    nt, d = xs.shape
    tm = TM_PROJ
    qw = GQA_HEADS * HEAD_DIM
    kw = GQA_KV_HEADS * HEAD_DIM
    assert w.shape[2] == 2 * qw + 2 * kw and nt % tm == 0 and n_lat % tm == 0
    kern = functools.partial(_odd_proj_kernel, n_lat=n_lat, tm=tm)
    rows = lambda width: pl.BlockSpec((tm, width), lambda i: (i, 0))
    vec = lambda width: pl.BlockSpec((1, width), lambda i: (0, 0))
    return pl.pallas_call(
        kern,
        grid=(nt // tm,),
        in_specs=[
            rows(d), vec(d),
            pl.BlockSpec((8, 3 * d), lambda i: (0, 0)),
            _resident_weight_spec(w, layer),
            pl.BlockSpec((kw, d), lambda i: (0, 0)),
            rows(HEAD_DIM), rows(HEAD_DIM), vec(HEAD_DIM), vec(HEAD_DIM),
        ],
        out_specs=[rows(qw), rows(kw),
                   pl.BlockSpec((GQA_KV_HEADS, None, HEAD_DIM, tm), lambda i: (0, i, 0, 0)),
                   rows(qw)],
        out_shape=[
            jax.ShapeDtypeStruct((nt, qw), BF16),
            jax.ShapeDtypeStruct((nt, kw), BF16),
            jax.ShapeDtypeStruct((GQA_KV_HEADS, nt // tm, HEAD_DIM, tm), BF16),
            jax.ShapeDtypeStruct((nt, qw), BF16),
        ],
        compiler_params=_cparams(1),
        name="odd_proj",
    )(xs, g, mod, w, w_vt, cos_t, sin_t,
      q_norm.reshape(1, HEAD_DIM), k_norm.reshape(1, HEAD_DIM))


GQA_SAFE_BOUND = 60.0


GQA_ONES_ROWS = 16


def _gqa_kernel(bound_ref, q_ref, k_ref, vt_ref, g_ref, o_ref, *, n_lat, nc):
    b = pl.program_id(1)
    tq = q_ref.shape[0]
    vb = vt_ref.shape[-1]
    bound = bound_ref[0]
    is_lat = b * tq < n_lat
    q = jnp.concatenate([q_ref[:, h * HEAD_DIM:(h + 1) * HEAD_DIM] for h in range(GQA_GROUP)], axis=0)

    def vt_chunk(first_block, n_blocks):
        return jnp.concatenate([vt_ref[first_block + j] for j in range(n_blocks)], axis=1)

    def write_out(o_t):
        o = o_t.T
        for h in range(GQA_GROUP):
            sl = slice(h * HEAD_DIM, (h + 1) * HEAD_DIM)
            o_ref[:, sl] = (o[h * tq:(h + 1) * tq, :] * g_ref[:, sl].astype(F32)).astype(BF16)

    def fixed_shift_chunk(row0, n_rows):
        p = jnp.exp2(_dot_nt(k_ref[row0:row0 + n_rows, :], q) - bound).astype(BF16)
        vt = vt_chunk(row0 // vb, n_rows // vb)
        vt = jnp.concatenate([vt, jnp.ones((GQA_ONES_ROWS, n_rows), BF16)], axis=0)
        return _dot(vt, p)

    def finish(acc):
        write_out(acc[:HEAD_DIM, :] / acc[HEAD_DIM:HEAD_DIM + 1, :])

    small = bound <= GQA_SAFE_BOUND

    @pl.when(jnp.logical_and(small, is_lat))
    def _():
        acc = fixed_shift_chunk(n_lat, nc)
        for c in range(n_lat // GQA_TK):
            acc = acc + fixed_shift_chunk(c * GQA_TK, GQA_TK)
        finish(acc)

    @pl.when(jnp.logical_and(small, jnp.logical_not(is_lat)))
    def _():
        finish(fixed_shift_chunk(n_lat, nc))

    @pl.when(bound > GQA_SAFE_BOUND)
    def _():
        cols = GQA_GROUP * tq
        per = GQA_TK_ONLINE // vb

        def step(carry, k_chunk, vt):
            m, l, acc = carry
            s = _dot_nt(k_chunk, q)
            m_new = jnp.maximum(m, jnp.max(s, axis=0, keepdims=True))
            alpha = jnp.exp2(m - m_new)
            p = jnp.exp2(s - m_new)
            l = alpha * l + jnp.sum(p, axis=0, keepdims=True)
            acc = alpha * acc + _dot(vt, p.astype(BF16))
            return m_new, l, acc

        init = (jnp.full((1, cols), NEG_BIG, F32), jnp.zeros((1, cols), F32), jnp.zeros((HEAD_DIM, cols), F32))
        carry = step(init, k_ref[n_lat:n_lat + nc, :], vt_chunk(n_lat // vb, nc // vb))

        def body(c, carry):
            start = pl.multiple_of(c * GQA_TK_ONLINE, GQA_TK_ONLINE)
            vt = jnp.concatenate([vt_ref[c * per + j] for j in range(per)], axis=1)
            return step(carry, k_ref[pl.ds(start, GQA_TK_ONLINE), :], vt)

        m, l, acc = lax.fori_loop(0, jnp.where(is_lat, n_lat // GQA_TK_ONLINE, 0), body, carry)
        write_out(acc / l)


def _gqa_attention(q, k, vt, gs, q_norm, k_norm, n_lat):
    nt = q.shape[0]
    nc = nt - n_lat
    tq = GQA_TQ
    gw = GQA_GROUP * HEAD_DIM
    n_vb, vb = vt.shape[1], vt.shape[3]
    assert nt % tq == 0 and n_lat % tq == 0 and n_lat % GQA_TK == 0 and n_lat % GQA_TK_ONLINE == 0
    assert n_vb * vb == nt and n_lat % vb == 0 and nc % vb == 0 and GQA_TK_ONLINE % vb == 0
    bound = (1.01 * HEAD_DIM * (HEAD_DIM ** -0.5) * LOG2E) * jnp.max(jnp.abs(q_norm)) * jnp.max(jnp.abs(k_norm))
    kern = functools.partial(_gqa_kernel, n_lat=n_lat, nc=nc)
    return pl.pallas_call(
        kern,
        grid=(GQA_KV_HEADS, nt // tq),
        in_specs=[
            pl.BlockSpec(memory_space=pltpu.SMEM),
            pl.BlockSpec((tq, gw), lambda h, b: (b, h)),
            pl.BlockSpec((nt, HEAD_DIM), lambda h, b: (0, h)),
            pl.BlockSpec((None, n_vb, HEAD_DIM, vb), lambda h, b: (h, 0, 0, 0)),
            pl.BlockSpec((tq, gw), lambda h, b: (b, h)),
        ],
        out_specs=pl.BlockSpec((tq, gw), lambda h, b: (b, h)),
        out_shape=jax.ShapeDtypeStruct((nt, GQA_HEADS * HEAD_DIM), BF16),
        compiler_params=_cparams(2),
        name="gqa_attention",
    )(bound.reshape(1).astype(F32), q, k, vt, gs)


def kernel(x, c, ctx, c_ctx, norm_g, w_mod, b_mod, e_w_in, e_rpb, e_dw_w, e_dw_b, e_ln_g, e_ln_b, e_w_out,
           o_w_in, o_q_norm, o_k_norm, o_w_out, final_norm_g):
    batch, n_lat, d = x.shape
    nc = ctx.shape[1]
    depth = norm_g.shape[0]
    assert batch == 1
    rows = n_lat // GRID_W

    xs = jnp.concatenate([x[0], ctx[0]], axis=0)
    cvec = jnp.zeros((8, d), F32).at[0].set(c[0]).at[1].set(c_ctx)
    mods = _modulation(cvec, w_mod, b_mod)
    cos_t, sin_t = _rope_tables(n_lat, nc)
    fg = final_norm_g.reshape(1, d)
    e_w_in_b = e_w_in.astype(BF16)
    o_w_in_b = o_w_in.astype(BF16)

    for l in range(depth):
        i = l // 2
        g = norm_g[l].reshape(1, d)
        mod = mods[l]
        final = l == depth - 1
        if l % 2 == 0:
            qkv, ga, u, gb = _even_proj(xs, g, mod, e_w_in_b, i, n_lat)
            mix_a = _na_attention(qkv, ga, _na_bias_table(e_rpb[i], rows, nc), n_lat)
            mix_b = _conformer_conv(u, gb, e_dw_w[i], e_dw_b[i], e_ln_g[i], e_ln_b[i], n_lat)
            xs = _out_proj(mix_a, mix_b, (0, 0), e_w_out[i].astype(BF16), xs, mod, fg, n_lat, final)
        else:
            v0 = (GQA_HEADS + GQA_KV_HEADS) * HEAD_DIM
            w_vt = o_w_in[i, :, v0:v0 + GQA_KV_HEADS * HEAD_DIM].T.astype(BF16)
            q, k, vt, gs = _odd_proj(xs, g, mod, o_w_in_b, w_vt, i, cos_t, sin_t,
                                     o_q_norm[i], o_k_norm[i], n_lat)
            mix = _gqa_attention(q, k, vt, gs, o_q_norm[i], o_k_norm[i], n_lat)
            xs = _out_proj(mix, mix, (0, 1), o_w_out[i].astype(BF16), xs, mod, fg, n_lat, final)
    return xs[None]
```

```python
import functools

import numpy as np
import jax
import jax.numpy as jnp
from jax import lax
from jax.experimental import pallas as pl
from jax.experimental.pallas import tpu as pltpu

F32 = jnp.float32
BF16 = jnp.bfloat16

GRID_W = 64
HEAD_DIM = 128
NORM_EPS = 1e-6
NA_HEADS = 8
NA_WIN_ROWS = 8
NA_WIN_COLS = 16
CONV_KERNEL = 31
GQA_HEADS = 16
GQA_KV_HEADS = 4
GQA_GROUP = GQA_HEADS // GQA_KV_HEADS
ROPE_THETA = 10000.0
LOG2E = 1.4426950408889634

LANES = 128
V7X_VMEM_BYTES = 64 * 1024 * 1024
VMEM_LIMIT = V7X_VMEM_BYTES - 8 * 1024 * 1024

TM_PROJ = 256
TM_OUT = 256
NA_ROWS = 4
NA_SLAB = NA_WIN_ROWS + NA_ROWS
NA_TQ = NA_ROWS * GRID_W
NA_HP = 4
CONV_T = 128
CONV_HALO = 16
CONV_CH = 32
GQA_TQ = 256
GQA_TK = 2048
GQA_TK_ONLINE = 512
GQA_SAFE_BOUND = 60.0
GQA_ONES_ROWS = 16
NEG_BIG = -1e30


def _cparams(n_axes):
    return pltpu.CompilerParams(dimension_semantics=("arbitrary",) * n_axes, vmem_limit_bytes=VMEM_LIMIT)


def _sigmoid(x):
    return 1.0 / (1.0 + jnp.exp(-x))


def _silu(x):
    return x * _sigmoid(x)


def _dot(a, b):
    return jnp.dot(a, b, preferred_element_type=F32)


def _dot_nt(a, b):
    return lax.dot_general(a, b, (((1,), (1,)), ((), ())), preferred_element_type=F32)


def _row_is_ctx(tile_idx, tm, n_lat):
    rows = tile_idx * tm + lax.broadcasted_iota(jnp.int32, (tm, 1), 0)
    return rows >= n_lat


def _mod_rows(mod_ref, part, is_ctx, d):
    lat = mod_ref[0:1, part * d:(part + 1) * d]
    ctx = mod_ref[1:2, part * d:(part + 1) * d]
    return jnp.where(is_ctx, ctx, lat)


def _stream_specs(lat_src, ctx_src, ctx_row0, tm, n_lat):
    d = lat_src.shape[1]
    nlt = n_lat // tm
    n_ctx_tiles = (ctx_src.shape[0] - ctx_row0) // tm
    assert n_lat % tm == 0 and ctx_row0 % tm == 0 and n_ctx_tiles >= 1
    return [pl.BlockSpec((tm, d), lambda i: (jnp.minimum(i, nlt - 1), 0)),
            pl.BlockSpec((tm, d), lambda i: (ctx_row0 // tm + jnp.clip(i - nlt, 0, n_ctx_tiles - 1), 0))]


def _stream_tile(x_ref, c_ref, tile_idx, tm, n_lat):
    return jnp.where(tile_idx * tm >= n_lat, c_ref[...], x_ref[...])


def _prenorm(x_ref, c_ref, g_ref, mod_ref, tile_idx, tm, n_lat):
    d = x_ref.shape[-1]
    x = _stream_tile(x_ref, c_ref, tile_idx, tm, n_lat)
    ms = jnp.mean(x * x, axis=-1, keepdims=True)
    row = (tile_idx * tm >= n_lat).astype(jnp.int32)
    mod = mod_ref[pl.ds(row, 1), :]
    a = g_ref[...] * (1.0 + mod[:, d:2 * d])
    return (x * lax.rsqrt(ms + NORM_EPS) * a + mod[:, 0:d]).astype(BF16)


def _resident_weight_spec(w, layer):
    return pl.BlockSpec((None,) + w.shape[1:], lambda i: (layer, 0, 0), pipeline_mode=pl.Buffered(1))


def _mod_kernel(c_ref, w_ref, b_ref, o_ref):
    s = _silu(c_ref[...]).astype(BF16)
    o_ref[...] = _dot(s, w_ref[...].astype(BF16)) + b_ref[...]


def _modulation(cvec, w_mod, b_mod):
    depth, d, d3 = w_mod.shape
    tn = 768
    return pl.pallas_call(
        _mod_kernel,
        grid=(depth, d3 // tn),
        in_specs=[
            pl.BlockSpec((8, d), lambda l, j: (0, 0)),
            pl.BlockSpec((None, d, tn), lambda l, j: (l, 0, j)),
            pl.BlockSpec((None, 1, tn), lambda l, j: (l, 0, j)),
        ],
        out_specs=pl.BlockSpec((None, 8, tn), lambda l, j: (l, 0, j)),
        out_shape=jax.ShapeDtypeStruct((depth, 8, d3), F32),
        compiler_params=_cparams(2),
        name="modulation",
    )(cvec, w_mod, b_mod.reshape(depth, 1, d3))


def _even_proj_kernel(x_ref, c_ref, g_ref, mod_ref, w_ref, qkv_ref, ga_ref, u_ref, gb_ref, *, n_lat, tm):
    h = _prenorm(x_ref, c_ref, g_ref, mod_ref, pl.program_id(0), tm, n_lat)
    cw = ga_ref.shape[-1]

    def proj(seg):
        return _dot(h, w_ref[:, seg * cw:(seg + 1) * cw])

    for seg in range(3):
        qkv_ref[:, seg * cw:(seg + 1) * cw] = proj(seg).astype(BF16)
    ga_ref[...] = _silu(proj(3)).astype(BF16)
    u_ref[...] = proj(4) * _sigmoid(proj(5))
    gb_ref[...] = _silu(proj(6)).astype(BF16)


def _even_proj(stream, nt, g, mod, w, layer, n_lat):
    d = stream[0].shape[1]
    tm = TM_PROJ
    cw = w.shape[2] // 7
    assert nt % tm == 0 and n_lat % tm == 0
    kern = functools.partial(_even_proj_kernel, n_lat=n_lat, tm=tm)
    rows = lambda width: pl.BlockSpec((tm, width), lambda i: (i, 0))
    return pl.pallas_call(
        kern,
        grid=(nt // tm,),
        in_specs=_stream_specs(*stream, tm, n_lat) + [
            pl.BlockSpec((1, d), lambda i: (0, 0)),
            pl.BlockSpec((8, 3 * d), lambda i: (0, 0)),
            _resident_weight_spec(w, layer),
        ],
        out_specs=[rows(3 * cw), rows(cw), rows(cw), rows(cw)],
        out_shape=[
            jax.ShapeDtypeStruct((nt, 3 * cw), BF16),
            jax.ShapeDtypeStruct((nt, cw), BF16),
            jax.ShapeDtypeStruct((nt, cw), F32),
            jax.ShapeDtypeStruct((nt, cw), BF16),
        ],
        compiler_params=_cparams(1),
        name="even_proj",
    )(stream[0], stream[1], g, mod, w)


def _na_bias_kernel(rp_ref, o_ref, *, rows, nc):
    pattern = pl.program_id(0)
    lane = lax.broadcasted_iota(jnp.int32, (GRID_W, 2 * GRID_W), 1)
    qc = lax.broadcasted_iota(jnp.int32, (GRID_W, 2 * GRID_W), 0)
    kc = lane % GRID_W
    cs = jnp.clip(qc - NA_WIN_COLS // 2, 0, GRID_W - NA_WIN_COLS)
    col_ok = jnp.logical_and(kc >= cs, kc < cs + NA_WIN_COLS)
    left_half = lane < GRID_W
    masked = jnp.full((GRID_W, 2 * GRID_W), NEG_BIG, F32)

    def toeplitz(dr, right):
        x = jnp.broadcast_to(rp_ref[dr:dr + 1, :], (GRID_W, 2 * GRID_W))
        return pltpu.roll(x, 1 if right else GRID_W + 1, 1, stride=1, stride_axis=0)

    for pat_id, r0 in enumerate((0, 2 * NA_ROWS, rows - NA_ROWS)):
        @pl.when(pattern == pat_id)
        def _(r0=r0):
            ks = int(np.clip(r0 - NA_WIN_ROWS // 2, 0, rows - NA_SLAB))
            for r_local in range(NA_ROWS):
                r = r0 + r_local
                rs = int(np.clip(r - NA_WIN_ROWS // 2, 0, rows - NA_WIN_ROWS))
                for jp in range(NA_SLAB // 2):
                    halves = [toeplitz(kr - r + NA_WIN_ROWS - 1, side == 1) if rs <= kr < rs + NA_WIN_ROWS else None
                              for side, kr in enumerate((ks + 2 * jp, ks + 2 * jp + 1))]
                    if halves[0] is None and halves[1] is None:
                        tile = masked
                    else:
                        left = masked if halves[0] is None else halves[0]
                        right = masked if halves[1] is None else halves[1]
                        tile = jnp.where(col_ok, jnp.where(left_half, left, right), NEG_BIG)
                    o_ref[r_local * GRID_W:(r_local + 1) * GRID_W, jp * 2 * GRID_W:(jp + 1) * 2 * GRID_W] = tile
    o_ref[:, NA_SLAB * GRID_W:] = jnp.zeros((NA_TQ, nc), F32)


def _na_bias_table(rpb, rows, nc):
    heads = rpb.shape[0]
    assert NA_SLAB % 2 == 0 and 2 * GRID_W == LANES and rpb.shape[1] == 2 * NA_WIN_ROWS - 1
    pad = GRID_W - NA_WIN_COLS
    rp = jnp.pad(rpb.astype(F32), ((0, 0), (0, 1), (pad, LANES - pad - rpb.shape[2])))
    n_keys = NA_SLAB * GRID_W + nc
    kern = functools.partial(_na_bias_kernel, rows=rows, nc=nc)
    return pl.pallas_call(
        kern,
        grid=(3, heads),
        in_specs=[pl.BlockSpec((None, 2 * NA_WIN_ROWS, LANES), lambda p, h: (h, 0, 0))],
        out_specs=pl.BlockSpec((None, None, NA_TQ, n_keys), lambda p, h: (p, h, 0, 0)),
        out_shape=jax.ShapeDtypeStruct((3, heads, NA_TQ, n_keys), F32),
        compiler_params=_cparams(2),
        name="na_bias",
    )(rp)


def _na_kernel(q_ref, k_ref, v_ref, bias_ref, ga_ref, o_ref, *, n_lat, nc, rows):
    b = pl.program_id(1)
    nb_lat = rows // NA_ROWS
    scale = HEAD_DIM ** -0.5

    def attend(h, keys, values, bias):
        sl = slice(h * HEAD_DIM, (h + 1) * HEAD_DIM)
        s = _dot_nt(q_ref[:, sl], keys) * scale
        if bias is not None:
            s = s + bias
        p = jnp.exp(s - jnp.max(s, axis=-1, keepdims=True)).astype(BF16)
        o = _dot(p, jnp.concatenate([values, jnp.ones_like(values)], axis=1))
        gate = ga_ref[:, sl].astype(F32)
        o_ref[:, sl] = (o[:, :HEAD_DIM] / o[:, HEAD_DIM:] * gate).astype(BF16)

    @pl.when(b < nb_lat)
    def _():
        ks = jnp.clip(b * NA_ROWS - NA_WIN_ROWS // 2, 0, rows - NA_SLAB)
        start = pl.multiple_of(ks * GRID_W, GRID_W)
        for h in range(NA_HP):
            sl = slice(h * HEAD_DIM, (h + 1) * HEAD_DIM)
            keys = jnp.concatenate([k_ref[pl.ds(start, NA_SLAB * GRID_W), sl], k_ref[n_lat:n_lat + nc, sl]], axis=0)
            values = jnp.concatenate([v_ref[pl.ds(start, NA_SLAB * GRID_W), sl], v_ref[n_lat:n_lat + nc, sl]], axis=0)
            attend(h, keys, values, bias_ref[h])

    @pl.when(b >= nb_lat)
    def _():
        for h in range(NA_HP):
            sl = slice(h * HEAD_DIM, (h + 1) * HEAD_DIM)
            attend(h, k_ref[n_lat:n_lat + nc, sl], v_ref[n_lat:n_lat + nc, sl], None)


def _na_attention(qkv, ga, bias_tab, n_lat):
    nt = qkv.shape[0]
    nc = nt - n_lat
    rows = n_lat // GRID_W
    assert nc == NA_TQ and rows % NA_ROWS == 0 and rows >= NA_SLAB
    nb = nt // NA_TQ
    nb_lat = rows // NA_ROWS
    n_keys = NA_SLAB * GRID_W + nc
    hw = NA_HP * HEAD_DIM
    groups = NA_HEADS // NA_HP

    def pat(b):
        return jnp.where(b == 0, 0, jnp.where(b >= nb_lat - 1, 2, 1))

    kern = functools.partial(_na_kernel, n_lat=n_lat, nc=nc, rows=rows)
    return pl.pallas_call(
        kern,
        grid=(groups, nb),
        in_specs=[
            pl.BlockSpec((NA_TQ, hw), lambda h, b: (b, h)),
            pl.BlockSpec((nt, hw), lambda h, b: (0, groups + h)),
            pl.BlockSpec((nt, hw), lambda h, b: (0, 2 * groups + h)),
            pl.BlockSpec((None, NA_HP, NA_TQ, n_keys), lambda h, b: (pat(b), h, 0, 0)),
            pl.BlockSpec((NA_TQ, hw), lambda h, b: (b, h)),
        ],
        out_specs=pl.BlockSpec((NA_TQ, hw), lambda h, b: (b, h)),
        out_shape=jax.ShapeDtypeStruct((nt, NA_HEADS * HEAD_DIM), BF16),
        compiler_params=_cparams(2),
        name="na_attention",
    )(qkv, qkv, qkv, bias_tab, ga)


def _conv_kernel(um_ref, up_ref, un_ref, gb_ref, w_ref, b_ref, lg_ref, lb_ref, o_ref, ubuf, shifted, *, n_lat, nt):
    t = pl.program_id(0)
    start = t * CONV_T
    prev_ok = jnp.logical_and(start != 0, start != n_lat)
    next_ok = jnp.logical_and(start + CONV_T != n_lat, start + CONV_T != nt)
    ubuf[0:CONV_HALO, :] = jnp.where(prev_ok, up_ref[...], 0.0)
    ubuf[CONV_HALO:CONV_HALO + CONV_T, :] = um_ref[...]
    ubuf[CONV_HALO + CONV_T:, :] = jnp.where(next_ok, un_ref[...], 0.0)
    span = CONV_T + 2 * CONV_HALO - 8
    for b in range(1, 8):
        shifted[b - 1, :, :] = ubuf[b:b + span, :]
    off = CONV_HALO - CONV_KERNEL // 2
    for r0 in range(0, CONV_T, CONV_CH):
        acc = jnp.zeros((CONV_CH, um_ref.shape[-1]), F32) + b_ref[...]
        for j in range(CONV_KERNEL):
            a, b = divmod(off + j, 8)
            rows = slice(r0 + 8 * a, r0 + 8 * a + CONV_CH)
            tap = ubuf[rows, :] if b == 0 else shifted[b - 1, rows, :]
            acc = acc + w_ref[j:j + 1, :] * tap
        mu = jnp.mean(acc, axis=-1, keepdims=True)
        xc = acc - mu
        var = jnp.mean(xc * xc, axis=-1, keepdims=True)
        y = xc * lax.rsqrt(var + NORM_EPS) * lg_ref[...] + lb_ref[...]
        y = _silu(y) * gb_ref[r0:r0 + CONV_CH, :].astype(F32)
        o_ref[r0:r0 + CONV_CH, :] = y.astype(BF16)


def _conformer_conv(u, gb, dw_w, dw_b, ln_g, ln_b, n_lat):
    nt, cw = u.shape
    assert nt % CONV_T == 0 and n_lat % CONV_T == 0
    hb = CONV_T // CONV_HALO
    n_halo_blocks = nt // CONV_HALO
    w_pad = jnp.zeros((32, cw), F32).at[:CONV_KERNEL].set(dw_w)
    kern = functools.partial(_conv_kernel, n_lat=n_lat, nt=nt)
    vec = lambda: pl.BlockSpec((1, cw), lambda t: (0, 0))
    return pl.pallas_call(
        kern,
        grid=(nt // CONV_T,),
        in_specs=[
            pl.BlockSpec((CONV_T, cw), lambda t: (t, 0)),
            pl.BlockSpec((CONV_HALO, cw), lambda t: (jnp.maximum(t * hb - 1, 0), 0)),
            pl.BlockSpec((CONV_HALO, cw), lambda t: (jnp.minimum((t + 1) * hb, n_halo_blocks - 1), 0)),
            pl.BlockSpec((CONV_T, cw), lambda t: (t, 0)),
            pl.BlockSpec((32, cw), lambda t: (0, 0)),
            vec(), vec(), vec(),
        ],
        out_specs=pl.BlockSpec((CONV_T, cw), lambda t: (t, 0)),
        out_shape=jax.ShapeDtypeStruct((nt, cw), BF16),
        scratch_shapes=[pltpu.VMEM((CONV_T + 2 * CONV_HALO, cw), F32),
                        pltpu.VMEM((7, CONV_T + 2 * CONV_HALO - 8, cw), F32)],
        compiler_params=_cparams(1),
        name="conformer_conv",
    )(u, u, u, gb, w_pad, dw_b.reshape(1, cw), ln_g.reshape(1, cw), ln_b.reshape(1, cw))


def _out_proj_kernel(a_ref, b_ref, w_ref, x_ref, c_ref, mod_ref, fg_ref, o_ref, *, n_lat, tm, final):
    i = pl.program_id(0)
    half = a_ref.shape[-1]
    y = _dot(a_ref[...], w_ref[0:half, :]) + _dot(b_ref[...], w_ref[half:, :])
    d = y.shape[-1]
    gt = _mod_rows(mod_ref, 2, _row_is_ctx(i, tm, n_lat), d)
    xn = _stream_tile(x_ref, c_ref, i, tm, n_lat) + gt * y
    if final:
        ms = jnp.mean(xn * xn, axis=-1, keepdims=True)
        xn = xn * lax.rsqrt(ms + NORM_EPS) * fg_ref[...]
    o_ref[...] = xn


def _out_proj(mix_a, mix_b, cols, w, stream, nt, mod, fg, n_lat, final):
    d = stream[0].shape[1]
    tm = TM_OUT
    half = w.shape[0] // 2
    out_rows = n_lat if final else nt
    assert out_rows % tm == 0
    kern = functools.partial(_out_proj_kernel, n_lat=n_lat, tm=tm, final=final)
    return pl.pallas_call(
        kern,
        grid=(out_rows // tm,),
        in_specs=[
            pl.BlockSpec((tm, half), lambda i: (i, cols[0])),
            pl.BlockSpec((tm, half), lambda i: (i, cols[1])),
            pl.BlockSpec((2 * half, d), lambda i: (0, 0)),
        ] + _stream_specs(*stream, tm, n_lat) + [
            pl.BlockSpec((8, 3 * d), lambda i: (0, 0)),
            pl.BlockSpec((1, d), lambda i: (0, 0)),
        ],
        out_specs=pl.BlockSpec((tm, d), lambda i: (i, 0)),
        out_shape=jax.ShapeDtypeStruct((out_rows, d), F32),
        compiler_params=_cparams(1),
        name="out_proj_final" if final else "out_proj",
    )(mix_a, mix_b, w, stream[0], stream[1], mod, fg)


def _rope_tables(n_lat, nc):
    rows = n_lat // GRID_W
    n_freq = HEAD_DIM // 4
    inv = ROPE_THETA ** (-jnp.arange(n_freq, dtype=F32) / n_freq)
    row_ang = jnp.arange(rows, dtype=jnp.int32).astype(F32)[:, None] * inv
    col_ang = jnp.arange(GRID_W, dtype=jnp.int32).astype(F32)[:, None] * inv
    per_row = lambda a: jnp.repeat(a, GRID_W, axis=0)
    per_col = lambda a: jnp.tile(a, (rows, 1))
    cr, sr = per_row(jnp.cos(row_ang)), per_row(jnp.sin(row_ang))
    cc, sc = per_col(jnp.cos(col_ang)), per_col(jnp.sin(col_ang))
    cos_t = jnp.concatenate([cr, cr, cc, cc], axis=-1)
    sin_t = jnp.concatenate([-sr, sr, -sc, sc], axis=-1)
    cos_t = jnp.concatenate([cos_t, jnp.ones((nc, HEAD_DIM), F32)], axis=0)
    sin_t = jnp.concatenate([sin_t, jnp.zeros((nc, HEAD_DIM), F32)], axis=0)
    return cos_t, sin_t


def _norm_rope(x, gain, cos_t, sin_t, first_half):
    ms = jnp.mean(x * x, axis=-1, keepdims=True)
    xn = x * lax.rsqrt(ms + NORM_EPS) * gain
    partner = jnp.where(first_half, pltpu.roll(xn, 3 * HEAD_DIM // 4, 1), pltpu.roll(xn, HEAD_DIM // 4, 1))
    return xn * cos_t + partner * sin_t


def _odd_proj_kernel(x_ref, c_ref, g_ref, mod_ref, w_ref, wvt_ref, cos_ref, sin_ref, qn_ref, kn_ref,
                     q_ref, k_ref, vt_ref, gs_ref, *, n_lat, tm):
    h = _prenorm(x_ref, c_ref, g_ref, mod_ref, pl.program_id(0), tm, n_lat)
    tn = k_ref.shape[-1]
    cos_t = cos_ref[...]
    sin_t = sin_ref[...]
    lane = lax.broadcasted_iota(jnp.int32, (1, HEAD_DIM), 1)
    first_half = (lane % (HEAD_DIM // 2)) < (HEAD_DIM // 4)

    def proj(seg):
        return _dot(h, w_ref[:, seg * tn:(seg + 1) * tn])

    def rotated(seg, gain, out_ref, col0):
        acc = proj(seg)
        for hh in range(tn // HEAD_DIM):
            y = _norm_rope(acc[:, hh * HEAD_DIM:(hh + 1) * HEAD_DIM], gain, cos_t, sin_t, first_half)
            out_ref[:, col0 + hh * HEAD_DIM:col0 + (hh + 1) * HEAD_DIM] = y.astype(BF16)

    n_q = q_ref.shape[-1] // tn
    q_gain = qn_ref[...] * ((HEAD_DIM ** -0.5) * LOG2E)
    for seg in range(n_q):
        rotated(seg, q_gain, q_ref, seg * tn)
    rotated(n_q, kn_ref[...], k_ref, 0)
    v_t = _dot_nt(wvt_ref[...], h)
    vt_ref[...] = v_t.reshape(vt_ref.shape).astype(BF16)
    for seg in range(n_q):
        gs_ref[:, seg * tn:(seg + 1) * tn] = _silu(proj(n_q + 2 + seg)).astype(BF16)


def _odd_proj(stream, nt, g, mod, w, w_vt, layer, cos_t, sin_t, q_norm, k_norm, n_lat):
    d = stream[0].shape[1]
    tm = TM_PROJ
    qw = GQA_HEADS * HEAD_DIM
    kw = GQA_KV_HEADS * HEAD_DIM
    assert w.shape[2] == 2 * qw + 2 * kw and nt % tm == 0 and n_lat % tm == 0
    kern = functools.partial(_odd_proj_kernel, n_lat=n_lat, tm=tm)
    rows = lambda width: pl.BlockSpec((tm, width), lambda i: (i, 0))
    vec = lambda width: pl.BlockSpec((1, width), lambda i: (0, 0))
    return pl.pallas_call(
        kern,
        grid=(nt // tm,),
        in_specs=_stream_specs(*stream, tm, n_lat) + [
            vec(d),
            pl.BlockSpec((8, 3 * d), lambda i: (0, 0)),
            _resident_weight_spec(w, layer),
            pl.BlockSpec((kw, d), lambda i: (0, 0)),
            rows(HEAD_DIM), rows(HEAD_DIM), vec(HEAD_DIM), vec(HEAD_DIM),
        ],
        out_specs=[rows(qw), rows(kw),
                   pl.BlockSpec((GQA_KV_HEADS, None, HEAD_DIM, tm), lambda i: (0, i, 0, 0)),
                   rows(qw)],
        out_shape=[
            jax.ShapeDtypeStruct((nt, qw), BF16),
            jax.ShapeDtypeStruct((nt, kw), BF16),
            jax.ShapeDtypeStruct((GQA_KV_HEADS, nt // tm, HEAD_DIM, tm), BF16),
            jax.ShapeDtypeStruct((nt, qw), BF16),
        ],
        compiler_params=_cparams(1),
        name="odd_proj",
    )(stream[0], stream[1], g, mod, w, w_vt, cos_t, sin_t,
      q_norm.reshape(1, HEAD_DIM), k_norm.reshape(1, HEAD_DIM))


def _gqa_kernel(bound_ref, q_ref, k_ref, vt_ref, g_ref, o_ref, *, n_lat, nc):
    b = pl.program_id(1)
    tq = q_ref.shape[0]
    vb = vt_ref.shape[-1]
    bound = bound_ref[0]
    is_lat = b * tq < n_lat
    q = jnp.concatenate([q_ref[:, h * HEAD_DIM:(h + 1) * HEAD_DIM] for h in range(GQA_GROUP)], axis=0)

    def vt_chunk(first_block, n_blocks):
        return jnp.concatenate([vt_ref[first_block + j] for j in range(n_blocks)], axis=1)

    def write_out(o_t):
        o = o_t.T
        for h in range(GQA_GROUP):
            sl = slice(h * HEAD_DIM, (h + 1) * HEAD_DIM)
            o_ref[:, sl] = (o[h * tq:(h + 1) * tq, :] * g_ref[:, sl].astype(F32)).astype(BF16)

    def fixed_shift_chunk(row0, n_rows):
        p = jnp.exp2(_dot_nt(k_ref[row0:row0 + n_rows, :], q) - bound).astype(BF16)
        vt = vt_chunk(row0 // vb, n_rows // vb)
        vt = jnp.concatenate([vt, jnp.ones((GQA_ONES_ROWS, n_rows), BF16)], axis=0)
        return _dot(vt, p)

    def finish(acc):
        write_out(acc[:HEAD_DIM, :] / acc[HEAD_DIM:HEAD_DIM + 1, :])

    small = bound <= GQA_SAFE_BOUND

    @pl.when(jnp.logical_and(small, is_lat))
    def _():
        acc = fixed_shift_chunk(n_lat, nc)
        for c in range(n_lat // GQA_TK):
            acc = acc + fixed_shift_chunk(c * GQA_TK, GQA_TK)
        finish(acc)

    @pl.when(jnp.logical_and(small, jnp.logical_not(is_lat)))
    def _():
        finish(fixed_shift_chunk(n_lat, nc))

    @pl.when(bound > GQA_SAFE_BOUND)
    def _():
        cols = GQA_GROUP * tq
        per = GQA_TK_ONLINE // vb

        def step(carry, k_chunk, vt):
            m, l, acc = carry
            s = _dot_nt(k_chunk, q)
            m_new = jnp.maximum(m, jnp.max(s, axis=0, keepdims=True))
            alpha = jnp.exp2(m - m_new)
            p = jnp.exp2(s - m_new)
            l = alpha * l + jnp.sum(p, axis=0, keepdims=True)
            acc = alpha * acc + _dot(vt, p.astype(BF16))
            return m_new, l, acc

        init = (jnp.full((1, cols), NEG_BIG, F32), jnp.zeros((1, cols), F32), jnp.zeros((HEAD_DIM, cols), F32))
        carry = step(init, k_ref[n_lat:n_lat + nc, :], vt_chunk(n_lat // vb, nc // vb))

        def body(c, carry):
            start = pl.multiple_of(c * GQA_TK_ONLINE, GQA_TK_ONLINE)
            vt = jnp.concatenate([vt_ref[c * per + j] for j in range(per)], axis=1)
            return step(carry, k_ref[pl.ds(start, GQA_TK_ONLINE), :], vt)

        m, l, acc = lax.fori_loop(0, jnp.where(is_lat, n_lat // GQA_TK_ONLINE, 0), body, carry)
        write_out(acc / l)


def _gqa_attention(q, k, vt, gs, q_norm, k_norm, n_lat):
    nt = q.shape[0]
    nc = nt - n_lat
    tq = GQA_TQ
    gw = GQA_GROUP * HEAD_DIM
    n_vb, vb = vt.shape[1], vt.shape[3]
    assert nt % tq == 0 and n_lat % tq == 0 and n_lat % GQA_TK == 0 and n_lat % GQA_TK_ONLINE == 0
    assert n_vb * vb == nt and n_lat % vb == 0 and nc % vb == 0 and GQA_TK_ONLINE % vb == 0
    bound = (1.01 * HEAD_DIM * (HEAD_DIM ** -0.5) * LOG2E) * jnp.max(jnp.abs(q_norm)) * jnp.max(jnp.abs(k_norm))
    kern = functools.partial(_gqa_kernel, n_lat=n_lat, nc=nc)
    return pl.pallas_call(
        kern,
        grid=(GQA_KV_HEADS, nt // tq),
        in_specs=[
            pl.BlockSpec(memory_space=pltpu.SMEM),
            pl.BlockSpec((tq, gw), lambda h, b: (b, h)),
            pl.BlockSpec((nt, HEAD_DIM), lambda h, b: (0, h)),
            pl.BlockSpec((None, n_vb, HEAD_DIM, vb), lambda h, b: (h, 0, 0, 0)),
            pl.BlockSpec((tq, gw), lambda h, b: (b, h)),
        ],
        out_specs=pl.BlockSpec((tq, gw), lambda h, b: (b, h)),
        out_shape=jax.ShapeDtypeStruct((nt, GQA_HEADS * HEAD_DIM), BF16),
        compiler_params=_cparams(2),
        name="gqa_attention",
    )(bound.reshape(1).astype(F32), q, k, vt, gs)


def kernel(x, c, ctx, c_ctx, norm_g, w_mod, b_mod, e_w_in, e_rpb, e_dw_w, e_dw_b, e_ln_g, e_ln_b, e_w_out,
           o_w_in, o_q_norm, o_k_norm, o_w_out, final_norm_g):
    batch, n_lat, d = x.shape
    nc = ctx.shape[1]
    depth = norm_g.shape[0]
    assert batch == 1
    rows = n_lat // GRID_W

    nt = n_lat + nc
    stream = (x[0], ctx[0], 0)
    cvec = jnp.zeros((8, d), F32).at[0].set(c[0]).at[1].set(c_ctx)
    mods = _modulation(cvec, w_mod, b_mod)
    cos_t, sin_t = _rope_tables(n_lat, nc)
    fg = final_norm_g.reshape(1, d)
    e_w_in_b = e_w_in.astype(BF16)
    o_w_in_b = o_w_in.astype(BF16)

    for l in range(depth):
        i = l // 2
        g = norm_g[l].reshape(1, d)
        mod = mods[l]
        final = l == depth - 1
        if l % 2 == 0:
            qkv, ga, u, gb = _even_proj(stream, nt, g, mod, e_w_in_b, i, n_lat)
            mix_a = _na_attention(qkv, ga, _na_bias_table(e_rpb[i], rows, nc), n_lat)
            mix_b = _conformer_conv(u, gb, e_dw_w[i], e_dw_b[i], e_ln_g[i], e_ln_b[i], n_lat)
            xs = _out_proj(mix_a, mix_b, (0, 0), e_w_out[i].astype(BF16), stream, nt, mod, fg, n_lat, final)
        else:
            v0 = (GQA_HEADS + GQA_KV_HEADS) * HEAD_DIM
            w_v = lax.optimization_barrier(o_w_in[i, :, v0:v0 + GQA_KV_HEADS * HEAD_DIM])
            w_vt = w_v.T.astype(BF16)
            q, k, vt, gs = _odd_proj(stream, nt, g, mod, o_w_in_b, w_vt, i, cos_t, sin_t,
                                     o_q_norm[i], o_k_norm[i], n_lat)
            mix = _gqa_attention(q, k, vt, gs, o_q_norm[i], o_k_norm[i], n_lat)
            xs = _out_proj(mix, mix, (0, 1), o_w_out[i].astype(BF16), stream, nt, mod, fg, n_lat, final)
        stream = (xs, xs, n_lat)
    return xs[None]
```

```python
import functools

import numpy as np
import jax
import jax.numpy as jnp
from jax import lax
from jax.experimental import pallas as pl
from jax.experimental.pallas import tpu as pltpu

F32 = jnp.float32
BF16 = jnp.bfloat16

GRID_W = 64
HEAD_DIM = 128
NORM_EPS = 1e-6
NA_HEADS = 8
NA_WIN_ROWS = 8
NA_WIN_COLS = 16
CONV_KERNEL = 31
GQA_HEADS = 16
GQA_KV_HEADS = 4
GQA_GROUP = GQA_HEADS // GQA_KV_HEADS
ROPE_THETA = 10000.0
LOG2E = 1.4426950408889634

LANES = 128
V7X_VMEM_BYTES = 64 * 1024 * 1024
VMEM_LIMIT = V7X_VMEM_BYTES - 8 * 1024 * 1024

TM_PROJ = 256
TM_OUT = 256
NA_ROWS = 4
NA_SLAB = NA_WIN_ROWS + NA_ROWS
NA_TQ = NA_ROWS * GRID_W
NA_HP = 4
CONV_T = 256
CONV_HALO = 16
CONV_CH = 32
GQA_TQ = 256
GQA_TK = 2048
GQA_TK_ONLINE = 512
GQA_SAFE_BOUND = 60.0
GQA_ONES_ROWS = 16
NEG_BIG = -1e30


def _cparams(n_axes):
    return pltpu.CompilerParams(dimension_semantics=("arbitrary",) * n_axes, vmem_limit_bytes=VMEM_LIMIT)


def _sigmoid(x):
    return 1.0 / (1.0 + jnp.exp(-x))


def _silu(x):
    return x * _sigmoid(x)


def _dot(a, b):
    return jnp.dot(a, b, preferred_element_type=F32)


def _dot_nt(a, b):
    return lax.dot_general(a, b, (((1,), (1,)), ((), ())), preferred_element_type=F32)


def _stream_specs(lat_src, ctx_src, ctx_row0, tm, n_lat):
    d = lat_src.shape[1]
    nlt = n_lat // tm
    n_ctx_tiles = (ctx_src.shape[0] - ctx_row0) // tm
    assert n_lat % tm == 0 and ctx_row0 % tm == 0 and n_ctx_tiles >= 1
    return [pl.BlockSpec((tm, d), lambda i: (jnp.minimum(i, nlt - 1), 0)),
            pl.BlockSpec((tm, d), lambda i: (ctx_row0 // tm + jnp.clip(i - nlt, 0, n_ctx_tiles - 1), 0))]


def _stream_tile(x_ref, c_ref, tile_idx, tm, n_lat):
    return jnp.where(tile_idx * tm >= n_lat, c_ref[...], x_ref[...])


def _prenorm(x_ref, c_ref, g_ref, mod_ref, tile_idx, tm, n_lat):
    d = x_ref.shape[-1]
    x = _stream_tile(x_ref, c_ref, tile_idx, tm, n_lat)
    ms = jnp.mean(x * x, axis=-1, keepdims=True)
    row = (tile_idx * tm >= n_lat).astype(jnp.int32)
    mod = mod_ref[pl.ds(row, 1), :]
    a = g_ref[...] * (1.0 + mod[:, d:2 * d])
    return (x * lax.rsqrt(ms + NORM_EPS) * a + mod[:, 0:d]).astype(BF16)


W_STAGE_COLS = 256


def _load_weights_bf16(w_hbm, layer, w_bf, stage, sem):
    cols = stage.shape[2]
    n_chunks = w_bf.shape[1] // cols

    def copy(c):
        return pltpu.make_async_copy(w_hbm.at[layer, :, pl.ds(c * cols, cols)], stage.at[c % 2], sem.at[c % 2])

    copy(0).start()
    for c in range(n_chunks):
        copy(c).wait()
        if c + 1 < n_chunks:
            copy(c + 1).start()
        w_bf[:, c * cols:(c + 1) * cols] = stage[c % 2].astype(BF16)


def _weight_scratch(w):
    d, n = w.shape[1:]
    assert n % W_STAGE_COLS == 0
    return [pltpu.VMEM((d, n), BF16), pltpu.VMEM((2, d, W_STAGE_COLS), F32), pltpu.SemaphoreType.DMA((2,))]


def _mod_kernel(c_ref, w_ref, b_ref, o_ref):
    s = _silu(c_ref[...]).astype(BF16)
    o_ref[...] = _dot(s, w_ref[...].astype(BF16)) + b_ref[...]


def _modulation(cvec, w_mod, b_mod):
    depth, d, d3 = w_mod.shape
    tn = 768
    return pl.pallas_call(
        _mod_kernel,
        grid=(depth, d3 // tn),
        in_specs=[
            pl.BlockSpec((8, d), lambda l, j: (0, 0)),
            pl.BlockSpec((None, d, tn), lambda l, j: (l, 0, j)),
            pl.BlockSpec((None, 1, tn), lambda l, j: (l, 0, j)),
        ],
        out_specs=pl.BlockSpec((None, 8, tn), lambda l, j: (l, 0, j)),
        out_shape=jax.ShapeDtypeStruct((depth, 8, d3), F32),
        compiler_params=_cparams(2),
        name="modulation",
    )(cvec, w_mod, b_mod.reshape(depth, 1, d3))


def _even_proj_kernel(x_ref, c_ref, g_ref, mod_ref, w_hbm, qkv_ref, ga_ref, u_ref, gb_ref, w_ref, stage, sem,
                      *, n_lat, tm, layer):
    @pl.when(pl.program_id(0) == 0)
    def _():
        _load_weights_bf16(w_hbm, layer, w_ref, stage, sem)

    h = _prenorm(x_ref, c_ref, g_ref, mod_ref, pl.program_id(0), tm, n_lat)
    cw = ga_ref.shape[-1]

    def proj(seg):
        return _dot(h, w_ref[:, seg * cw:(seg + 1) * cw])

    for seg in range(3):
        qkv_ref[:, seg * cw:(seg + 1) * cw] = proj(seg).astype(BF16)
    ga_ref[...] = _silu(proj(3)).astype(BF16)
    u_ref[...] = proj(4) * _sigmoid(proj(5))
    gb_ref[...] = _silu(proj(6)).astype(BF16)


def _even_proj(stream, nt, g, mod, w, layer, n_lat):
    d = stream[0].shape[1]
    tm = TM_PROJ
    cw = w.shape[2] // 7
    assert nt % tm == 0 and n_lat % tm == 0
    kern = functools.partial(_even_proj_kernel, n_lat=n_lat, tm=tm, layer=layer)
    rows = lambda width: pl.BlockSpec((tm, width), lambda i: (i, 0))
    return pl.pallas_call(
        kern,
        grid=(nt // tm,),
        in_specs=_stream_specs(*stream, tm, n_lat) + [
            pl.BlockSpec((1, d), lambda i: (0, 0)),
            pl.BlockSpec((8, 3 * d), lambda i: (0, 0)),
            pl.BlockSpec(memory_space=pl.ANY),
        ],
        out_specs=[rows(3 * cw), rows(cw), rows(cw), rows(cw)],
        out_shape=[
            jax.ShapeDtypeStruct((nt, 3 * cw), BF16),
            jax.ShapeDtypeStruct((nt, cw), BF16),
            jax.ShapeDtypeStruct((nt, cw), F32),
            jax.ShapeDtypeStruct((nt, cw), BF16),
        ],
        scratch_shapes=_weight_scratch(w),
        compiler_params=_cparams(1),
        name="even_proj",
    )(stream[0], stream[1], g, mod, w)


def _na_bias_kernel(rp_ref, o_ref, *, rows, nc):
    pattern = pl.program_id(0)
    lane = lax.broadcasted_iota(jnp.int32, (GRID_W, 2 * GRID_W), 1)
    qc = lax.broadcasted_iota(jnp.int32, (GRID_W, 2 * GRID_W), 0)
    kc = lane % GRID_W
    cs = jnp.clip(qc - NA_WIN_COLS // 2, 0, GRID_W - NA_WIN_COLS)
    col_ok = jnp.logical_and(kc >= cs, kc < cs + NA_WIN_COLS)
    left_half = lane < GRID_W
    masked = jnp.full((GRID_W, 2 * GRID_W), NEG_BIG, F32)

    def toeplitz(dr, right):
        x = jnp.broadcast_to(rp_ref[dr:dr + 1, :], (GRID_W, 2 * GRID_W))
        return pltpu.roll(x, 1 if right else GRID_W + 1, 1, stride=1, stride_axis=0)

    for pat_id, r0 in enumerate((0, 2 * NA_ROWS, rows - NA_ROWS)):
        @pl.when(pattern == pat_id)
        def _(r0=r0):
            ks = int(np.clip(r0 - NA_WIN_ROWS // 2, 0, rows - NA_SLAB))
            for r_local in range(NA_ROWS):
                r = r0 + r_local
                rs = int(np.clip(r - NA_WIN_ROWS // 2, 0, rows - NA_WIN_ROWS))
                for jp in range(NA_SLAB // 2):
                    halves = [toeplitz(kr - r + NA_WIN_ROWS - 1, side == 1) if rs <= kr < rs + NA_WIN_ROWS else None
                              for side, kr in enumerate((ks + 2 * jp, ks + 2 * jp + 1))]
                    if halves[0] is None and halves[1] is None:
                        tile = masked
                    else:
                        left = masked if halves[0] is None else halves[0]
                        right = masked if halves[1] is None else halves[1]
                        tile = jnp.where(col_ok, jnp.where(left_half, left, right), NEG_BIG)
                    o_ref[r_local * GRID_W:(r_local + 1) * GRID_W, jp * 2 * GRID_W:(jp + 1) * 2 * GRID_W] = tile
    o_ref[:, NA_SLAB * GRID_W:] = jnp.zeros((NA_TQ, nc), F32)


def _na_bias_table(rpb, rows, nc):
    heads = rpb.shape[0]
    assert NA_SLAB % 2 == 0 and 2 * GRID_W == LANES and rpb.shape[1] == 2 * NA_WIN_ROWS - 1
    pad = GRID_W - NA_WIN_COLS
    rp = jnp.pad(rpb.astype(F32), ((0, 0), (0, 1), (pad, LANES - pad - rpb.shape[2])))
    n_keys = NA_SLAB * GRID_W + nc
    kern = functools.partial(_na_bias_kernel, rows=rows, nc=nc)
    return pl.pallas_call(
        kern,
        grid=(3, heads),
        in_specs=[pl.BlockSpec((None, 2 * NA_WIN_ROWS, LANES), lambda p, h: (h, 0, 0))],
        out_specs=pl.BlockSpec((None, None, NA_TQ, n_keys), lambda p, h: (p, h, 0, 0)),
        out_shape=jax.ShapeDtypeStruct((3, heads, NA_TQ, n_keys), F32),
        compiler_params=_cparams(2),
        name="na_bias",
    )(rp)


def _na_kernel(q_ref, k_ref, v_ref, bias_ref, ga_ref, o_ref, *, n_lat, nc, rows):
    b = pl.program_id(1)
    nb_lat = rows // NA_ROWS
    scale = HEAD_DIM ** -0.5

    def attend(h, keys, values, bias):
        sl = slice(h * HEAD_DIM, (h + 1) * HEAD_DIM)
        s = _dot_nt(q_ref[:, sl], keys) * scale
        if bias is not None:
            s = s + bias
        p = jnp.exp(s - jnp.max(s, axis=-1, keepdims=True)).astype(BF16)
        o = _dot(p, jnp.concatenate([values, jnp.ones_like(values)], axis=1))
        gate = ga_ref[:, sl].astype(F32)
        o_ref[:, sl] = (o[:, :HEAD_DIM] / o[:, HEAD_DIM:] * gate).astype(BF16)

    @pl.when(b < nb_lat)
    def _():
        ks = jnp.clip(b * NA_ROWS - NA_WIN_ROWS // 2, 0, rows - NA_SLAB)
        start = pl.multiple_of(ks * GRID_W, GRID_W)
        for h in range(NA_HP):
            sl = slice(h * HEAD_DIM, (h + 1) * HEAD_DIM)
            keys = jnp.concatenate([k_ref[pl.ds(start, NA_SLAB * GRID_W), sl], k_ref[n_lat:n_lat + nc, sl]], axis=0)
            values = jnp.concatenate([v_ref[pl.ds(start, NA_SLAB * GRID_W), sl], v_ref[n_lat:n_lat + nc, sl]], axis=0)
            attend(h, keys, values, bias_ref[h])

    @pl.when(b >= nb_lat)
    def _():
        for h in range(NA_HP):
            sl = slice(h * HEAD_DIM, (h + 1) * HEAD_DIM)
            attend(h, k_ref[n_lat:n_lat + nc, sl], v_ref[n_lat:n_lat + nc, sl], None)


def _na_attention(qkv, ga, bias_tab, n_lat):
    nt = qkv.shape[0]
    nc = nt - n_lat
    rows = n_lat // GRID_W
    assert nc == NA_TQ and rows % NA_ROWS == 0 and rows >= NA_SLAB
    nb = nt // NA_TQ
    nb_lat = rows // NA_ROWS
    n_keys = NA_SLAB * GRID_W + nc
    hw = NA_HP * HEAD_DIM
    groups = NA_HEADS // NA_HP

    def pat(b):
        return jnp.where(b == 0, 0, jnp.where(b >= nb_lat - 1, 2, 1))

    kern = functools.partial(_na_kernel, n_lat=n_lat, nc=nc, rows=rows)
    return pl.pallas_call(
        kern,
        grid=(groups, nb),
        in_specs=[
            pl.BlockSpec((NA_TQ, hw), lambda h, b: (b, h)),
            pl.BlockSpec((nt, hw), lambda h, b: (0, groups + h)),
            pl.BlockSpec((nt, hw), lambda h, b: (0, 2 * groups + h)),
            pl.BlockSpec((None, NA_HP, NA_TQ, n_keys), lambda h, b: (pat(b), h, 0, 0)),
            pl.BlockSpec((NA_TQ, hw), lambda h, b: (b, h)),
        ],
        out_specs=pl.BlockSpec((NA_TQ, hw), lambda h, b: (b, h)),
        out_shape=jax.ShapeDtypeStruct((nt, NA_HEADS * HEAD_DIM), BF16),
        compiler_params=_cparams(2),
        name="na_attention",
    )(qkv, qkv, qkv, bias_tab, ga)


def _conv_kernel(um_ref, up_ref, un_ref, gb_ref, w_ref, b_ref, lg_ref, lb_ref, o_ref, ubuf, shifted, *, n_lat, nt):
    t = pl.program_id(0)
    start = t * CONV_T
    prev_ok = jnp.logical_and(start != 0, start != n_lat)
    next_ok = jnp.logical_and(start + CONV_T != n_lat, start + CONV_T != nt)
    ubuf[0:CONV_HALO, :] = jnp.where(prev_ok, up_ref[...], 0.0)
    ubuf[CONV_HALO:CONV_HALO + CONV_T, :] = um_ref[...]
    ubuf[CONV_HALO + CONV_T:, :] = jnp.where(next_ok, un_ref[...], 0.0)
    span = CONV_T + 2 * CONV_HALO - 8
    for b in range(1, 8):
        shifted[b - 1, :, :] = ubuf[b:b + span, :]
    off = CONV_HALO - CONV_KERNEL // 2
    for r0 in range(0, CONV_T, CONV_CH):
        acc = jnp.zeros((CONV_CH, um_ref.shape[-1]), F32) + b_ref[...]
        for j in range(CONV_KERNEL):
            a, b = divmod(off + j, 8)
            rows = slice(r0 + 8 * a, r0 + 8 * a + CONV_CH)
            tap = ubuf[rows, :] if b == 0 else shifted[b - 1, rows, :]
            acc = acc + w_ref[j:j + 1, :] * tap
        mu = jnp.mean(acc, axis=-1, keepdims=True)
        xc = acc - mu
        var = jnp.mean(xc * xc, axis=-1, keepdims=True)
        y = xc * lax.rsqrt(var + NORM_EPS) * lg_ref[...] + lb_ref[...]
        y = _silu(y) * gb_ref[r0:r0 + CONV_CH, :].astype(F32)
        o_ref[r0:r0 + CONV_CH, :] = y.astype(BF16)


def _conformer_conv(u, gb, dw_w, dw_b, ln_g, ln_b, n_lat):
    nt, cw = u.shape
    assert nt % CONV_T == 0 and n_lat % CONV_T == 0
    hb = CONV_T // CONV_HALO
    n_halo_blocks = nt // CONV_HALO
    w_pad = jnp.zeros((32, cw), F32).at[:CONV_KERNEL].set(dw_w)
    kern = functools.partial(_conv_kernel, n_lat=n_lat, nt=nt)
    vec = lambda: pl.BlockSpec((1, cw), lambda t: (0, 0))
    return pl.pallas_call(
        kern,
        grid=(nt // CONV_T,),
        in_specs=[
            pl.BlockSpec((CONV_T, cw), lambda t: (t, 0)),
            pl.BlockSpec((CONV_HALO, cw), lambda t: (jnp.maximum(t * hb - 1, 0), 0)),
            pl.BlockSpec((CONV_HALO, cw), lambda t: (jnp.minimum((t + 1) * hb, n_halo_blocks - 1), 0)),
            pl.BlockSpec((CONV_T, cw), lambda t: (t, 0)),
            pl.BlockSpec((32, cw), lambda t: (0, 0)),
            vec(), vec(), vec(),
        ],
        out_specs=pl.BlockSpec((CONV_T, cw), lambda t: (t, 0)),
        out_shape=jax.ShapeDtypeStruct((nt, cw), BF16),
        scratch_shapes=[pltpu.VMEM((CONV_T + 2 * CONV_HALO, cw), F32),
                        pltpu.VMEM((7, CONV_T + 2 * CONV_HALO - 8, cw), F32)],
        compiler_params=_cparams(1),
        name="conformer_conv",
    )(u, u, u, gb, w_pad, dw_b.reshape(1, cw), ln_g.reshape(1, cw), ln_b.reshape(1, cw))


def _out_proj_kernel(a_ref, b_ref, w32_ref, x_ref, c_ref, mod_ref, fg_ref, o_ref, w_ref, *, n_lat, tm, final):
    i = pl.program_id(0)

    @pl.when(i == 0)
    def _():
        w_ref[...] = w32_ref[...].astype(BF16)

    half = a_ref.shape[-1]
    y = _dot(a_ref[...], w_ref[0:half, :]) + _dot(b_ref[...], w_ref[half:, :])
    d = y.shape[-1]
    row = (i * tm >= n_lat).astype(jnp.int32)
    gt = mod_ref[pl.ds(row, 1), 2 * d:3 * d]
    xn = _stream_tile(x_ref, c_ref, i, tm, n_lat) + gt * y
    if final:
        ms = jnp.mean(xn * xn, axis=-1, keepdims=True)
        xn = xn * lax.rsqrt(ms + NORM_EPS) * fg_ref[...]
    o_ref[...] = xn


def _out_proj(mix_a, mix_b, cols, w, layer, stream, nt, mod, fg, n_lat, final):
    d = stream[0].shape[1]
    tm = TM_OUT
    half = w.shape[1] // 2
    out_rows = n_lat if final else nt
    assert out_rows % tm == 0
    kern = functools.partial(_out_proj_kernel, n_lat=n_lat, tm=tm, final=final)
    return pl.pallas_call(
        kern,
        grid=(out_rows // tm,),
        in_specs=[
            pl.BlockSpec((tm, half), lambda i: (i, cols[0])),
            pl.BlockSpec((tm, half), lambda i: (i, cols[1])),
            pl.BlockSpec((None, 2 * half, d), lambda i: (layer, 0, 0), pipeline_mode=pl.Buffered(1)),
        ] + _stream_specs(*stream, tm, n_lat) + [
            pl.BlockSpec((8, 3 * d), lambda i: (0, 0)),
            pl.BlockSpec((1, d), lambda i: (0, 0)),
        ],
        out_specs=pl.BlockSpec((tm, d), lambda i: (i, 0)),
        out_shape=jax.ShapeDtypeStruct((out_rows, d), F32),
        scratch_shapes=[pltpu.VMEM((2 * half, d), BF16)],
        compiler_params=_cparams(1),
        name="out_proj_final" if final else "out_proj",
    )(mix_a, mix_b, w, stream[0], stream[1], mod, fg)


def _rope_tables(n_lat, nc):
    rows = n_lat // GRID_W
    n_freq = HEAD_DIM // 4
    inv = ROPE_THETA ** (-jnp.arange(n_freq, dtype=F32) / n_freq)
    row_ang = jnp.arange(rows, dtype=jnp.int32).astype(F32)[:, None] * inv
    col_ang = jnp.arange(GRID_W, dtype=jnp.int32).astype(F32)[:, None] * inv
    per_row = lambda a: jnp.repeat(a, GRID_W, axis=0)
    per_col = lambda a: jnp.tile(a, (rows, 1))
    cr, sr = per_row(jnp.cos(row_ang)), per_row(jnp.sin(row_ang))
    cc, sc = per_col(jnp.cos(col_ang)), per_col(jnp.sin(col_ang))
    cos_t = jnp.concatenate([cr, cr, cc, cc], axis=-1)
    sin_t = jnp.concatenate([-sr, sr, -sc, sc], axis=-1)
    cos_t = jnp.concatenate([cos_t, jnp.ones((nc, HEAD_DIM), F32)], axis=0)
    sin_t = jnp.concatenate([sin_t, jnp.zeros((nc, HEAD_DIM), F32)], axis=0)
    return cos_t, sin_t


def _norm_rope(x, gain, cos_t, sin_t, first_half):
    ms = jnp.mean(x * x, axis=-1, keepdims=True)
    xn = x * lax.rsqrt(ms + NORM_EPS) * gain
    partner = jnp.where(first_half, pltpu.roll(xn, 3 * HEAD_DIM // 4, 1), pltpu.roll(xn, HEAD_DIM // 4, 1))
    return xn * cos_t + partner * sin_t


def _odd_proj_kernel(x_ref, c_ref, g_ref, mod_ref, w_hbm, wvt_ref, cos_ref, sin_ref, qn_ref, kn_ref,
                     q_ref, k_ref, vt_ref, gs_ref, w_ref, stage, sem, *, n_lat, tm, layer):
    @pl.when(pl.program_id(0) == 0)
    def _():
        _load_weights_bf16(w_hbm, layer, w_ref, stage, sem)

    h = _prenorm(x_ref, c_ref, g_ref, mod_ref, pl.program_id(0), tm, n_lat)
    tn = k_ref.shape[-1]
    cos_t = cos_ref[...]
    sin_t = sin_ref[...]
    lane = lax.broadcasted_iota(jnp.int32, (1, HEAD_DIM), 1)
    first_half = (lane % (HEAD_DIM // 2)) < (HEAD_DIM // 4)

    def proj(seg):
        return _dot(h, w_ref[:, seg * tn:(seg + 1) * tn])

    def rotated(seg, gain, out_ref, col0):
        acc = proj(seg)
        for hh in range(tn // HEAD_DIM):
            y = _norm_rope(acc[:, hh * HEAD_DIM:(hh + 1) * HEAD_DIM], gain, cos_t, sin_t, first_half)
            out_ref[:, col0 + hh * HEAD_DIM:col0 + (hh + 1) * HEAD_DIM] = y.astype(BF16)

    n_q = q_ref.shape[-1] // tn
    q_gain = qn_ref[...] * ((HEAD_DIM ** -0.5) * LOG2E)
    for seg in range(n_q):
        rotated(seg, q_gain, q_ref, seg * tn)
    rotated(n_q, kn_ref[...], k_ref, 0)
    v_t = _dot_nt(wvt_ref[...], h)
    vt_ref[...] = v_t.reshape(vt_ref.shape).astype(BF16)
    for seg in range(n_q):
        gs_ref[:, seg * tn:(seg + 1) * tn] = _silu(proj(n_q + 2 + seg)).astype(BF16)


def _odd_proj(stream, nt, g, mod, w, w_vt, layer, cos_t, sin_t, q_norm, k_norm, n_lat):
    d = stream[0].shape[1]
    tm = TM_PROJ
    qw = GQA_HEADS * HEAD_DIM
    kw = GQA_KV_HEADS * HEAD_DIM
    assert w.shape[2] == 2 * qw + 2 * kw and nt % tm == 0 and n_lat % tm == 0
    kern = functools.partial(_odd_proj_kernel, n_lat=n_lat, tm=tm, layer=layer)
    rows = lambda width: pl.BlockSpec((tm, width), lambda i: (i, 0))
    vec = lambda width: pl.BlockSpec((1, width), lambda i: (0, 0))
    return pl.pallas_call(
        kern,
        grid=(nt // tm,),
        in_specs=_stream_specs(*stream, tm, n_lat) + [
            vec(d),
            pl.BlockSpec((8, 3 * d), lambda i: (0, 0)),
            pl.BlockSpec(memory_space=pl.ANY),
            pl.BlockSpec((kw, d), lambda i: (0, 0)),
            rows(HEAD_DIM), rows(HEAD_DIM), vec(HEAD_DIM), vec(HEAD_DIM),
        ],
        out_specs=[rows(qw), rows(kw),
                   pl.BlockSpec((GQA_KV_HEADS, None, HEAD_DIM, tm), lambda i: (0, i, 0, 0)),
                   rows(qw)],
        out_shape=[
            jax.ShapeDtypeStruct((nt, qw), BF16),
            jax.ShapeDtypeStruct((nt, kw), BF16),
            jax.ShapeDtypeStruct((GQA_KV_HEADS, nt // tm, HEAD_DIM, tm), BF16),
            jax.ShapeDtypeStruct((nt, qw), BF16),
        ],
        scratch_shapes=_weight_scratch(w),
        compiler_params=_cparams(1),
        name="odd_proj",
    )(stream[0], stream[1], g, mod, w, w_vt, cos_t, sin_t,
      q_norm.reshape(1, HEAD_DIM), k_norm.reshape(1, HEAD_DIM))


def _gqa_kernel(bound_ref, q_ref, k_ref, vt_ref, g_ref, o_ref, *, n_lat, nc):
    b = pl.program_id(1)
    tq = q_ref.shape[0]
    vb = vt_ref.shape[-1]
    bound = bound_ref[0]
    is_lat = b * tq < n_lat
    q = jnp.concatenate([q_ref[:, h * HEAD_DIM:(h + 1) * HEAD_DIM] for h in range(GQA_GROUP)], axis=0)

    def vt_chunk(first_block, n_blocks):
        return jnp.concatenate([vt_ref[first_block + j] for j in range(n_blocks)], axis=1)

    def write_out(o_t):
        o = o_t.T
        for h in range(GQA_GROUP):
            sl = slice(h * HEAD_DIM, (h + 1) * HEAD_DIM)
            o_ref[:, sl] = (o[h * tq:(h + 1) * tq, :] * g_ref[:, sl].astype(F32)).astype(BF16)

    def fixed_shift_chunk(row0, n_rows):
        p = jnp.exp2(_dot_nt(k_ref[row0:row0 + n_rows, :], q) - bound).astype(BF16)
        vt = vt_chunk(row0 // vb, n_rows // vb)
        vt = jnp.concatenate([vt, jnp.ones((GQA_ONES_ROWS, n_rows), BF16)], axis=0)
        return _dot(vt, p)

    def finish(acc):
        write_out(acc[:HEAD_DIM, :] / acc[HEAD_DIM:HEAD_DIM + 1, :])

    small = bound <= GQA_SAFE_BOUND

    @pl.when(jnp.logical_and(small, is_lat))
    def _():
        acc = fixed_shift_chunk(n_lat, nc)
        for c in range(n_lat // GQA_TK):
            acc = acc + fixed_shift_chunk(c * GQA_TK, GQA_TK)
        finish(acc)

    @pl.when(jnp.logical_and(small, jnp.logical_not(is_lat)))
    def _():
        finish(fixed_shift_chunk(n_lat, nc))

    @pl.when(bound > GQA_SAFE_BOUND)
    def _():
        cols = GQA_GROUP * tq
        per = GQA_TK_ONLINE // vb

        def step(carry, k_chunk, vt):
            m, l, acc = carry
            s = _dot_nt(k_chunk, q)
            m_new = jnp.maximum(m, jnp.max(s, axis=0, keepdims=True))
            alpha = jnp.exp2(m - m_new)
            p = jnp.exp2(s - m_new)
            l = alpha * l + jnp.sum(p, axis=0, keepdims=True)
            acc = alpha * acc + _dot(vt, p.astype(BF16))
            return m_new, l, acc

        init = (jnp.full((1, cols), NEG_BIG, F32), jnp.zeros((1, cols), F32), jnp.zeros((HEAD_DIM, cols), F32))
        carry = step(init, k_ref[n_lat:n_lat + nc, :], vt_chunk(n_lat // vb, nc // vb))

        def body(c, carry):
            start = pl.multiple_of(c * GQA_TK_ONLINE, GQA_TK_ONLINE)
            vt = jnp.concatenate([vt_ref[c * per + j] for j in range(per)], axis=1)
            return step(carry, k_ref[pl.ds(start, GQA_TK_ONLINE), :], vt)

        m, l, acc = lax.fori_loop(0, jnp.where(is_lat, n_lat // GQA_TK_ONLINE, 0), body, carry)
        write_out(acc / l)


def _gqa_attention(q, k, vt, gs, q_norm, k_norm, n_lat):
    nt = q.shape[0]
    nc = nt - n_lat
    tq = GQA_TQ
    gw = GQA_GROUP * HEAD_DIM
    n_vb, vb = vt.shape[1], vt.shape[3]
    assert nt % tq == 0 and n_lat % tq == 0 and n_lat % GQA_TK == 0 and n_lat % GQA_TK_ONLINE == 0
    assert n_vb * vb == nt and n_lat % vb == 0 and nc % vb == 0 and GQA_TK_ONLINE % vb == 0
    bound = (1.01 * HEAD_DIM * (HEAD_DIM ** -0.5) * LOG2E) * jnp.max(jnp.abs(q_norm)) * jnp.max(jnp.abs(k_norm))
    kern = functools.partial(_gqa_kernel, n_lat=n_lat, nc=nc)
    return pl.pallas_call(
        kern,
        grid=(GQA_KV_HEADS, nt // tq),
        in_specs=[
            pl.BlockSpec(memory_space=pltpu.SMEM),
            pl.BlockSpec((tq, gw), lambda h, b: (b, h)),
            pl.BlockSpec((nt, HEAD_DIM), lambda h, b: (0, h)),
            pl.BlockSpec((None, n_vb, HEAD_DIM, vb), lambda h, b: (h, 0, 0, 0)),
            pl.BlockSpec((tq, gw), lambda h, b: (b, h)),
        ],
        out_specs=pl.BlockSpec((tq, gw), lambda h, b: (b, h)),
        out_shape=jax.ShapeDtypeStruct((nt, GQA_HEADS * HEAD_DIM), BF16),
        compiler_params=_cparams(2),
        name="gqa_attention",
    )(bound.reshape(1).astype(F32), q, k, vt, gs)


def kernel(x, c, ctx, c_ctx, norm_g, w_mod, b_mod, e_w_in, e_rpb, e_dw_w, e_dw_b, e_ln_g, e_ln_b, e_w_out,
           o_w_in, o_q_norm, o_k_norm, o_w_out, final_norm_g):
    batch, n_lat, d = x.shape
    nc = ctx.shape[1]
    depth = norm_g.shape[0]
    assert batch == 1
    rows = n_lat // GRID_W

    nt = n_lat + nc
    stream = (x[0], ctx[0], 0)
    cvec = jnp.zeros((8, d), F32).at[0].set(c[0]).at[1].set(c_ctx)
    mods = _modulation(cvec, w_mod, b_mod)
    cos_t, sin_t = _rope_tables(n_lat, nc)
    fg = final_norm_g.reshape(1, d)

    for l in range(depth):
        i = l // 2
        g = norm_g[l].reshape(1, d)
        mod = mods[l]
        final = l == depth - 1
        if l % 2 == 0:
            qkv, ga, u, gb = _even_proj(stream, nt, g, mod, e_w_in, i, n_lat)
            mix_a = _na_attention(qkv, ga, _na_bias_table(e_rpb[i], rows, nc), n_lat)
            mix_b = _conformer_conv(u, gb, e_dw_w[i], e_dw_b[i], e_ln_g[i], e_ln_b[i], n_lat)
            xs = _out_proj(mix_a, mix_b, (0, 0), e_w_out, i, stream, nt, mod, fg, n_lat, final)
        else:
            v0 = (GQA_HEADS + GQA_KV_HEADS) * HEAD_DIM
            w_vt = o_w_in[i, :, v0:v0 + GQA_KV_HEADS * HEAD_DIM].T.astype(BF16)
            q, k, vt, gs = _odd_proj(stream, nt, g, mod, o_w_in, w_vt, i, cos_t, sin_t,
                                     o_q_norm[i], o_k_norm[i], n_lat)
            mix = _gqa_attention(q, k, vt, gs, o_q_norm[i], o_k_norm[i], n_lat)
            xs = _out_proj(mix, mix, (0, 1), o_w_out, i, stream, nt, mod, fg, n_lat, final)
        stream = (xs, xs, n_lat)
    return xs[None]
```

```python
import functools

import numpy as np
import jax
import jax.numpy as jnp
from jax import lax
from jax.experimental import pallas as pl
from jax.experimental.pallas import tpu as pltpu

F32 = jnp.float32
BF16 = jnp.bfloat16

GRID_W = 64
HEAD_DIM = 128
NORM_EPS = 1e-6
NA_HEADS = 8
NA_WIN_ROWS = 8
NA_WIN_COLS = 16
CONV_KERNEL = 31
GQA_HEADS = 16
GQA_KV_HEADS = 4
GQA_GROUP = GQA_HEADS // GQA_KV_HEADS
ROPE_THETA = 10000.0
LOG2E = 1.4426950408889634

LANES = 128
V7X_VMEM_BYTES = 64 * 1024 * 1024
VMEM_LIMIT = V7X_VMEM_BYTES - 8 * 1024 * 1024

TM_PROJ = 256
TM_OUT = 256
NA_ROWS = 4
NA_SLAB = NA_WIN_ROWS + NA_ROWS
NA_TQ = NA_ROWS * GRID_W
NA_HP = 4
CONV_T = 256
CONV_HALO = 16
CONV_CH = 32
GQA_TQ = 256
GQA_TK = 2048
GQA_TK_ONLINE = 512
GQA_SAFE_BOUND = 60.0
GQA_ONES_ROWS = 16
NEG_BIG = -1e30


def _cparams(n_axes):
    return pltpu.CompilerParams(dimension_semantics=("arbitrary",) * n_axes, vmem_limit_bytes=VMEM_LIMIT)


def _sigmoid(x):
    return 1.0 / (1.0 + jnp.exp(-x))


def _silu(x):
    return x * _sigmoid(x)


def _dot(a, b):
    return jnp.dot(a, b, preferred_element_type=F32)


def _dot_nt(a, b):
    return lax.dot_general(a, b, (((1,), (1,)), ((), ())), preferred_element_type=F32)


def _stream_specs(lat_src, ctx_src, ctx_row0, tm, n_lat):
    d = lat_src.shape[1]
    nlt = n_lat // tm
    n_ctx_tiles = (ctx_src.shape[0] - ctx_row0) // tm
    assert n_lat % tm == 0 and ctx_row0 % tm == 0 and n_ctx_tiles >= 1
    return [pl.BlockSpec((tm, d), lambda i: (jnp.minimum(i, nlt - 1), 0)),
            pl.BlockSpec((tm, d), lambda i: (ctx_row0 // tm + jnp.clip(i - nlt, 0, n_ctx_tiles - 1), 0))]


def _stream_tile(x_ref, c_ref, tile_idx, tm, n_lat):
    return jnp.where(tile_idx * tm >= n_lat, c_ref[...], x_ref[...])


def _prenorm(x_ref, c_ref, g_ref, mod_ref, tile_idx, tm, n_lat):
    d = x_ref.shape[-1]
    x = _stream_tile(x_ref, c_ref, tile_idx, tm, n_lat)
    ms = jnp.mean(x * x, axis=-1, keepdims=True)
    row = (tile_idx * tm >= n_lat).astype(jnp.int32)
    mod = mod_ref[pl.ds(row, 1), :]
    a = g_ref[...] * (1.0 + mod[:, d:2 * d])
    return (x * lax.rsqrt(ms + NORM_EPS) * a + mod[:, 0:d]).astype(BF16)


W_STAGE_ROWS = 64


def _load_weights_bf16(w_hbm, layer, w_bf, stage, sem):
    rows = stage.shape[1]
    n_chunks = w_bf.shape[0] // rows

    def copy(c):
        return pltpu.make_async_copy(w_hbm.at[layer, pl.ds(c * rows, rows), :], stage.at[c % 2], sem.at[c % 2])

    copy(0).start()
    for c in range(n_chunks):
        copy(c).wait()
        if c + 1 < n_chunks:
            copy(c + 1).start()
        w_bf[c * rows:(c + 1) * rows, :] = stage[c % 2].astype(BF16)


def _weight_scratch(w):
    d, n = w.shape[1:]
    assert d % W_STAGE_ROWS == 0
    return [pltpu.VMEM((d, n), BF16), pltpu.VMEM((2, W_STAGE_ROWS, n), F32), pltpu.SemaphoreType.DMA((2,))]


def _mod_kernel(c_ref, w_ref, b_ref, o_ref):
    s = _silu(c_ref[...]).astype(BF16)
    o_ref[...] = _dot(s, w_ref[...].astype(BF16)) + b_ref[...]


def _modulation(cvec, w_mod, b_mod):
    depth, d, d3 = w_mod.shape
    tn = 768
    return pl.pallas_call(
        _mod_kernel,
        grid=(depth, d3 // tn),
        in_specs=[
            pl.BlockSpec((8, d), lambda l, j: (0, 0)),
            pl.BlockSpec((None, d, tn), lambda l, j: (l, 0, j)),
            pl.BlockSpec((None, 1, tn), lambda l, j: (l, 0, j)),
        ],
        out_specs=pl.BlockSpec((None, 8, tn), lambda l, j: (l, 0, j)),
        out_shape=jax.ShapeDtypeStruct((depth, 8, d3), F32),
        compiler_params=_cparams(2),
        name="modulation",
    )(cvec, w_mod, b_mod.reshape(depth, 1, d3))


def _even_proj_kernel(x_ref, c_ref, g_ref, mod_ref, w_hbm, qkv_ref, ga_ref, u_ref, gb_ref, w_ref, stage, sem,
                      *, n_lat, tm, layer):
    @pl.when(pl.program_id(0) == 0)
    def _():
        _load_weights_bf16(w_hbm, layer, w_ref, stage, sem)

    h = _prenorm(x_ref, c_ref, g_ref, mod_ref, pl.program_id(0), tm, n_lat)
    cw = ga_ref.shape[-1]

    def proj(seg):
        return _dot(h, w_ref[:, seg * cw:(seg + 1) * cw])

    for seg in range(3):
        qkv_ref[:, seg * cw:(seg + 1) * cw] = proj(seg).astype(BF16)
    ga_ref[...] = _silu(proj(3)).astype(BF16)
    u_ref[...] = proj(4) * _sigmoid(proj(5))
    gb_ref[...] = _silu(proj(6)).astype(BF16)


def _even_proj(stream, nt, g, mod, w, layer, n_lat):
    d = stream[0].shape[1]
    tm = TM_PROJ
    cw = w.shape[2] // 7
    assert nt % tm == 0 and n_lat % tm == 0
    kern = functools.partial(_even_proj_kernel, n_lat=n_lat, tm=tm, layer=layer)
    rows = lambda width: pl.BlockSpec((tm, width), lambda i: (i, 0))
    return pl.pallas_call(
        kern,
        grid=(nt // tm,),
        in_specs=_stream_specs(*stream, tm, n_lat) + [
            pl.BlockSpec((1, d), lambda i: (0, 0)),
            pl.BlockSpec((8, 3 * d), lambda i: (0, 0)),
            pl.BlockSpec(memory_space=pl.ANY),
        ],
        out_specs=[rows(3 * cw), rows(cw), rows(cw), rows(cw)],
        out_shape=[
            jax.ShapeDtypeStruct((nt, 3 * cw), BF16),
            jax.ShapeDtypeStruct((nt, cw), BF16),
            jax.ShapeDtypeStruct((nt, cw), F32),
            jax.ShapeDtypeStruct((nt, cw), BF16),
        ],
        scratch_shapes=_weight_scratch(w),
        compiler_params=_cparams(1),
        name="even_proj",
    )(stream[0], stream[1], g, mod, w)


def _na_bias_kernel(rp_ref, o_ref, *, rows, nc):
    pattern = pl.program_id(0)
    lane = lax.broadcasted_iota(jnp.int32, (GRID_W, 2 * GRID_W), 1)
    qc = lax.broadcasted_iota(jnp.int32, (GRID_W, 2 * GRID_W), 0)
    kc = lane % GRID_W
    cs = jnp.clip(qc - NA_WIN_COLS // 2, 0, GRID_W - NA_WIN_COLS)
    col_ok = jnp.logical_and(kc >= cs, kc < cs + NA_WIN_COLS)
    left_half = lane < GRID_W
    masked = jnp.full((GRID_W, 2 * GRID_W), NEG_BIG, F32)

    def toeplitz(dr, right):
        x = jnp.broadcast_to(rp_ref[dr:dr + 1, :], (GRID_W, 2 * GRID_W))
        return pltpu.roll(x, 1 if right else GRID_W + 1, 1, stride=1, stride_axis=0)

    for pat_id, r0 in enumerate((0, 2 * NA_ROWS, rows - NA_ROWS)):
        @pl.when(pattern == pat_id)
        def _(r0=r0):
            ks = int(np.clip(r0 - NA_WIN_ROWS // 2, 0, rows - NA_SLAB))
            for r_local in range(NA_ROWS):
                r = r0 + r_local
                rs = int(np.clip(r - NA_WIN_ROWS // 2, 0, rows - NA_WIN_ROWS))
                for jp in range(NA_SLAB // 2):
                    halves = [toeplitz(kr - r + NA_WIN_ROWS - 1, side == 1) if rs <= kr < rs + NA_WIN_ROWS else None
                              for side, kr in enumerate((ks + 2 * jp, ks + 2 * jp + 1))]
                    if halves[0] is None and halves[1] is None:
                        tile = masked
                    else:
                        left = masked if halves[0] is None else halves[0]
                        right = masked if halves[1] is None else halves[1]
                        tile = jnp.where(col_ok, jnp.where(left_half, left, right), NEG_BIG)
                    o_ref[r_local * GRID_W:(r_local + 1) * GRID_W, jp * 2 * GRID_W:(jp + 1) * 2 * GRID_W] = tile
    o_ref[:, NA_SLAB * GRID_W:] = jnp.zeros((NA_TQ, nc), F32)


def _na_bias_table(rpb, rows, nc):
    heads = rpb.shape[0]
    assert NA_SLAB % 2 == 0 and 2 * GRID_W == LANES and rpb.shape[1] == 2 * NA_WIN_ROWS - 1
    pad = GRID_W - NA_WIN_COLS
    rp = jnp.pad(rpb.astype(F32), ((0, 0), (0, 1), (pad, LANES - pad - rpb.shape[2])))
    n_keys = NA_SLAB * GRID_W + nc
    kern = functools.partial(_na_bias_kernel, rows=rows, nc=nc)
    return pl.pallas_call(
        kern,
        grid=(3, heads),
        in_specs=[pl.BlockSpec((None, 2 * NA_WIN_ROWS, LANES), lambda p, h: (h, 0, 0))],
        out_specs=pl.BlockSpec((None, None, NA_TQ, n_keys), lambda p, h: (p, h, 0, 0)),
        out_shape=jax.ShapeDtypeStruct((3, heads, NA_TQ, n_keys), F32),
        compiler_params=_cparams(2),
        name="na_bias",
    )(rp)


def _na_kernel(q_ref, k_ref, v_ref, bias_ref, ga_ref, o_ref, *, n_lat, nc, rows):
    b = pl.program_id(1)
    nb_lat = rows // NA_ROWS
    scale = HEAD_DIM ** -0.5

    def attend(h, keys, values, bias):
        sl = slice(h * HEAD_DIM, (h + 1) * HEAD_DIM)
        s = _dot_nt(q_ref[:, sl], keys) * scale
        if bias is not None:
            s = s + bias
        p = jnp.exp(s - jnp.max(s, axis=-1, keepdims=True)).astype(BF16)
        o = _dot(p, jnp.concatenate([values, jnp.ones_like(values)], axis=1))
        gate = ga_ref[:, sl].astype(F32)
        o_ref[:, sl] = (o[:, :HEAD_DIM] / o[:, HEAD_DIM:] * gate).astype(BF16)

    @pl.when(b < nb_lat)
    def _():
        ks = jnp.clip(b * NA_ROWS - NA_WIN_ROWS // 2, 0, rows - NA_SLAB)
        start = pl.multiple_of(ks * GRID_W, GRID_W)
        for h in range(NA_HP):
            sl = slice(h * HEAD_DIM, (h + 1) * HEAD_DIM)
            keys = jnp.concatenate([k_ref[pl.ds(start, NA_SLAB * GRID_W), sl], k_ref[n_lat:n_lat + nc, sl]], axis=0)
            values = jnp.concatenate([v_ref[pl.ds(start, NA_SLAB * GRID_W), sl], v_ref[n_lat:n_lat + nc, sl]], axis=0)
            attend(h, keys, values, bias_ref[h])

    @pl.when(b >= nb_lat)
    def _():
        for h in range(NA_HP):
            sl = slice(h * HEAD_DIM, (h + 1) * HEAD_DIM)
            attend(h, k_ref[n_lat:n_lat + nc, sl], v_ref[n_lat:n_lat + nc, sl], None)


def _na_attention(qkv, ga, bias_tab, n_lat):
    nt = qkv.shape[0]
    nc = nt - n_lat
    rows = n_lat // GRID_W
    assert nc == NA_TQ and rows % NA_ROWS == 0 and rows >= NA_SLAB
    nb = nt // NA_TQ
    nb_lat = rows // NA_ROWS
    n_keys = NA_SLAB * GRID_W + nc
    hw = NA_HP * HEAD_DIM
    groups = NA_HEADS // NA_HP

    def pat(b):
        return jnp.where(b == 0, 0, jnp.where(b >= nb_lat - 1, 2, 1))

    kern = functools.partial(_na_kernel, n_lat=n_lat, nc=nc, rows=rows)
    return pl.pallas_call(
        kern,
        grid=(groups, nb),
        in_specs=[
            pl.BlockSpec((NA_TQ, hw), lambda h, b: (b, h)),
            pl.BlockSpec((nt, hw), lambda h, b: (0, groups + h)),
            pl.BlockSpec((nt, hw), lambda h, b: (0, 2 * groups + h)),
            pl.BlockSpec((None, NA_HP, NA_TQ, n_keys), lambda h, b: (pat(b), h, 0, 0)),
            pl.BlockSpec((NA_TQ, hw), lambda h, b: (b, h)),
        ],
        out_specs=pl.BlockSpec((NA_TQ, hw), lambda h, b: (b, h)),
        out_shape=jax.ShapeDtypeStruct((nt, NA_HEADS * HEAD_DIM), BF16),
        compiler_params=_cparams(2),
        name="na_attention",
    )(qkv, qkv, qkv, bias_tab, ga)


def _conv_kernel(um_ref, up_ref, un_ref, gb_ref, w_ref, b_ref, lg_ref, lb_ref, o_ref, ubuf, shifted, *, n_lat, nt):
    t = pl.program_id(0)
    start = t * CONV_T
    prev_ok = jnp.logical_and(start != 0, start != n_lat)
    next_ok = jnp.logical_and(start + CONV_T != n_lat, start + CONV_T != nt)
    ubuf[0:CONV_HALO, :] = jnp.where(prev_ok, up_ref[...], 0.0)
    ubuf[CONV_HALO:CONV_HALO + CONV_T, :] = um_ref[...]
    ubuf[CONV_HALO + CONV_T:, :] = jnp.where(next_ok, un_ref[...], 0.0)
    span = CONV_T + 2 * CONV_HALO - 8
    for b in range(1, 8):
        shifted[b - 1, :, :] = ubuf[b:b + span, :]
    off = CONV_HALO - CONV_KERNEL // 2
    for r0 in range(0, CONV_T, CONV_CH):
        acc = jnp.zeros((CONV_CH, um_ref.shape[-1]), F32) + b_ref[...]
        for j in range(CONV_KERNEL):
            a, b = divmod(off + j, 8)
            rows = slice(r0 + 8 * a, r0 + 8 * a + CONV_CH)
            tap = ubuf[rows, :] if b == 0 else shifted[b - 1, rows, :]
            acc = acc + w_ref[j:j + 1, :] * tap
        mu = jnp.mean(acc, axis=-1, keepdims=True)
        xc = acc - mu
        var = jnp.mean(xc * xc, axis=-1, keepdims=True)
        y = xc * lax.rsqrt(var + NORM_EPS) * lg_ref[...] + lb_ref[...]
        y = _silu(y) * gb_ref[r0:r0 + CONV_CH, :].astype(F32)
        o_ref[r0:r0 + CONV_CH, :] = y.astype(BF16)


def _conformer_conv(u, gb, dw_w, dw_b, ln_g, ln_b, n_lat):
    nt, cw = u.shape
    assert nt % CONV_T == 0 and n_lat % CONV_T == 0
    hb = CONV_T // CONV_HALO
    n_halo_blocks = nt // CONV_HALO
    w_pad = jnp.zeros((32, cw), F32).at[:CONV_KERNEL].set(dw_w)
    kern = functools.partial(_conv_kernel, n_lat=n_lat, nt=nt)
    vec = lambda: pl.BlockSpec((1, cw), lambda t: (0, 0))
    return pl.pallas_call(
        kern,
        grid=(nt // CONV_T,),
        in_specs=[
            pl.BlockSpec((CONV_T, cw), lambda t: (t, 0)),
            pl.BlockSpec((CONV_HALO, cw), lambda t: (jnp.maximum(t * hb - 1, 0), 0)),
            pl.BlockSpec((CONV_HALO, cw), lambda t: (jnp.minimum((t + 1) * hb, n_halo_blocks - 1), 0)),
            pl.BlockSpec((CONV_T, cw), lambda t: (t, 0)),
            pl.BlockSpec((32, cw), lambda t: (0, 0)),
            vec(), vec(), vec(),
        ],
        out_specs=pl.BlockSpec((CONV_T, cw), lambda t: (t, 0)),
        out_shape=jax.ShapeDtypeStruct((nt, cw), BF16),
        scratch_shapes=[pltpu.VMEM((CONV_T + 2 * CONV_HALO, cw), F32),
                        pltpu.VMEM((7, CONV_T + 2 * CONV_HALO - 8, cw), F32)],
        compiler_params=_cparams(1),
        name="conformer_conv",
    )(u, u, u, gb, w_pad, dw_b.reshape(1, cw), ln_g.reshape(1, cw), ln_b.reshape(1, cw))


def _out_proj_kernel(a_ref, b_ref, w32_ref, x_ref, c_ref, mod_ref, fg_ref, o_ref, w_ref, *, n_lat, tm, final):
    i = pl.program_id(0)

    @pl.when(i == 0)
    def _():
        w_ref[...] = w32_ref[...].astype(BF16)

    half = a_ref.shape[-1]
    y = _dot(a_ref[...], w_ref[0:half, :]) + _dot(b_ref[...], w_ref[half:, :])
    d = y.shape[-1]
    row = (i * tm >= n_lat).astype(jnp.int32)
    gt = mod_ref[pl.ds(row, 1), 2 * d:3 * d]
    xn = _stream_tile(x_ref, c_ref, i, tm, n_lat) + gt * y
    if final:
        ms = jnp.mean(xn * xn, axis=-1, keepdims=True)
        xn = xn * lax.rsqrt(ms + NORM_EPS) * fg_ref[...]
    o_ref[...] = xn


def _out_proj(mix_a, mix_b, cols, w, layer, stream, nt, mod, fg, n_lat, final):
    d = stream[0].shape[1]
    tm = TM_OUT
    half = w.shape[1] // 2
    out_rows = n_lat if final else nt
    assert out_rows % tm == 0
    kern = functools.partial(_out_proj_kernel, n_lat=n_lat, tm=tm, final=final)
    return pl.pallas_call(
        kern,
        grid=(out_rows // tm,),
        in_specs=[
            pl.BlockSpec((tm, half), lambda i: (i, cols[0])),
            pl.BlockSpec((tm, half), lambda i: (i, cols[1])),
            pl.BlockSpec((None, 2 * half, d), lambda i: (layer, 0, 0), pipeline_mode=pl.Buffered(1)),
        ] + _stream_specs(*stream, tm, n_lat) + [
            pl.BlockSpec((8, 3 * d), lambda i: (0, 0)),
            pl.BlockSpec((1, d), lambda i: (0, 0)),
        ],
        out_specs=pl.BlockSpec((tm, d), lambda i: (i, 0)),
        out_shape=jax.ShapeDtypeStruct((out_rows, d), F32),
        scratch_shapes=[pltpu.VMEM((2 * half, d), BF16)],
        compiler_params=_cparams(1),
        name="out_proj_final" if final else "out_proj",
    )(mix_a, mix_b, w, stream[0], stream[1], mod, fg)


def _rope_tables(n_lat, nc):
    rows = n_lat // GRID_W
    n_freq = HEAD_DIM // 4
    inv = ROPE_THETA ** (-jnp.arange(n_freq, dtype=F32) / n_freq)
    row_ang = jnp.arange(rows, dtype=jnp.int32).astype(F32)[:, None] * inv
    col_ang = jnp.arange(GRID_W, dtype=jnp.int32).astype(F32)[:, None] * inv
    per_row = lambda a: jnp.repeat(a, GRID_W, axis=0)
    per_col = lambda a: jnp.tile(a, (rows, 1))
    cr, sr = per_row(jnp.cos(row_ang)), per_row(jnp.sin(row_ang))
    cc, sc = per_col(jnp.cos(col_ang)), per_col(jnp.sin(col_ang))
    cos_t = jnp.concatenate([cr, cr, cc, cc], axis=-1)
    sin_t = jnp.concatenate([-sr, sr, -sc, sc], axis=-1)
    cos_t = jnp.concatenate([cos_t, jnp.ones((nc, HEAD_DIM), F32)], axis=0)
    sin_t = jnp.concatenate([sin_t, jnp.zeros((nc, HEAD_DIM), F32)], axis=0)
    return cos_t, sin_t


def _norm_rope(x, gain, cos_t, sin_t, first_half):
    ms = jnp.mean(x * x, axis=-1, keepdims=True)
    xn = x * lax.rsqrt(ms + NORM_EPS) * gain
    partner = jnp.where(first_half, pltpu.roll(xn, 3 * HEAD_DIM // 4, 1), pltpu.roll(xn, HEAD_DIM // 4, 1))
    return xn * cos_t + partner * sin_t


def _odd_proj_kernel(x_ref, c_ref, g_ref, mod_ref, w_hbm, wvt_ref, cos_ref, sin_ref, qn_ref, kn_ref,
                     q_ref, k_ref, vt_ref, gs_ref, w_ref, stage, sem, *, n_lat, tm, layer):
    @pl.when(pl.program_id(0) == 0)
    def _():
        _load_weights_bf16(w_hbm, layer, w_ref, stage, sem)

    h = _prenorm(x_ref, c_ref, g_ref, mod_ref, pl.program_id(0), tm, n_lat)
    tn = k_ref.shape[-1]
    cos_t = cos_ref[...]
    sin_t = sin_ref[...]
    lane = lax.broadcasted_iota(jnp.int32, (1, HEAD_DIM), 1)
    first_half = (lane % (HEAD_DIM // 2)) < (HEAD_DIM // 4)

    def proj(seg):
        return _dot(h, w_ref[:, seg * tn:(seg + 1) * tn])

    def rotated(seg, gain, out_ref, col0):
        acc = proj(seg)
        for hh in range(tn // HEAD_DIM):
            y = _norm_rope(acc[:, hh * HEAD_DIM:(hh + 1) * HEAD_DIM], gain, cos_t, sin_t, first_half)
            out_ref[:, col0 + hh * HEAD_DIM:col0 + (hh + 1) * HEAD_DIM] = y.astype(BF16)

    n_q = q_ref.shape[-1] // tn
    q_gain = qn_ref[...] * ((HEAD_DIM ** -0.5) * LOG2E)
    for seg in range(n_q):
        rotated(seg, q_gain, q_ref, seg * tn)
    rotated(n_q, kn_ref[...], k_ref, 0)
    v_t = _dot_nt(wvt_ref[...], h)
    vt_ref[...] = v_t.reshape(vt_ref.shape).astype(BF16)
    for seg in range(n_q):
        gs_ref[:, seg * tn:(seg + 1) * tn] = _silu(proj(n_q + 2 + seg)).astype(BF16)


def _odd_proj(stream, nt, g, mod, w, w_vt, layer, cos_t, sin_t, q_norm, k_norm, n_lat):
    d = stream[0].shape[1]
    tm = TM_PROJ
    qw = GQA_HEADS * HEAD_DIM
    kw = GQA_KV_HEADS * HEAD_DIM
    assert w.shape[2] == 2 * qw + 2 * kw and nt % tm == 0 and n_lat % tm == 0
    kern = functools.partial(_odd_proj_kernel, n_lat=n_lat, tm=tm, layer=layer)
    rows = lambda width: pl.BlockSpec((tm, width), lambda i: (i, 0))
    vec = lambda width: pl.BlockSpec((1, width), lambda i: (0, 0))
    return pl.pallas_call(
        kern,
        grid=(nt // tm,),
        in_specs=_stream_specs(*stream, tm, n_lat) + [
            vec(d),
            pl.BlockSpec((8, 3 * d), lambda i: (0, 0)),
            pl.BlockSpec(memory_space=pl.ANY),
            pl.BlockSpec((kw, d), lambda i: (0, 0)),
            rows(HEAD_DIM), rows(HEAD_DIM), vec(HEAD_DIM), vec(HEAD_DIM),
        ],
        out_specs=[rows(qw), rows(kw),
                   pl.BlockSpec((GQA_KV_HEADS, None, HEAD_DIM, tm), lambda i: (0, i, 0, 0)),
                   rows(qw)],
        out_shape=[
            jax.ShapeDtypeStruct((nt, qw), BF16),
            jax.ShapeDtypeStruct((nt, kw), BF16),
            jax.ShapeDtypeStruct((GQA_KV_HEADS, nt // tm, HEAD_DIM, tm), BF16),
            jax.ShapeDtypeStruct((nt, qw), BF16),
        ],
        scratch_shapes=_weight_scratch(w),
        compiler_params=_cparams(1),
        name="odd_proj",
    )(stream[0], stream[1], g, mod, w, w_vt, cos_t, sin_t,
      q_norm.reshape(1, HEAD_DIM), k_norm.reshape(1, HEAD_DIM))


def _gqa_kernel(bound_ref, q_ref, k_ref, vt_ref, g_ref, o_ref, *, n_lat, nc):
    b = pl.program_id(1)
    tq = q_ref.shape[0]
    vb = vt_ref.shape[-1]
    bound = bound_ref[0]
    is_lat = b * tq < n_lat
    q = jnp.concatenate([q_ref[:, h * HEAD_DIM:(h + 1) * HEAD_DIM] for h in range(GQA_GROUP)], axis=0)

    def vt_chunk(first_block, n_blocks):
        return jnp.concatenate([vt_ref[first_block + j] for j in range(n_blocks)], axis=1)

    def write_out(o_t):
        o = o_t.T
        for h in range(GQA_GROUP):
            sl = slice(h * HEAD_DIM, (h + 1) * HEAD_DIM)
            o_ref[:, sl] = (o[h * tq:(h + 1) * tq, :] * g_ref[:, sl].astype(F32)).astype(BF16)

    def fixed_shift_chunk(row0, n_rows):
        p = jnp.exp2(_dot_nt(k_ref[row0:row0 + n_rows, :], q) - bound).astype(BF16)
        vt = vt_chunk(row0 // vb, n_rows // vb)
        vt = jnp.concatenate([vt, jnp.ones((GQA_ONES_ROWS, n_rows), BF16)], axis=0)
        return _dot(vt, p)

    def finish(acc):
        write_out(acc[:HEAD_DIM, :] / acc[HEAD_DIM:HEAD_DIM + 1, :])

    small = bound <= GQA_SAFE_BOUND

    @pl.when(jnp.logical_and(small, is_lat))
    def _():
        acc = fixed_shift_chunk(n_lat, nc)
        for c in range(n_lat // GQA_TK):
            acc = acc + fixed_shift_chunk(c * GQA_TK, GQA_TK)
        finish(acc)

    @pl.when(jnp.logical_and(small, jnp.logical_not(is_lat)))
    def _():
        finish(fixed_shift_chunk(n_lat, nc))

    @pl.when(bound > GQA_SAFE_BOUND)
    def _():
        cols = GQA_GROUP * tq
        per = GQA_TK_ONLINE // vb

        def step(carry, k_chunk, vt):
            m, l, acc = carry
            s = _dot_nt(k_chunk, q)
            m_new = jnp.maximum(m, jnp.max(s, axis=0, keepdims=True))
            alpha = jnp.exp2(m - m_new)
            p = jnp.exp2(s - m_new)
            l = alpha * l + jnp.sum(p, axis=0, keepdims=True)
            acc = alpha * acc + _dot(vt, p.astype(BF16))
            return m_new, l, acc

        init = (jnp.full((1, cols), NEG_BIG, F32), jnp.zeros((1, cols), F32), jnp.zeros((HEAD_DIM, cols), F32))
        carry = step(init, k_ref[n_lat:n_lat + nc, :], vt_chunk(n_lat // vb, nc // vb))

        def body(c, carry):
            start = pl.multiple_of(c * GQA_TK_ONLINE, GQA_TK_ONLINE)
            vt = jnp.concatenate([vt_ref[c * per + j] for j in range(per)], axis=1)
            return step(carry, k_ref[pl.ds(start, GQA_TK_ONLINE), :], vt)

        m, l, acc = lax.fori_loop(0, jnp.where(is_lat, n_lat // GQA_TK_ONLINE, 0), body, carry)
        write_out(acc / l)


def _gqa_attention(q, k, vt, gs, q_norm, k_norm, n_lat):
    nt = q.shape[0]
    nc = nt - n_lat
    tq = GQA_TQ
    gw = GQA_GROUP * HEAD_DIM
    n_vb, vb = vt.shape[1], vt.shape[3]
    assert nt % tq == 0 and n_lat % tq == 0 and n_lat % GQA_TK == 0 and n_lat % GQA_TK_ONLINE == 0
    assert n_vb * vb == nt and n_lat % vb == 0 and nc % vb == 0 and GQA_TK_ONLINE % vb == 0
    bound = (1.01 * HEAD_DIM * (HEAD_DIM ** -0.5) * LOG2E) * jnp.max(jnp.abs(q_norm)) * jnp.max(jnp.abs(k_norm))
    kern = functools.partial(_gqa_kernel, n_lat=n_lat, nc=nc)
    return pl.pallas_call(
        kern,
        grid=(GQA_KV_HEADS, nt // tq),
        in_specs=[
            pl.BlockSpec(memory_space=pltpu.SMEM),
            pl.BlockSpec((tq, gw), lambda h, b: (b, h)),
            pl.BlockSpec((nt, HEAD_DIM), lambda h, b: (0, h)),
            pl.BlockSpec((None, n_vb, HEAD_DIM, vb), lambda h, b: (h, 0, 0, 0)),
            pl.BlockSpec((tq, gw), lambda h, b: (b, h)),
        ],
        out_specs=pl.BlockSpec((tq, gw), lambda h, b: (b, h)),
        out_shape=jax.ShapeDtypeStruct((nt, GQA_HEADS * HEAD_DIM), BF16),
        compiler_params=_cparams(2),
        name="gqa_attention",
    )(bound.reshape(1).astype(F32), q, k, vt, gs)


def kernel(x, c, ctx, c_ctx, norm_g, w_mod, b_mod, e_w_in, e_rpb, e_dw_w, e_dw_b, e_ln_g, e_ln_b, e_w_out,
           o_w_in, o_q_norm, o_k_norm, o_w_out, final_norm_g):
    batch, n_lat, d = x.shape
    nc = ctx.shape[1]
    depth = norm_g.shape[0]
    assert batch == 1
    rows = n_lat // GRID_W

    nt = n_lat + nc
    stream = (x[0], ctx[0], 0)
    cvec = jnp.zeros((8, d), F32).at[0].set(c[0]).at[1].set(c_ctx)
    mods = _modulation(cvec, w_mod, b_mod)
    cos_t, sin_t = _rope_tables(n_lat, nc)
    fg = final_norm_g.reshape(1, d)

    for l in range(depth):
        i = l // 2
        g = norm_g[l].reshape(1, d)
        mod = mods[l]
        final = l == depth - 1
        if l % 2 == 0:
            qkv, ga, u, gb = _even_proj(stream, nt, g, mod, e_w_in, i, n_lat)
            mix_a = _na_attention(qkv, ga, _na_bias_table(e_rpb[i], rows, nc), n_lat)
            mix_b = _conformer_conv(u, gb, e_dw_w[i], e_dw_b[i], e_ln_g[i], e_ln_b[i], n_lat)
            xs = _out_proj(mix_a, mix_b, (0, 0), e_w_out, i, stream, nt, mod, fg, n_lat, final)
        else:
            v0 = (GQA_HEADS + GQA_KV_HEADS) * HEAD_DIM
            w_v = lax.optimization_barrier(o_w_in[i, :, v0:v0 + GQA_KV_HEADS * HEAD_DIM])
            w_vt = w_v.T.astype(BF16)
            q, k, vt, gs = _odd_proj(stream, nt, g, mod, o_w_in, w_vt, i, cos_t, sin_t,
                                     o_q_norm[i], o_k_norm[i], n_lat)
            mix = _gqa_attention(q, k, vt, gs, o_q_norm[i], o_k_norm[i], n_lat)
            xs = _out_proj(mix, mix, (0, 1), o_w_out, i, stream, nt, mod, fg, n_lat, final)
        stream = (xs, xs, n_lat)
    return xs[None]
```

```python
import functools

import numpy as np
import jax
import jax.numpy as jnp
from jax import lax
from jax.experimental import pallas as pl
from jax.experimental.pallas import tpu as pltpu

F32 = jnp.float32
BF16 = jnp.bfloat16

GRID_W = 64
HEAD_DIM = 128
NORM_EPS = 1e-6
NA_HEADS = 8
NA_WIN_ROWS = 8
NA_WIN_COLS = 16
CONV_KERNEL = 31
GQA_HEADS = 16
GQA_KV_HEADS = 4
GQA_GROUP = GQA_HEADS // GQA_KV_HEADS
ROPE_THETA = 10000.0
LOG2E = 1.4426950408889634

LANES = 128
V7X_VMEM_BYTES = 64 * 1024 * 1024
VMEM_LIMIT = V7X_VMEM_BYTES - 8 * 1024 * 1024

TM_PROJ = 256
TM_OUT = 256
NA_ROWS = 4
NA_SLAB = NA_WIN_ROWS + NA_ROWS
NA_TQ = NA_ROWS * GRID_W
NA_HP = 4
CONV_T = 256
CONV_HALO = 16
CONV_CH = 32
GQA_TQ = 256
GQA_TK = 2048
GQA_TK_ONLINE = 512
GQA_SAFE_BOUND = 60.0
GQA_ONES_ROWS = 16
NEG_BIG = -1e30


def _cparams(n_axes):
    return pltpu.CompilerParams(dimension_semantics=("arbitrary",) * n_axes, vmem_limit_bytes=VMEM_LIMIT)


def _sigmoid(x):
    return 1.0 / (1.0 + jnp.exp(-x))


def _silu(x):
    return x * _sigmoid(x)


def _dot(a, b):
    return jnp.dot(a, b, preferred_element_type=F32)


def _dot_nt(a, b):
    return lax.dot_general(a, b, (((1,), (1,)), ((), ())), preferred_element_type=F32)


def _stream_specs(lat_src, ctx_src, ctx_row0, tm, n_lat):
    d = lat_src.shape[1]
    nlt = n_lat // tm
    n_ctx_tiles = (ctx_src.shape[0] - ctx_row0) // tm
    assert n_lat % tm == 0 and ctx_row0 % tm == 0 and n_ctx_tiles >= 1
    return [pl.BlockSpec((tm, d), lambda i: (jnp.minimum(i, nlt - 1), 0)),
            pl.BlockSpec((tm, d), lambda i: (ctx_row0 // tm + jnp.clip(i - nlt, 0, n_ctx_tiles - 1), 0))]


def _stream_tile(x_ref, c_ref, tile_idx, tm, n_lat):
    return jnp.where(tile_idx * tm >= n_lat, c_ref[...], x_ref[...])


def _prenorm(x_ref, c_ref, g_ref, mod_ref, tile_idx, tm, n_lat):
    d = x_ref.shape[-1]
    x = _stream_tile(x_ref, c_ref, tile_idx, tm, n_lat)
    ms = jnp.mean(x * x, axis=-1, keepdims=True)
    row = (tile_idx * tm >= n_lat).astype(jnp.int32)
    mod = mod_ref[pl.ds(row, 1), :]
    a = g_ref[...] * (1.0 + mod[:, d:2 * d])
    return (x * lax.rsqrt(ms + NORM_EPS) * a + mod[:, 0:d]).astype(BF16)


W_STAGE_ROWS = 32
W_STAGE_SLOTS = 4


def _load_weights_bf16(w_hbm, layer, w_bf, stage, sem):
    slots, rows = stage.shape[0], stage.shape[1]
    n_chunks = w_bf.shape[0] // rows

    def copy(c):
        return pltpu.make_async_copy(w_hbm.at[layer, pl.ds(c * rows, rows), :],
                                     stage.at[c % slots], sem.at[c % slots])

    for c in range(min(slots - 1, n_chunks)):
        copy(c).start()
    for c in range(n_chunks):
        copy(c).wait()
        if c + slots - 1 < n_chunks:
            copy(c + slots - 1).start()
        w_bf[c * rows:(c + 1) * rows, :] = stage[c % slots].astype(BF16)


def _weight_scratch(w):
    d, n = w.shape[1:]
    assert d % W_STAGE_ROWS == 0
    return [pltpu.VMEM((d, n), BF16), pltpu.VMEM((W_STAGE_SLOTS, W_STAGE_ROWS, n), F32),
            pltpu.SemaphoreType.DMA((W_STAGE_SLOTS,))]


def _mod_kernel(c_ref, w_ref, b_ref, o_ref):
    s = _silu(c_ref[...]).astype(BF16)
    o_ref[...] = _dot(s, w_ref[...].astype(BF16)) + b_ref[...]


def _modulation(cvec, w_mod, b_mod):
    depth, d, d3 = w_mod.shape
    tn = 768
    return pl.pallas_call(
        _mod_kernel,
        grid=(depth, d3 // tn),
        in_specs=[
            pl.BlockSpec((8, d), lambda l, j: (0, 0)),
            pl.BlockSpec((None, d, tn), lambda l, j: (l, 0, j)),
            pl.BlockSpec((None, 1, tn), lambda l, j: (l, 0, j)),
        ],
        out_specs=pl.BlockSpec((None, 8, tn), lambda l, j: (l, 0, j)),
        out_shape=jax.ShapeDtypeStruct((depth, 8, d3), F32),
        compiler_params=_cparams(2),
        name="modulation",
    )(cvec, w_mod, b_mod.reshape(depth, 1, d3))


def _even_proj_kernel(x_ref, c_ref, g_ref, mod_ref, w_hbm, qkv_ref, ga_ref, u_ref, gb_ref, w_ref, stage, sem,
                      *, n_lat, tm, layer):
    @pl.when(pl.program_id(0) == 0)
    def _():
        _load_weights_bf16(w_hbm, layer, w_ref, stage, sem)

    h = _prenorm(x_ref, c_ref, g_ref, mod_ref, pl.program_id(0), tm, n_lat)
    cw = ga_ref.shape[-1]

    def proj(seg):
        return _dot(h, w_ref[:, seg * cw:(seg + 1) * cw])

    for seg in range(3):
        qkv_ref[:, seg * cw:(seg + 1) * cw] = proj(seg).astype(BF16)
    ga_ref[...] = _silu(proj(3)).astype(BF16)
    u_ref[...] = proj(4) * _sigmoid(proj(5))
    gb_ref[...] = _silu(proj(6)).astype(BF16)


def _even_proj(stream, nt, g, mod, w, layer, n_lat):
    d = stream[0].shape[1]
    tm = TM_PROJ
    cw = w.shape[2] // 7
    assert nt % tm == 0 and n_lat % tm == 0
    kern = functools.partial(_even_proj_kernel, n_lat=n_lat, tm=tm, layer=layer)
    rows = lambda width: pl.BlockSpec((tm, width), lambda i: (i, 0))
    return pl.pallas_call(
        kern,
        grid=(nt // tm,),
        in_specs=_stream_specs(*stream, tm, n_lat) + [
            pl.BlockSpec((1, d), lambda i: (0, 0)),
            pl.BlockSpec((8, 3 * d), lambda i: (0, 0)),
            pl.BlockSpec(memory_space=pl.ANY),
        ],
        out_specs=[rows(3 * cw), rows(cw), rows(cw), rows(cw)],
        out_shape=[
            jax.ShapeDtypeStruct((nt, 3 * cw), BF16),
            jax.ShapeDtypeStruct((nt, cw), BF16),
            jax.ShapeDtypeStruct((nt, cw), F32),
            jax.ShapeDtypeStruct((nt, cw), BF16),
        ],
        scratch_shapes=_weight_scratch(w),
        compiler_params=_cparams(1),
        name="even_proj",
    )(stream[0], stream[1], g, mod, w)


def _na_bias_kernel(rp_ref, o_ref, *, rows, nc):
    pattern = pl.program_id(0)
    lane = lax.broadcasted_iota(jnp.int32, (GRID_W, 2 * GRID_W), 1)
    qc = lax.broadcasted_iota(jnp.int32, (GRID_W, 2 * GRID_W), 0)
    kc = lane % GRID_W
    cs = jnp.clip(qc - NA_WIN_COLS // 2, 0, GRID_W - NA_WIN_COLS)
    col_ok = jnp.logical_and(kc >= cs, kc < cs + NA_WIN_COLS)
    left_half = lane < GRID_W
    masked = jnp.full((GRID_W, 2 * GRID_W), NEG_BIG, F32)

    def toeplitz(dr, right):
        x = jnp.broadcast_to(rp_ref[dr:dr + 1, :], (GRID_W, 2 * GRID_W))
        return pltpu.roll(x, 1 if right else GRID_W + 1, 1, stride=1, stride_axis=0)

    for pat_id, r0 in enumerate((0, 2 * NA_ROWS, rows - NA_ROWS)):
        @pl.when(pattern == pat_id)
        def _(r0=r0):
            ks = int(np.clip(r0 - NA_WIN_ROWS // 2, 0, rows - NA_SLAB))
            for r_local in range(NA_ROWS):
                r = r0 + r_local
                rs = int(np.clip(r - NA_WIN_ROWS // 2, 0, rows - NA_WIN_ROWS))
                for jp in range(NA_SLAB // 2):
                    halves = [toeplitz(kr - r + NA_WIN_ROWS - 1, side == 1) if rs <= kr < rs + NA_WIN_ROWS else None
                              for side, kr in enumerate((ks + 2 * jp, ks + 2 * jp + 1))]
                    if halves[0] is None and halves[1] is None:
                        tile = masked
                    else:
                        left = masked if halves[0] is None else halves[0]
                        right = masked if halves[1] is None else halves[1]
                        tile = jnp.where(col_ok, jnp.where(left_half, left, right), NEG_BIG)
                    o_ref[r_local * GRID_W:(r_local + 1) * GRID_W, jp * 2 * GRID_W:(jp + 1) * 2 * GRID_W] = tile
    o_ref[:, NA_SLAB * GRID_W:] = jnp.zeros((NA_TQ, nc), F32)


def _na_bias_table(rpb, rows, nc):
    heads = rpb.shape[0]
    assert NA_SLAB % 2 == 0 and 2 * GRID_W == LANES and rpb.shape[1] == 2 * NA_WIN_ROWS - 1
    pad = GRID_W - NA_WIN_COLS
    rp = jnp.pad(rpb.astype(F32), ((0, 0), (0, 1), (pad, LANES - pad - rpb.shape[2])))
    n_keys = NA_SLAB * GRID_W + nc
    kern = functools.partial(_na_bias_kernel, rows=rows, nc=nc)
    return pl.pallas_call(
        kern,
        grid=(3, heads),
        in_specs=[pl.BlockSpec((None, 2 * NA_WIN_ROWS, LANES), lambda p, h: (h, 0, 0))],
        out_specs=pl.BlockSpec((None, None, NA_TQ, n_keys), lambda p, h: (p, h, 0, 0)),
        out_shape=jax.ShapeDtypeStruct((3, heads, NA_TQ, n_keys), F32),
        compiler_params=_cparams(2),
        name="na_bias",
    )(rp)


def _na_kernel(q_ref, k_ref, v_ref, bias_ref, ga_ref, o_ref, *, n_lat, nc, rows):
    b = pl.program_id(1)
    nb_lat = rows // NA_ROWS
    scale = HEAD_DIM ** -0.5

    def attend(h, keys, values, bias):
        sl = slice(h * HEAD_DIM, (h + 1) * HEAD_DIM)
        s = _dot_nt(q_ref[:, sl], keys) * scale
        if bias is not None:
            s = s + bias
        p = jnp.exp(s - jnp.max(s, axis=-1, keepdims=True)).astype(BF16)
        o = _dot(p, jnp.concatenate([values, jnp.ones_like(values)], axis=1))
        gate = ga_ref[:, sl].astype(F32)
        o_ref[:, sl] = (o[:, :HEAD_DIM] / o[:, HEAD_DIM:] * gate).astype(BF16)

    @pl.when(b < nb_lat)
    def _():
        ks = jnp.clip(b * NA_ROWS - NA_WIN_ROWS // 2, 0, rows - NA_SLAB)
        start = pl.multiple_of(ks * GRID_W, GRID_W)
        for h in range(NA_HP):
            sl = slice(h * HEAD_DIM, (h + 1) * HEAD_DIM)
            keys = jnp.concatenate([k_ref[pl.ds(start, NA_SLAB * GRID_W), sl], k_ref[n_lat:n_lat + nc, sl]], axis=0)
            values = jnp.concatenate([v_ref[pl.ds(start, NA_SLAB * GRID_W), sl], v_ref[n_lat:n_lat + nc, sl]], axis=0)
            attend(h, keys, values, bias_ref[h])

    @pl.when(b >= nb_lat)
    def _():
        for h in range(NA_HP):
            sl = slice(h * HEAD_DIM, (h + 1) * HEAD_DIM)
            attend(h, k_ref[n_lat:n_lat + nc, sl], v_ref[n_lat:n_lat + nc, sl], None)


def _na_attention(qkv, ga, bias_tab, n_lat):
    nt = qkv.shape[0]
    nc = nt - n_lat
    rows = n_lat // GRID_W
    assert nc == NA_TQ and rows % NA_ROWS == 0 and rows >= NA_SLAB
    nb = nt // NA_TQ
    nb_lat = rows // NA_ROWS
    n_keys = NA_SLAB * GRID_W + nc
    hw = NA_HP * HEAD_DIM
    groups = NA_HEADS // NA_HP

    def pat(b):
        return jnp.where(b == 0, 0, jnp.where(b >= nb_lat - 1, 2, 1))

    kern = functools.partial(_na_kernel, n_lat=n_lat, nc=nc, rows=rows)
    return pl.pallas_call(
        kern,
        grid=(groups, nb),
        in_specs=[
            pl.BlockSpec((NA_TQ, hw), lambda h, b: (b, h)),
            pl.BlockSpec((nt, hw), lambda h, b: (0, groups + h)),
            pl.BlockSpec((nt, hw), lambda h, b: (0, 2 * groups + h)),
            pl.BlockSpec((None, NA_HP, NA_TQ, n_keys), lambda h, b: (pat(b), h, 0, 0)),
            pl.BlockSpec((NA_TQ, hw), lambda h, b: (b, h)),
        ],
        out_specs=pl.BlockSpec((NA_TQ, hw), lambda h, b: (b, h)),
        out_shape=jax.ShapeDtypeStruct((nt, NA_HEADS * HEAD_DIM), BF16),
        compiler_params=_cparams(2),
        name="na_attention",
    )(qkv, qkv, qkv, bias_tab, ga)


def _conv_kernel(um_ref, up_ref, un_ref, gb_ref, w_ref, b_ref, lg_ref, lb_ref, o_ref, ubuf, shifted, *, n_lat, nt):
    t = pl.program_id(0)
    start = t * CONV_T
    prev_ok = jnp.logical_and(start != 0, start != n_lat)
    next_ok = jnp.logical_and(start + CONV_T != n_lat, start + CONV_T != nt)
    ubuf[0:CONV_HALO, :] = jnp.where(prev_ok, up_ref[...], 0.0)
    ubuf[CONV_HALO:CONV_HALO + CONV_T, :] = um_ref[...]
    ubuf[CONV_HALO + CONV_T:, :] = jnp.where(next_ok, un_ref[...], 0.0)
    span = CONV_T + 2 * CONV_HALO - 8
    for b in range(1, 8):
        shifted[b - 1, :, :] = ubuf[b:b + span, :]
    off = CONV_HALO - CONV_KERNEL // 2
    for r0 in range(0, CONV_T, CONV_CH):
        acc = jnp.zeros((CONV_CH, um_ref.shape[-1]), F32) + b_ref[...]
        for j in range(CONV_KERNEL):
            a, b = divmod(off + j, 8)
            rows = slice(r0 + 8 * a, r0 + 8 * a + CONV_CH)
            tap = ubuf[rows, :] if b == 0 else shifted[b - 1, rows, :]
            acc = acc + w_ref[j:j + 1, :] * tap
        mu = jnp.mean(acc, axis=-1, keepdims=True)
        xc = acc - mu
        var = jnp.mean(xc * xc, axis=-1, keepdims=True)
        y = xc * lax.rsqrt(var + NORM_EPS) * lg_ref[...] + lb_ref[...]
        y = _silu(y) * gb_ref[r0:r0 + CONV_CH, :].astype(F32)
        o_ref[r0:r0 + CONV_CH, :] = y.astype(BF16)


def _conformer_conv(u, gb, dw_w, dw_b, ln_g, ln_b, n_lat):
    nt, cw = u.shape
    assert nt % CONV_T == 0 and n_lat % CONV_T == 0
    hb = CONV_T // CONV_HALO
    n_halo_blocks = nt // CONV_HALO
    w_pad = jnp.zeros((32, cw), F32).at[:CONV_KERNEL].set(dw_w)
    kern = functools.partial(_conv_kernel, n_lat=n_lat, nt=nt)
    vec = lambda: pl.BlockSpec((1, cw), lambda t: (0, 0))
    return pl.pallas_call(
        kern,
        grid=(nt // CONV_T,),
        in_specs=[
            pl.BlockSpec((CONV_T, cw), lambda t: (t, 0)),
            pl.BlockSpec((CONV_HALO, cw), lambda t: (jnp.maximum(t * hb - 1, 0), 0)),
            pl.BlockSpec((CONV_HALO, cw), lambda t: (jnp.minimum((t + 1) * hb, n_halo_blocks - 1), 0)),
            pl.BlockSpec((CONV_T, cw), lambda t: (t, 0)),
            pl.BlockSpec((32, cw), lambda t: (0, 0)),
            vec(), vec(), vec(),
        ],
        out_specs=pl.BlockSpec((CONV_T, cw), lambda t: (t, 0)),
        out_shape=jax.ShapeDtypeStruct((nt, cw), BF16),
        scratch_shapes=[pltpu.VMEM((CONV_T + 2 * CONV_HALO, cw), F32),
                        pltpu.VMEM((7, CONV_T + 2 * CONV_HALO - 8, cw), F32)],
        compiler_params=_cparams(1),
        name="conformer_conv",
    )(u, u, u, gb, w_pad, dw_b.reshape(1, cw), ln_g.reshape(1, cw), ln_b.reshape(1, cw))


def _out_proj_kernel(a_ref, b_ref, w32_ref, x_ref, c_ref, mod_ref, fg_ref, o_ref, w_ref, *, n_lat, tm, final):
    i = pl.program_id(0)

    @pl.when(i == 0)
    def _():
        w_ref[...] = w32_ref[...].astype(BF16)

    half = a_ref.shape[-1]
    y = _dot(a_ref[...], w_ref[0:half, :]) + _dot(b_ref[...], w_ref[half:, :])
    d = y.shape[-1]
    row = (i * tm >= n_lat).astype(jnp.int32)
    gt = mod_ref[pl.ds(row, 1), 2 * d:3 * d]
    xn = _stream_tile(x_ref, c_ref, i, tm, n_lat) + gt * y
    if final:
        ms = jnp.mean(xn * xn, axis=-1, keepdims=True)
        xn = xn * lax.rsqrt(ms + NORM_EPS) * fg_ref[...]
    o_ref[...] = xn


def _out_proj(mix_a, mix_b, cols, w, layer, stream, nt, mod, fg, n_lat, final):
    d = stream[0].shape[1]
    tm = TM_OUT
    half = w.shape[1] // 2
    out_rows = n_lat if final else nt
    assert out_rows % tm == 0
    kern = functools.partial(_out_proj_kernel, n_lat=n_lat, tm=tm, final=final)
    return pl.pallas_call(
        kern,
        grid=(out_rows // tm,),
        in_specs=[
            pl.BlockSpec((tm, half), lambda i: (i, cols[0])),
            pl.BlockSpec((tm, half), lambda i: (i, cols[1])),
            pl.BlockSpec((None, 2 * half, d), lambda i: (layer, 0, 0), pipeline_mode=pl.Buffered(1)),
        ] + _stream_specs(*stream, tm, n_lat) + [
            pl.BlockSpec((8, 3 * d), lambda i: (0, 0)),
            pl.BlockSpec((1, d), lambda i: (0, 0)),
        ],
        out_specs=pl.BlockSpec((tm, d), lambda i: (i, 0)),
        out_shape=jax.ShapeDtypeStruct((out_rows, d), F32),
        scratch_shapes=[pltpu.VMEM((2 * half, d), BF16)],
        compiler_params=_cparams(1),
        name="out_proj_final" if final else "out_proj",
    )(mix_a, mix_b, w, stream[0], stream[1], mod, fg)


def _rope_tables(n_lat, nc):
    rows = n_lat // GRID_W
    n_freq = HEAD_DIM // 4
    inv = ROPE_THETA ** (-jnp.arange(n_freq, dtype=F32) / n_freq)
    row_ang = jnp.arange(rows, dtype=jnp.int32).astype(F32)[:, None] * inv
    col_ang = jnp.arange(GRID_W, dtype=jnp.int32).astype(F32)[:, None] * inv
    per_row = lambda a: jnp.repeat(a, GRID_W, axis=0)
    per_col = lambda a: jnp.tile(a, (rows, 1))
    cr, sr = per_row(jnp.cos(row_ang)), per_row(jnp.sin(row_ang))
    cc, sc = per_col(jnp.cos(col_ang)), per_col(jnp.sin(col_ang))
    cos_t = jnp.concatenate([cr, cr, cc, cc], axis=-1)
    sin_t = jnp.concatenate([-sr, sr, -sc, sc], axis=-1)
    cos_t = jnp.concatenate([cos_t, jnp.ones((nc, HEAD_DIM), F32)], axis=0)
    sin_t = jnp.concatenate([sin_t, jnp.zeros((nc, HEAD_DIM), F32)], axis=0)
    return cos_t, sin_t


def _norm_rope(x, gain, cos_t, sin_t, first_half):
    ms = jnp.mean(x * x, axis=-1, keepdims=True)
    xn = x * lax.rsqrt(ms + NORM_EPS) * gain
    partner = jnp.where(first_half, pltpu.roll(xn, 3 * HEAD_DIM // 4, 1), pltpu.roll(xn, HEAD_DIM // 4, 1))
    return xn * cos_t + partner * sin_t


def _odd_proj_kernel(x_ref, c_ref, g_ref, mod_ref, w_hbm, wvt_ref, cos_ref, sin_ref, qn_ref, kn_ref,
                     q_ref, k_ref, vt_ref, gs_ref, w_ref, stage, sem, *, n_lat, tm, layer):
    @pl.when(pl.program_id(0) == 0)
    def _():
        _load_weights_bf16(w_hbm, layer, w_ref, stage, sem)

    h = _prenorm(x_ref, c_ref, g_ref, mod_ref, pl.program_id(0), tm, n_lat)
    tn = k_ref.shape[-1]
    cos_t = cos_ref[...]
    sin_t = sin_ref[...]
    lane = lax.broadcasted_iota(jnp.int32, (1, HEAD_DIM), 1)
    first_half = (lane % (HEAD_DIM // 2)) < (HEAD_DIM // 4)

    def proj(seg):
        return _dot(h, w_ref[:, seg * tn:(seg + 1) * tn])

    def rotated(seg, gain, out_ref, col0):
        acc = proj(seg)
        for hh in range(tn // HEAD_DIM):
            y = _norm_rope(acc[:, hh * HEAD_DIM:(hh + 1) * HEAD_DIM], gain, cos_t, sin_t, first_half)
            out_ref[:, col0 + hh * HEAD_DIM:col0 + (hh + 1) * HEAD_DIM] = y.astype(BF16)

    n_q = q_ref.shape[-1] // tn
    q_gain = qn_ref[...] * ((HEAD_DIM ** -0.5) * LOG2E)
    for seg in range(n_q):
        rotated(seg, q_gain, q_ref, seg * tn)
    rotated(n_q, kn_ref[...], k_ref, 0)
    v_t = _dot_nt(wvt_ref[...], h)
    vt_ref[...] = v_t.reshape(vt_ref.shape).astype(BF16)
    for seg in range(n_q):
        gs_ref[:, seg * tn:(seg + 1) * tn] = _silu(proj(n_q + 2 + seg)).astype(BF16)


def _odd_proj(stream, nt, g, mod, w, w_vt, layer, cos_t, sin_t, q_norm, k_norm, n_lat):
    d = stream[0].shape[1]
    tm = TM_PROJ
    qw = GQA_HEADS * HEAD_DIM
    kw = GQA_KV_HEADS * HEAD_DIM
    assert w.shape[2] == 2 * qw + 2 * kw and nt % tm == 0 and n_lat % tm == 0
    kern = functools.partial(_odd_proj_kernel, n_lat=n_lat, tm=tm, layer=layer)
    rows = lambda width: pl.BlockSpec((tm, width), lambda i: (i, 0))
    vec = lambda width: pl.BlockSpec((1, width), lambda i: (0, 0))
    return pl.pallas_call(
        kern,
        grid=(nt // tm,),
        in_specs=_stream_specs(*stream, tm, n_lat) + [
            vec(d),
            pl.BlockSpec((8, 3 * d), lambda i: (0, 0)),
            pl.BlockSpec(memory_space=pl.ANY),
            pl.BlockSpec((kw, d), lambda i: (0, 0)),
            rows(HEAD_DIM), rows(HEAD_DIM), vec(HEAD_DIM), vec(HEAD_DIM),
        ],
        out_specs=[rows(qw), rows(kw),
                   pl.BlockSpec((GQA_KV_HEADS, None, HEAD_DIM, tm), lambda i: (0, i, 0, 0)),
                   rows(qw)],
        out_shape=[
            jax.ShapeDtypeStruct((nt, qw), BF16),
            jax.ShapeDtypeStruct((nt, kw), BF16),
            jax.ShapeDtypeStruct((GQA_KV_HEADS, nt // tm, HEAD_DIM, tm), BF16),
            jax.ShapeDtypeStruct((nt, qw), BF16),
        ],
        scratch_shapes=_weight_scratch(w),
        compiler_params=_cparams(1),
        name="odd_proj",
    )(stream[0], stream[1], g, mod, w, w_vt, cos_t, sin_t,
      q_norm.reshape(1, HEAD_DIM), k_norm.reshape(1, HEAD_DIM))


def _gqa_kernel(bound_ref, q_ref, k_ref, vt_ref, g_ref, o_ref, *, n_lat, nc):
    b = pl.program_id(1)
    tq = q_ref.shape[0]
    vb = vt_ref.shape[-1]
    bound = bound_ref[0]
    is_lat = b * tq < n_lat
    q = jnp.concatenate([q_ref[:, h * HEAD_DIM:(h + 1) * HEAD_DIM] for h in range(GQA_GROUP)], axis=0)

    def vt_chunk(first_block, n_blocks):
        return jnp.concatenate([vt_ref[first_block + j] for j in range(n_blocks)], axis=1)

    def write_out(o_t):
        o = o_t.T
        for h in range(GQA_GROUP):
            sl = slice(h * HEAD_DIM, (h + 1) * HEAD_DIM)
            o_ref[:, sl] = (o[h * tq:(h + 1) * tq, :] * g_ref[:, sl].astype(F32)).astype(BF16)

    def fixed_shift_chunk(row0, n_rows):
        p = jnp.exp2(_dot_nt(k_ref[row0:row0 + n_rows, :], q) - bound).astype(BF16)
        vt = vt_chunk(row0 // vb, n_rows // vb)
        vt = jnp.concatenate([vt, jnp.ones((GQA_ONES_ROWS, n_rows), BF16)], axis=0)
        return _dot(vt, p)

    def finish(acc):
        write_out(acc[:HEAD_DIM, :] / acc[HEAD_DIM:HEAD_DIM + 1, :])

    small = bound <= GQA_SAFE_BOUND

    @pl.when(jnp.logical_and(small, is_lat))
    def _():
        acc = fixed_shift_chunk(n_lat, nc)
        for c in range(n_lat // GQA_TK):
            acc = acc + fixed_shift_chunk(c * GQA_TK, GQA_TK)
        finish(acc)

    @pl.when(jnp.logical_and(small, jnp.logical_not(is_lat)))
    def _():
        finish(fixed_shift_chunk(n_lat, nc))

    @pl.when(bound > GQA_SAFE_BOUND)
    def _():
        cols = GQA_GROUP * tq
        per = GQA_TK_ONLINE // vb

        def step(carry, k_chunk, vt):
            m, l, acc = carry
            s = _dot_nt(k_chunk, q)
            m_new = jnp.maximum(m, jnp.max(s, axis=0, keepdims=True))
            alpha = jnp.exp2(m - m_new)
            p = jnp.exp2(s - m_new)
            l = alpha * l + jnp.sum(p, axis=0, keepdims=True)
            acc = alpha * acc + _dot(vt, p.astype(BF16))
            return m_new, l, acc

        init = (jnp.full((1, cols), NEG_BIG, F32), jnp.zeros((1, cols), F32), jnp.zeros((HEAD_DIM, cols), F32))
        carry = step(init, k_ref[n_lat:n_lat + nc, :], vt_chunk(n_lat // vb, nc // vb))

        def body(c, carry):
            start = pl.multiple_of(c * GQA_TK_ONLINE, GQA_TK_ONLINE)
            vt = jnp.concatenate([vt_ref[c * per + j] for j in range(per)], axis=1)
            return step(carry, k_ref[pl.ds(start, GQA_TK_ONLINE), :], vt)

        m, l, acc = lax.fori_loop(0, jnp.where(is_lat, n_lat // GQA_TK_ONLINE, 0), body, carry)
        write_out(acc / l)


def _gqa_attention(q, k, vt, gs, q_norm, k_norm, n_lat):
    nt = q.shape[0]
    nc = nt - n_lat
    tq = GQA_TQ
    gw = GQA_GROUP * HEAD_DIM
    n_vb, vb = vt.shape[1], vt.shape[3]
    assert nt % tq == 0 and n_lat % tq == 0 and n_lat % GQA_TK == 0 and n_lat % GQA_TK_ONLINE == 0
    assert n_vb * vb == nt and n_lat % vb == 0 and nc % vb == 0 and GQA_TK_ONLINE % vb == 0
    bound = (1.01 * HEAD_DIM * (HEAD_DIM ** -0.5) * LOG2E) * jnp.max(jnp.abs(q_norm)) * jnp.max(jnp.abs(k_norm))
    kern = functools.partial(_gqa_kernel, n_lat=n_lat, nc=nc)
    return pl.pallas_call(
        kern,
        grid=(GQA_KV_HEADS, nt // tq),
        in_specs=[
            pl.BlockSpec(memory_space=pltpu.SMEM),
            pl.BlockSpec((tq, gw), lambda h, b: (b, h)),
            pl.BlockSpec((nt, HEAD_DIM), lambda h, b: (0, h)),
            pl.BlockSpec((None, n_vb, HEAD_DIM, vb), lambda h, b: (h, 0, 0, 0)),
            pl.BlockSpec((tq, gw), lambda h, b: (b, h)),
        ],
        out_specs=pl.BlockSpec((tq, gw), lambda h, b: (b, h)),
        out_shape=jax.ShapeDtypeStruct((nt, GQA_HEADS * HEAD_DIM), BF16),
        compiler_params=_cparams(2),
        name="gqa_attention",
    )(bound.reshape(1).astype(F32), q, k, vt, gs)


def kernel(x, c, ctx, c_ctx, norm_g, w_mod, b_mod, e_w_in, e_rpb, e_dw_w, e_dw_b, e_ln_g, e_ln_b, e_w_out,
           o_w_in, o_q_norm, o_k_norm, o_w_out, final_norm_g):
    batch, n_lat, d = x.shape
    nc = ctx.shape[1]
    depth = norm_g.shape[0]
    assert batch == 1
    rows = n_lat // GRID_W

    nt = n_lat + nc
    stream = (x[0], ctx[0], 0)
    cvec = jnp.zeros((8, d), F32).at[0].set(c[0]).at[1].set(c_ctx)
    mods = _modulation(cvec, w_mod, b_mod)
    cos_t, sin_t = _rope_tables(n_lat, nc)
    fg = final_norm_g.reshape(1, d)

    for l in range(depth):
        i = l // 2
        g = norm_g[l].reshape(1, d)
        mod = mods[l]
        final = l == depth - 1
        if l % 2 == 0:
            qkv, ga, u, gb = _even_proj(stream, nt, g, mod, e_w_in, i, n_lat)
            mix_a = _na_attention(qkv, ga, _na_bias_table(e_rpb[i], rows, nc), n_lat)
            mix_b = _conformer_conv(u, gb, e_dw_w[i], e_dw_b[i], e_ln_g[i], e_ln_b[i], n_lat)
            xs = _out_proj(mix_a, mix_b, (0, 0), e_w_out, i, stream, nt, mod, fg, n_lat, final)
        else:
            v0 = (GQA_HEADS + GQA_KV_HEADS) * HEAD_DIM
            w_v = lax.optimization_barrier(o_w_in[i, :, v0:v0 + GQA_KV_HEADS * HEAD_DIM])
            w_vt = w_v.T.astype(BF16)
            q, k, vt, gs = _odd_proj(stream, nt, g, mod, o_w_in, w_vt, i, cos_t, sin_t,
                                     o_q_norm[i], o_k_norm[i], n_lat)
            mix = _gqa_attention(q, k, vt, gs, o_q_norm[i], o_k_norm[i], n_lat)
            xs = _out_proj(mix, mix, (0, 1), o_w_out, i, stream, nt, mod, fg, n_lat, final)
        stream = (xs, xs, n_lat)
    return xs[None]
```

```python
import functools

import numpy as np
import jax
import jax.numpy as jnp
from jax import lax
from jax.experimental import pallas as pl
from jax.experimental.pallas import tpu as pltpu

F32 = jnp.float32
BF16 = jnp.bfloat16

GRID_W = 64
HEAD_DIM = 128
NORM_EPS = 1e-6
NA_HEADS = 8
NA_WIN_ROWS = 8
NA_WIN_COLS = 16
CONV_KERNEL = 31
GQA_HEADS = 16
GQA_KV_HEADS = 4
GQA_GROUP = GQA_HEADS // GQA_KV_HEADS
ROPE_THETA = 10000.0
LOG2E = 1.4426950408889634

LANES = 128
V7X_VMEM_BYTES = 64 * 1024 * 1024
VMEM_LIMIT = V7X_VMEM_BYTES - 8 * 1024 * 1024

TM_PROJ = 256
TM_OUT = 256
NA_ROWS = 4
NA_SLAB = NA_WIN_ROWS + NA_ROWS
NA_TQ = NA_ROWS * GRID_W
NA_HP = 4
CONV_T = 256
CONV_HALO = 16
CONV_CH = 32
GQA_TQ = 256
GQA_TK = 2048
GQA_TK_ONLINE = 512
GQA_SAFE_BOUND = 60.0
GQA_ONES_ROWS = 16
NEG_BIG = -1e30


def _cparams(n_axes):
    return pltpu.CompilerParams(dimension_semantics=("arbitrary",) * n_axes, vmem_limit_bytes=VMEM_LIMIT)


def _sigmoid(x):
    return 1.0 / (1.0 + jnp.exp(-x))


def _silu(x):
    return x * _sigmoid(x)


def _dot(a, b):
    return jnp.dot(a, b, preferred_element_type=F32)


def _dot_nt(a, b):
    return lax.dot_general(a, b, (((1,), (1,)), ((), ())), preferred_element_type=F32)


def _stream_specs(lat_src, ctx_src, ctx_row0, tm, n_lat):
    d = lat_src.shape[1]
    nlt = n_lat // tm
    n_ctx_tiles = (ctx_src.shape[0] - ctx_row0) // tm
    assert n_lat % tm == 0 and ctx_row0 % tm == 0 and n_ctx_tiles >= 1
    ctx_mode = dict(pipeline_mode=pl.Buffered(1)) if n_ctx_tiles == 1 else {}
    return [pl.BlockSpec((tm, d), lambda i: (jnp.minimum(i, nlt - 1), 0)),
            pl.BlockSpec((tm, d), lambda i: (ctx_row0 // tm + jnp.clip(i - nlt, 0, n_ctx_tiles - 1), 0), **ctx_mode)]


def _stream_tile(x_ref, c_ref, tile_idx, tm, n_lat):
    return jnp.where(tile_idx * tm >= n_lat, c_ref[...], x_ref[...])


def _prenorm(x_ref, c_ref, g_ref, mod_ref, tile_idx, tm, n_lat):
    d = x_ref.shape[-1]
    x = _stream_tile(x_ref, c_ref, tile_idx, tm, n_lat)
    ms = jnp.mean(x * x, axis=-1, keepdims=True)
    row = (tile_idx * tm >= n_lat).astype(jnp.int32)
    mod = mod_ref[pl.ds(row, 1), :]
    a = g_ref[...] * (1.0 + mod[:, d:2 * d])
    return (x * lax.rsqrt(ms + NORM_EPS) * a + mod[:, 0:d]).astype(BF16)


W_STAGE_ROWS = 64
W_STAGE_SLOTS = 4


def _load_weights_bf16(w_hbm, layer, w_bf, stage, sem):
    slots, rows = stage.shape[0], stage.shape[1]
    n_chunks = w_bf.shape[0] // rows

    def copy(c):
        return pltpu.make_async_copy(w_hbm.at[layer, pl.ds(c * rows, rows), :],
                                     stage.at[c % slots], sem.at[c % slots])

    for c in range(min(slots - 1, n_chunks)):
        copy(c).start()
    for c in range(n_chunks):
        copy(c).wait()
        if c + slots - 1 < n_chunks:
            copy(c + slots - 1).start()
        w_bf[c * rows:(c + 1) * rows, :] = stage[c % slots].astype(BF16)


def _weight_scratch(w):
    d, n = w.shape[1:]
    assert d % W_STAGE_ROWS == 0
    return [pltpu.VMEM((d, n), BF16), pltpu.VMEM((W_STAGE_SLOTS, W_STAGE_ROWS, n), F32),
            pltpu.SemaphoreType.DMA((W_STAGE_SLOTS,))]


def _mod_kernel(c_ref, w_ref, b_ref, o_ref):
    s = _silu(c_ref[...]).astype(BF16)
    o_ref[...] = _dot(s, w_ref[...].astype(BF16)) + b_ref[...]


def _modulation(cvec, w_mod, b_mod):
    depth, d, d3 = w_mod.shape
    tn = 768
    return pl.pallas_call(
        _mod_kernel,
        grid=(depth, d3 // tn),
        in_specs=[
            pl.BlockSpec((8, d), lambda l, j: (0, 0)),
            pl.BlockSpec((None, d, tn), lambda l, j: (l, 0, j)),
            pl.BlockSpec((None, 1, tn), lambda l, j: (l, 0, j)),
        ],
        out_specs=pl.BlockSpec((None, 8, tn), lambda l, j: (l, 0, j)),
        out_shape=jax.ShapeDtypeStruct((depth, 8, d3), F32),
        compiler_params=_cparams(2),
        name="modulation",
    )(cvec, w_mod, b_mod.reshape(depth, 1, d3))


def _even_proj_kernel(x_ref, c_ref, g_ref, mod_ref, w_hbm, qkv_ref, ga_ref, u_ref, gb_ref, w_ref, stage, sem,
                      *, n_lat, tm, layer):
    @pl.when(pl.program_id(0) == 0)
    def _():
        _load_weights_bf16(w_hbm, layer, w_ref, stage, sem)

    h = _prenorm(x_ref, c_ref, g_ref, mod_ref, pl.program_id(0), tm, n_lat)
    cw = ga_ref.shape[-1]

    def proj(seg):
        return _dot(h, w_ref[:, seg * cw:(seg + 1) * cw])

    for seg in range(3):
        qkv_ref[:, seg * cw:(seg + 1) * cw] = proj(seg).astype(BF16)
    ga_ref[...] = _silu(proj(3)).astype(BF16)
    u_ref[...] = proj(4) * _sigmoid(proj(5))
    gb_ref[...] = _silu(proj(6)).astype(BF16)


def _even_proj(stream, nt, g, mod, w, layer, n_lat):
    d = stream[0].shape[1]
    tm = TM_PROJ
    cw = w.shape[2] // 7
    assert nt % tm == 0 and n_lat % tm == 0
    kern = functools.partial(_even_proj_kernel, n_lat=n_lat, tm=tm, layer=layer)
    rows = lambda width: pl.BlockSpec((tm, width), lambda i: (i, 0))
    return pl.pallas_call(
        kern,
        grid=(nt // tm,),
        in_specs=_stream_specs(*stream, tm, n_lat) + [
            pl.BlockSpec((1, d), lambda i: (0, 0)),
            pl.BlockSpec((8, 3 * d), lambda i: (0, 0)),
            pl.BlockSpec(memory_space=pl.ANY),
        ],
        out_specs=[rows(3 * cw), rows(cw), rows(cw), rows(cw)],
        out_shape=[
            jax.ShapeDtypeStruct((nt, 3 * cw), BF16),
            jax.ShapeDtypeStruct((nt, cw), BF16),
            jax.ShapeDtypeStruct((nt, cw), F32),
            jax.ShapeDtypeStruct((nt, cw), BF16),
        ],
        scratch_shapes=_weight_scratch(w),
        compiler_params=_cparams(1),
        name="even_proj",
    )(stream[0], stream[1], g, mod, w)


def _na_bias_kernel(rp_ref, o_ref, *, rows, nc):
    pattern = pl.program_id(0)
    lane = lax.broadcasted_iota(jnp.int32, (GRID_W, 2 * GRID_W), 1)
    qc = lax.broadcasted_iota(jnp.int32, (GRID_W, 2 * GRID_W), 0)
    kc = lane % GRID_W
    cs = jnp.clip(qc - NA_WIN_COLS // 2, 0, GRID_W - NA_WIN_COLS)
    col_ok = jnp.logical_and(kc >= cs, kc < cs + NA_WIN_COLS)
    left_half = lane < GRID_W
    masked = jnp.full((GRID_W, 2 * GRID_W), NEG_BIG, F32)

    def toeplitz(dr, right):
        x = jnp.broadcast_to(rp_ref[dr:dr + 1, :], (GRID_W, 2 * GRID_W))
        return pltpu.roll(x, 1 if right else GRID_W + 1, 1, stride=1, stride_axis=0)

    for pat_id, r0 in enumerate((0, 2 * NA_ROWS, rows - NA_ROWS)):
        @pl.when(pattern == pat_id)
        def _(r0=r0):
            ks = int(np.clip(r0 - NA_WIN_ROWS // 2, 0, rows - NA_SLAB))
            for r_local in range(NA_ROWS):
                r = r0 + r_local
                rs = int(np.clip(r - NA_WIN_ROWS // 2, 0, rows - NA_WIN_ROWS))
                for jp in range(NA_SLAB // 2):
                    halves = [toeplitz(kr - r + NA_WIN_ROWS - 1, side == 1) if rs <= kr < rs + NA_WIN_ROWS else None
                              for side, kr in enumerate((ks + 2 * jp, ks + 2 * jp + 1))]
                    if halves[0] is None and halves[1] is None:
                        tile = masked
                    else:
                        left = masked if halves[0] is None else halves[0]
                        right = masked if halves[1] is None else halves[1]
                        tile = jnp.where(col_ok, jnp.where(left_half, left, right) * LOG2E, NEG_BIG)
                    o_ref[r_local * GRID_W:(r_local + 1) * GRID_W, jp * 2 * GRID_W:(jp + 1) * 2 * GRID_W] = tile
    o_ref[:, NA_SLAB * GRID_W:] = jnp.zeros((NA_TQ, nc), F32)


def _na_bias_table(rpb, rows, nc):
    heads = rpb.shape[0]
    assert NA_SLAB % 2 == 0 and 2 * GRID_W == LANES and rpb.shape[1] == 2 * NA_WIN_ROWS - 1
    pad = GRID_W - NA_WIN_COLS
    rp = jnp.pad(rpb.astype(F32), ((0, 0), (0, 1), (pad, LANES - pad - rpb.shape[2])))
    n_keys = NA_SLAB * GRID_W + nc
    kern = functools.partial(_na_bias_kernel, rows=rows, nc=nc)
    return pl.pallas_call(
        kern,
        grid=(3, heads),
        in_specs=[pl.BlockSpec((None, 2 * NA_WIN_ROWS, LANES), lambda p, h: (h, 0, 0))],
        out_specs=pl.BlockSpec((None, None, NA_TQ, n_keys), lambda p, h: (p, h, 0, 0)),
        out_shape=jax.ShapeDtypeStruct((3, heads, NA_TQ, n_keys), F32),
        compiler_params=_cparams(2),
        name="na_bias",
    )(rp)


def _na_kernel(q_ref, k_ref, v_ref, bias_ref, ga_ref, o_ref, *, n_lat, nc, rows):
    b = pl.program_id(1)
    nb_lat = rows // NA_ROWS
    scale = (HEAD_DIM ** -0.5) * LOG2E

    def attend(h, keys, values, bias):
        sl = slice(h * HEAD_DIM, (h + 1) * HEAD_DIM)
        s = _dot_nt(q_ref[:, sl], keys) * scale
        if bias is not None:
            s = s + bias
        p = jnp.exp2(s - jnp.max(s, axis=-1, keepdims=True)).astype(BF16)
        o = _dot(p, jnp.concatenate([values, jnp.ones_like(values)], axis=1))
        gate = ga_ref[:, sl].astype(F32)
        o_ref[:, sl] = (o[:, :HEAD_DIM] / o[:, HEAD_DIM:] * gate).astype(BF16)

    @pl.when(b < nb_lat)
    def _():
        ks = jnp.clip(b * NA_ROWS - NA_WIN_ROWS // 2, 0, rows - NA_SLAB)
        start = pl.multiple_of(ks * GRID_W, GRID_W)
        for h in range(NA_HP):
            sl = slice(h * HEAD_DIM, (h + 1) * HEAD_DIM)
            keys = jnp.concatenate([k_ref[pl.ds(start, NA_SLAB * GRID_W), sl], k_ref[n_lat:n_lat + nc, sl]], axis=0)
            values = jnp.concatenate([v_ref[pl.ds(start, NA_SLAB * GRID_W), sl], v_ref[n_lat:n_lat + nc, sl]], axis=0)
            attend(h, keys, values, bias_ref[h])

    @pl.when(b >= nb_lat)
    def _():
        for h in range(NA_HP):
            sl = slice(h * HEAD_DIM, (h + 1) * HEAD_DIM)
            attend(h, k_ref[n_lat:n_lat + nc, sl], v_ref[n_lat:n_lat + nc, sl], None)


def _na_attention(qkv, ga, bias_tab, n_lat):
    nt = qkv.shape[0]
    nc = nt - n_lat
    rows = n_lat // GRID_W
    assert nc == NA_TQ and rows % NA_ROWS == 0 and rows >= NA_SLAB
    nb = nt // NA_TQ
    nb_lat = rows // NA_ROWS
    n_keys = NA_SLAB * GRID_W + nc
    hw = NA_HP * HEAD_DIM
    groups = NA_HEADS // NA_HP

    def pat(b):
        return jnp.where(b == 0, 0, jnp.where(b >= nb_lat - 1, 2, 1))

    kern = functools.partial(_na_kernel, n_lat=n_lat, nc=nc, rows=rows)
    return pl.pallas_call(
        kern,
        grid=(groups, nb),
        in_specs=[
            pl.BlockSpec((NA_TQ, hw), lambda h, b: (b, h)),
            pl.BlockSpec((nt, hw), lambda h, b: (0, groups + h)),
            pl.BlockSpec((nt, hw), lambda h, b: (0, 2 * groups + h)),
            pl.BlockSpec((None, NA_HP, NA_TQ, n_keys), lambda h, b: (pat(b), h, 0, 0)),
            pl.BlockSpec((NA_TQ, hw), lambda h, b: (b, h)),
        ],
        out_specs=pl.BlockSpec((NA_TQ, hw), lambda h, b: (b, h)),
        out_shape=jax.ShapeDtypeStruct((nt, NA_HEADS * HEAD_DIM), BF16),
        compiler_params=_cparams(2),
        name="na_attention",
    )(qkv, qkv, qkv, bias_tab, ga)


def _conv_kernel(um_ref, up_ref, un_ref, gb_ref, w_ref, b_ref, lg_ref, lb_ref, o_ref, ubuf, shifted, *, n_lat, nt):
    t = pl.program_id(0)
    start = t * CONV_T
    prev_ok = jnp.logical_and(start != 0, start != n_lat)
    next_ok = jnp.logical_and(start + CONV_T != n_lat, start + CONV_T != nt)
    ubuf[0:CONV_HALO, :] = jnp.where(prev_ok, up_ref[...], 0.0)
    ubuf[CONV_HALO:CONV_HALO + CONV_T, :] = um_ref[...]
    ubuf[CONV_HALO + CONV_T:, :] = jnp.where(next_ok, un_ref[...], 0.0)
    span = CONV_T + 2 * CONV_HALO - 8
    for b in range(1, 8):
        shifted[b - 1, :, :] = ubuf[b:b + span, :]
    off = CONV_HALO - CONV_KERNEL // 2
    for r0 in range(0, CONV_T, CONV_CH):
        acc = jnp.zeros((CONV_CH, um_ref.shape[-1]), F32) + b_ref[...]
        for j in range(CONV_KERNEL):
            a, b = divmod(off + j, 8)
            rows = slice(r0 + 8 * a, r0 + 8 * a + CONV_CH)
            tap = ubuf[rows, :] if b == 0 else shifted[b - 1, rows, :]
            acc = acc + w_ref[j:j + 1, :] * tap
        mu = jnp.mean(acc, axis=-1, keepdims=True)
        xc = acc - mu
        var = jnp.mean(xc * xc, axis=-1, keepdims=True)
        y = xc * lax.rsqrt(var + NORM_EPS) * lg_ref[...] + lb_ref[...]
        y = _silu(y) * gb_ref[r0:r0 + CONV_CH, :].astype(F32)
        o_ref[r0:r0 + CONV_CH, :] = y.astype(BF16)


def _conformer_conv(u, gb, dw_w, dw_b, ln_g, ln_b, n_lat):
    nt, cw = u.shape
    assert nt % CONV_T == 0 and n_lat % CONV_T == 0
    hb = CONV_T // CONV_HALO
    n_halo_blocks = nt // CONV_HALO
    w_pad = jnp.zeros((32, cw), F32).at[:CONV_KERNEL].set(dw_w)
    kern = functools.partial(_conv_kernel, n_lat=n_lat, nt=nt)
    vec = lambda: pl.BlockSpec((1, cw), lambda t: (0, 0))
    return pl.pallas_call(
        kern,
        grid=(nt // CONV_T,),
        in_specs=[
            pl.BlockSpec((CONV_T, cw), lambda t: (t, 0)),
            pl.BlockSpec((CONV_HALO, cw), lambda t: (jnp.maximum(t * hb - 1, 0), 0)),
            pl.BlockSpec((CONV_HALO, cw), lambda t: (jnp.minimum((t + 1) * hb, n_halo_blocks - 1), 0)),
            pl.BlockSpec((CONV_T, cw), lambda t: (t, 0)),
            pl.BlockSpec((32, cw), lambda t: (0, 0)),
            vec(), vec(), vec(),
        ],
        out_specs=pl.BlockSpec((CONV_T, cw), lambda t: (t, 0)),
        out_shape=jax.ShapeDtypeStruct((nt, cw), BF16),
        scratch_shapes=[pltpu.VMEM((CONV_T + 2 * CONV_HALO, cw), F32),
                        pltpu.VMEM((7, CONV_T + 2 * CONV_HALO - 8, cw), F32)],
        compiler_params=_cparams(1),
        name="conformer_conv",
    )(u, u, u, gb, w_pad, dw_b.reshape(1, cw), ln_g.reshape(1, cw), ln_b.reshape(1, cw))


def _out_proj_kernel(a_ref, b_ref, w32_ref, x_ref, c_ref, mod_ref, fg_ref, o_ref, w_ref, *, n_lat, tm, final):
    i = pl.program_id(0)

    @pl.when(i == 0)
    def _():
        w_ref[...] = w32_ref[...].astype(BF16)

    half = a_ref.shape[-1]
    y = _dot(a_ref[...], w_ref[0:half, :]) + _dot(b_ref[...], w_ref[half:, :])
    d = y.shape[-1]
    row = (i * tm >= n_lat).astype(jnp.int32)
    gt = mod_ref[pl.ds(row, 1), 2 * d:3 * d]
    xn = _stream_tile(x_ref, c_ref, i, tm, n_lat) + gt * y
    if final:
        ms = jnp.mean(xn * xn, axis=-1, keepdims=True)
        xn = xn * lax.rsqrt(ms + NORM_EPS) * fg_ref[...]
    o_ref[...] = xn


def _out_proj(mix_a, mix_b, cols, w, layer, stream, nt, mod, fg, n_lat, final):
    d = stream[0].shape[1]
    tm = TM_OUT
    half = w.shape[1] // 2
    out_rows = n_lat if final else nt
    assert out_rows % tm == 0
    kern = functools.partial(_out_proj_kernel, n_lat=n_lat, tm=tm, final=final)
    return pl.pallas_call(
        kern,
        grid=(out_rows // tm,),
        in_specs=[
            pl.BlockSpec((tm, half), lambda i: (i, cols[0])),
            pl.BlockSpec((tm, half), lambda i: (i, cols[1])),
            pl.BlockSpec((None, 2 * half, d), lambda i: (layer, 0, 0), pipeline_mode=pl.Buffered(1)),
        ] + _stream_specs(*stream, tm, n_lat) + [
            pl.BlockSpec((8, 3 * d), lambda i: (0, 0)),
            pl.BlockSpec((1, d), lambda i: (0, 0)),
        ],
        out_specs=pl.BlockSpec((tm, d), lambda i: (i, 0)),
        out_shape=jax.ShapeDtypeStruct((out_rows, d), F32),
        scratch_shapes=[pltpu.VMEM((2 * half, d), BF16)],
        compiler_params=_cparams(1),
        name="out_proj_final" if final else "out_proj",
    )(mix_a, mix_b, w, stream[0], stream[1], mod, fg)


def _rope_tables(n_lat, nc):
    rows = n_lat // GRID_W
    n_freq = HEAD_DIM // 4
    inv = ROPE_THETA ** (-jnp.arange(n_freq, dtype=F32) / n_freq)
    row_ang = jnp.arange(rows, dtype=jnp.int32).astype(F32)[:, None] * inv
    col_ang = jnp.arange(GRID_W, dtype=jnp.int32).astype(F32)[:, None] * inv
    per_row = lambda a: jnp.repeat(a, GRID_W, axis=0)
    per_col = lambda a: jnp.tile(a, (rows, 1))
    cr, sr = per_row(jnp.cos(row_ang)), per_row(jnp.sin(row_ang))
    cc, sc = per_col(jnp.cos(col_ang)), per_col(jnp.sin(col_ang))
    cos_t = jnp.concatenate([cr, cr, cc, cc], axis=-1)
    sin_t = jnp.concatenate([-sr, sr, -sc, sc], axis=-1)
    cos_t = jnp.concatenate([cos_t, jnp.ones((nc, HEAD_DIM), F32)], axis=0)
    sin_t = jnp.concatenate([sin_t, jnp.zeros((nc, HEAD_DIM), F32)], axis=0)
    return cos_t, sin_t


def _norm_rope(x, gain, cos_t, sin_t, first_half):
    ms = jnp.mean(x * x, axis=-1, keepdims=True)
    xn = x * lax.rsqrt(ms + NORM_EPS) * gain
    partner = jnp.where(first_half, pltpu.roll(xn, 3 * HEAD_DIM // 4, 1), pltpu.roll(xn, HEAD_DIM // 4, 1))
    return xn * cos_t + partner * sin_t


def _odd_proj_kernel(x_ref, c_ref, g_ref, mod_ref, w_hbm, wvt_ref, cos_ref, sin_ref, qn_ref, kn_ref,
                     q_ref, k_ref, vt_ref, gs_ref, w_ref, stage, sem, *, n_lat, tm, layer):
    @pl.when(pl.program_id(0) == 0)
    def _():
        _load_weights_bf16(w_hbm, layer, w_ref, stage, sem)

    h = _prenorm(x_ref, c_ref, g_ref, mod_ref, pl.program_id(0), tm, n_lat)
    tn = k_ref.shape[-1]
    cos_t = cos_ref[...]
    sin_t = sin_ref[...]
    lane = lax.broadcasted_iota(jnp.int32, (1, HEAD_DIM), 1)
    first_half = (lane % (HEAD_DIM // 2)) < (HEAD_DIM // 4)

    def proj(seg):
        return _dot(h, w_ref[:, seg * tn:(seg + 1) * tn])

    def rotated(seg, gain, out_ref, col0):
        acc = proj(seg)
        for hh in range(tn // HEAD_DIM):
            y = _norm_rope(acc[:, hh * HEAD_DIM:(hh + 1) * HEAD_DIM], gain, cos_t, sin_t, first_half)
            out_ref[:, col0 + hh * HEAD_DIM:col0 + (hh + 1) * HEAD_DIM] = y.astype(BF16)

    n_q = q_ref.shape[-1] // tn
    q_gain = qn_ref[...] * ((HEAD_DIM ** -0.5) * LOG2E)
    for seg in range(n_q):
        rotated(seg, q_gain, q_ref, seg * tn)
    rotated(n_q, kn_ref[...], k_ref, 0)
    v_t = _dot_nt(wvt_ref[...], h)
    vt_ref[...] = v_t.reshape(vt_ref.shape).astype(BF16)
    for seg in range(n_q):
        gs_ref[:, seg * tn:(seg + 1) * tn] = _silu(proj(n_q + 2 + seg)).astype(BF16)


def _odd_proj(stream, nt, g, mod, w, w_vt, layer, cos_t, sin_t, q_norm, k_norm, n_lat):
    d = stream[0].shape[1]
    tm = TM_PROJ
    qw = GQA_HEADS * HEAD_DIM
    kw = GQA_KV_HEADS * HEAD_DIM
    assert w.shape[2] == 2 * qw + 2 * kw and nt % tm == 0 and n_lat % tm == 0
    kern = functools.partial(_odd_proj_kernel, n_lat=n_lat, tm=tm, layer=layer)
    rows = lambda width: pl.BlockSpec((tm, width), lambda i: (i, 0))
    vec = lambda width: pl.BlockSpec((1, width), lambda i: (0, 0))
    return pl.pallas_call(
        kern,
        grid=(nt // tm,),
        in_specs=_stream_specs(*stream, tm, n_lat) + [
            vec(d),
            pl.BlockSpec((8, 3 * d), lambda i: (0, 0)),
            pl.BlockSpec(memory_space=pl.ANY),
            pl.BlockSpec((kw, d), lambda i: (0, 0)),
            rows(HEAD_DIM), rows(HEAD_DIM), vec(HEAD_DIM), vec(HEAD_DIM),
        ],
        out_specs=[rows(qw), rows(kw),
                   pl.BlockSpec((GQA_KV_HEADS, None, HEAD_DIM, tm), lambda i: (0, i, 0, 0)),
                   rows(qw)],
        out_shape=[
            jax.ShapeDtypeStruct((nt, qw), BF16),
            jax.ShapeDtypeStruct((nt, kw), BF16),
            jax.ShapeDtypeStruct((GQA_KV_HEADS, nt // tm, HEAD_DIM, tm), BF16),
            jax.ShapeDtypeStruct((nt, qw), BF16),
        ],
        scratch_shapes=_weight_scratch(w),
        compiler_params=_cparams(1),
        name="odd_proj",
    )(stream[0], stream[1], g, mod, w, w_vt, cos_t, sin_t,
      q_norm.reshape(1, HEAD_DIM), k_norm.reshape(1, HEAD_DIM))


def _gqa_kernel(bound_ref, q_ref, k_ref, vt_ref, g_ref, o_ref, *, n_lat, nc):
    b = pl.program_id(1)
    tq = q_ref.shape[0]
    vb = vt_ref.shape[-1]
    bound = bound_ref[0]
    is_lat = b * tq < n_lat
    q = jnp.concatenate([q_ref[:, h * HEAD_DIM:(h + 1) * HEAD_DIM] for h in range(GQA_GROUP)], axis=0)

    def vt_chunk(first_block, n_blocks):
        return jnp.concatenate([vt_ref[first_block + j] for j in range(n_blocks)], axis=1)

    def write_out(o_t):
        o = o_t.T
        for h in range(GQA_GROUP):
            sl = slice(h * HEAD_DIM, (h + 1) * HEAD_DIM)
            o_ref[:, sl] = (o[h * tq:(h + 1) * tq, :] * g_ref[:, sl].astype(F32)).astype(BF16)

    def fixed_shift_chunk(row0, n_rows):
        p = jnp.exp2(_dot_nt(k_ref[row0:row0 + n_rows, :], q) - bound).astype(BF16)
        vt = vt_chunk(row0 // vb, n_rows // vb)
        vt = jnp.concatenate([vt, jnp.ones((GQA_ONES_ROWS, n_rows), BF16)], axis=0)
        return _dot(vt, p)

    def finish(acc):
        write_out(acc[:HEAD_DIM, :] / acc[HEAD_DIM:HEAD_DIM + 1, :])

    small = bound <= GQA_SAFE_BOUND

    @pl.when(jnp.logical_and(small, is_lat))
    def _():
        acc = fixed_shift_chunk(n_lat, nc)
        for c in range(n_lat // GQA_TK):
            acc = acc + fixed_shift_chunk(c * GQA_TK, GQA_TK)
        finish(acc)

    @pl.when(jnp.logical_and(small, jnp.logical_not(is_lat)))
    def _():
        finish(fixed_shift_chunk(n_lat, nc))

    @pl.when(bound > GQA_SAFE_BOUND)
    def _():
        cols = GQA_GROUP * tq
        per = GQA_TK_ONLINE // vb

        def step(carry, k_chunk, vt):
            m, l, acc = carry
            s = _dot_nt(k_chunk, q)
            m_new = jnp.maximum(m, jnp.max(s, axis=0, keepdims=True))
            alpha = jnp.exp2(m - m_new)
            p = jnp.exp2(s - m_new)
            l = alpha * l + jnp.sum(p, axis=0, keepdims=True)
            acc = alpha * acc + _dot(vt, p.astype(BF16))
            return m_new, l, acc

        init = (jnp.full((1, cols), NEG_BIG, F32), jnp.zeros((1, cols), F32), jnp.zeros((HEAD_DIM, cols), F32))
        carry = step(init, k_ref[n_lat:n_lat + nc, :], vt_chunk(n_lat // vb, nc // vb))

        def body(c, carry):
            start = pl.multiple_of(c * GQA_TK_ONLINE, GQA_TK_ONLINE)
            vt = jnp.concatenate([vt_ref[c * per + j] for j in range(per)], axis=1)
            return step(carry, k_ref[pl.ds(start, GQA_TK_ONLINE), :], vt)

        m, l, acc = lax.fori_loop(0, jnp.where(is_lat, n_lat // GQA_TK_ONLINE, 0), body, carry)
        write_out(acc / l)


def _gqa_attention(q, k, vt, gs, q_norm, k_norm, n_lat):
    nt = q.shape[0]
    nc = nt - n_lat
    tq = GQA_TQ
    gw = GQA_GROUP * HEAD_DIM
    n_vb, vb = vt.shape[1], vt.shape[3]
    assert nt % tq == 0 and n_lat % tq == 0 and n_lat % GQA_TK == 0 and n_lat % GQA_TK_ONLINE == 0
    assert n_vb * vb == nt and n_lat % vb == 0 and nc % vb == 0 and GQA_TK_ONLINE % vb == 0
    bound = (1.01 * HEAD_DIM * (HEAD_DIM ** -0.5) * LOG2E) * jnp.max(jnp.abs(q_norm)) * jnp.max(jnp.abs(k_norm))
    kern = functools.partial(_gqa_kernel, n_lat=n_lat, nc=nc)
    return pl.pallas_call(
        kern,
        grid=(GQA_KV_HEADS, nt // tq),
        in_specs=[
            pl.BlockSpec(memory_space=pltpu.SMEM),
            pl.BlockSpec((tq, gw), lambda h, b: (b, h)),
            pl.BlockSpec((nt, HEAD_DIM), lambda h, b: (0, h)),
            pl.BlockSpec((None, n_vb, HEAD_DIM, vb), lambda h, b: (h, 0, 0, 0)),
            pl.BlockSpec((tq, gw), lambda h, b: (b, h)),
        ],
        out_specs=pl.BlockSpec((tq, gw), lambda h, b: (b, h)),
        out_shape=jax.ShapeDtypeStruct((nt, GQA_HEADS * HEAD_DIM), BF16),
        compiler_params=_cparams(2),
        name="gqa_attention",
    )(bound.reshape(1).astype(F32), q, k, vt, gs)


def kernel(x, c, ctx, c_ctx, norm_g, w_mod, b_mod, e_w_in, e_rpb, e_dw_w, e_dw_b, e_ln_g, e_ln_b, e_w_out,
           o_w_in, o_q_norm, o_k_norm, o_w_out, final_norm_g):
    batch, n_lat, d = x.shape
    nc = ctx.shape[1]
    depth = norm_g.shape[0]
    assert batch == 1
    rows = n_lat // GRID_W

    nt = n_lat + nc
    stream = (x[0], ctx[0], 0)
    cvec = jnp.zeros((8, d), F32).at[0].set(c[0]).at[1].set(c_ctx)
    mods = _modulation(cvec, w_mod, b_mod)
    cos_t, sin_t = _rope_tables(n_lat, nc)
    fg = final_norm_g.reshape(1, d)

    for l in range(depth):
        i = l // 2
        g = norm_g[l].reshape(1, d)
        mod = mods[l]
        final = l == depth - 1
        if l % 2 == 0:
            qkv, ga, u, gb = _even_proj(stream, nt, g, mod, e_w_in, i, n_lat)
            mix_a = _na_attention(qkv, ga, _na_bias_table(e_rpb[i], rows, nc), n_lat)
            mix_b = _conformer_conv(u, gb, e_dw_w[i], e_dw_b[i], e_ln_g[i], e_ln_b[i], n_lat)
            xs = _out_proj(mix_a, mix_b, (0, 0), e_w_out, i, stream, nt, mod, fg, n_lat, final)
        else:
            v0 = (GQA_HEADS + GQA_KV_HEADS) * HEAD_DIM
            w_v = lax.optimization_barrier(o_w_in[i, :, v0:v0 + GQA_KV_HEADS * HEAD_DIM])
            w_vt = w_v.T.astype(BF16)
            q, k, vt, gs = _odd_proj(stream, nt, g, mod, o_w_in, w_vt, i, cos_t, sin_t,
                                     o_q_norm[i], o_k_norm[i], n_lat)
            mix = _gqa_attention(q, k, vt, gs, o_q_norm[i], o_k_norm[i], n_lat)
            xs = _out_proj(mix, mix, (0, 1), o_w_out, i, stream, nt, mod, fg, n_lat, final)
        stream = (xs, xs, n_lat)
    return xs[None]
```

```python
import functools

import numpy as np
import jax
import jax.numpy as jnp
from jax import lax
from jax.experimental import pallas as pl
from jax.experimental.pallas import tpu as pltpu

F32 = jnp.float32
BF16 = jnp.bfloat16

GRID_W = 64
HEAD_DIM = 128
NORM_EPS = 1e-6
NA_HEADS = 8
NA_WIN_ROWS = 8
NA_WIN_COLS = 16
CONV_KERNEL = 31
GQA_HEADS = 16
GQA_KV_HEADS = 4
GQA_GROUP = GQA_HEADS // GQA_KV_HEADS
ROPE_THETA = 10000.0
LOG2E = 1.4426950408889634

LANES = 128
V7X_VMEM_BYTES = 64 * 1024 * 1024
VMEM_LIMIT = V7X_VMEM_BYTES - 8 * 1024 * 1024

TM_PROJ = 256
TM_OUT = 256
NA_ROWS = 4
NA_SLAB = NA_WIN_ROWS + NA_ROWS
NA_TQ = NA_ROWS * GRID_W
NA_HP = 4
CONV_T = 256
CONV_HALO = 16
CONV_CH = 32
GQA_TQ = 256
GQA_TK = 2048
GQA_TK_ONLINE = 512
GQA_SAFE_BOUND = 60.0
GQA_ONES_ROWS = 16
NEG_BIG = -1e30


def _cparams(n_axes):
    return pltpu.CompilerParams(dimension_semantics=("arbitrary",) * n_axes, vmem_limit_bytes=VMEM_LIMIT)


def _sigmoid(x):
    return 1.0 / (1.0 + jnp.exp(-x))


def _silu(x):
    return x * _sigmoid(x)


def _dot(a, b):
    return jnp.dot(a, b, preferred_element_type=F32)


def _dot_nt(a, b):
    return lax.dot_general(a, b, (((1,), (1,)), ((), ())), preferred_element_type=F32)


def _stream_specs(lat_src, ctx_src, ctx_row0, tm, n_lat):
    d = lat_src.shape[1]
    nlt = n_lat // tm
    n_ctx_tiles = (ctx_src.shape[0] - ctx_row0) // tm
    assert n_lat % tm == 0 and ctx_row0 % tm == 0 and n_ctx_tiles >= 1
    ctx_mode = dict(pipeline_mode=pl.Buffered(1)) if n_ctx_tiles == 1 else {}
    return [pl.BlockSpec((tm, d), lambda i: (jnp.minimum(i, nlt - 1), 0)),
            pl.BlockSpec((tm, d), lambda i: (ctx_row0 // tm + jnp.clip(i - nlt, 0, n_ctx_tiles - 1), 0), **ctx_mode)]


def _stream_tile(x_ref, c_ref, tile_idx, tm, n_lat):
    return jnp.where(tile_idx * tm >= n_lat, c_ref[...], x_ref[...])


def _prenorm(x_ref, c_ref, g_ref, mod_ref, tile_idx, tm, n_lat):
    d = x_ref.shape[-1]
    x = _stream_tile(x_ref, c_ref, tile_idx, tm, n_lat)
    ms = jnp.mean(x * x, axis=-1, keepdims=True)
    row = (tile_idx * tm >= n_lat).astype(jnp.int32)
    mod = mod_ref[pl.ds(row, 1), :]
    a = g_ref[...] * (1.0 + mod[:, d:2 * d])
    return (x * lax.rsqrt(ms + NORM_EPS) * a + mod[:, 0:d]).astype(BF16)


W_STAGE_ROWS = 64
W_STAGE_SLOTS = 4


def _load_weights_bf16(w_hbm, layer, w_bf, stage, sem):
    slots, rows = stage.shape[0], stage.shape[1]
    n_chunks = w_bf.shape[0] // rows

    def copy(c):
        return pltpu.make_async_copy(w_hbm.at[layer, pl.ds(c * rows, rows), :],
                                     stage.at[c % slots], sem.at[c % slots])

    for c in range(min(slots - 1, n_chunks)):
        copy(c).start()
    for c in range(n_chunks):
        copy(c).wait()
        if c + slots - 1 < n_chunks:
            copy(c + slots - 1).start()
        w_bf[c * rows:(c + 1) * rows, :] = stage[c % slots].astype(BF16)


def _weight_scratch(w):
    d, n = w.shape[1:]
    assert d % W_STAGE_ROWS == 0
    return [pltpu.VMEM((d, n), BF16), pltpu.VMEM((W_STAGE_SLOTS, W_STAGE_ROWS, n), F32),
            pltpu.SemaphoreType.DMA((W_STAGE_SLOTS,))]


def _mod_kernel(c_ref, w_ref, b_ref, o_ref):
    s = _silu(c_ref[...]).astype(BF16)
    o_ref[...] = _dot(s, w_ref[...].astype(BF16)) + b_ref[...]


def _modulation(cvec, w_mod, b_mod):
    depth, d, d3 = w_mod.shape
    tn = 768
    return pl.pallas_call(
        _mod_kernel,
        grid=(depth, d3 // tn),
        in_specs=[
            pl.BlockSpec((8, d), lambda l, j: (0, 0)),
            pl.BlockSpec((None, d, tn), lambda l, j: (l, 0, j)),
            pl.BlockSpec((None, 1, tn), lambda l, j: (l, 0, j)),
        ],
        out_specs=pl.BlockSpec((None, 8, tn), lambda l, j: (l, 0, j)),
        out_shape=jax.ShapeDtypeStruct((depth, 8, d3), F32),
        compiler_params=_cparams(2),
        name="modulation",
    )(cvec, w_mod, b_mod.reshape(depth, 1, d3))


def _even_proj_kernel(x_ref, c_ref, g_ref, mod_ref, w_hbm, qkv_ref, ga_ref, u_ref, gb_ref, w_ref, stage, sem,
                      *, n_lat, tm, layer):
    @pl.when(pl.program_id(0) == 0)
    def _():
        _load_weights_bf16(w_hbm, layer, w_ref, stage, sem)

    h = _prenorm(x_ref, c_ref, g_ref, mod_ref, pl.program_id(0), tm, n_lat)
    cw = ga_ref.shape[-1]

    def proj(seg):
        return _dot(h, w_ref[:, seg * cw:(seg + 1) * cw])

    for seg in range(3):
        qkv_ref[:, seg * cw:(seg + 1) * cw] = proj(seg).astype(BF16)
    ga_ref[...] = _silu(proj(3)).astype(BF16)
    u_ref[...] = proj(4) * _sigmoid(proj(5))
    gb_ref[...] = _silu(proj(6)).astype(BF16)


def _even_proj(stream, nt, g, mod, w, layer, n_lat):
    d = stream[0].shape[1]
    tm = TM_PROJ
    cw = w.shape[2] // 7
    assert nt % tm == 0 and n_lat % tm == 0
    kern = functools.partial(_even_proj_kernel, n_lat=n_lat, tm=tm, layer=layer)
    rows = lambda width: pl.BlockSpec((tm, width), lambda i: (i, 0))
    return pl.pallas_call(
        kern,
        grid=(nt // tm,),
        in_specs=_stream_specs(*stream, tm, n_lat) + [
            pl.BlockSpec((1, d), lambda i: (0, 0)),
            pl.BlockSpec((8, 3 * d), lambda i: (0, 0)),
            pl.BlockSpec(memory_space=pl.ANY),
        ],
        out_specs=[rows(3 * cw), rows(cw), rows(cw), rows(cw)],
        out_shape=[
            jax.ShapeDtypeStruct((nt, 3 * cw), BF16),
            jax.ShapeDtypeStruct((nt, cw), BF16),
            jax.ShapeDtypeStruct((nt, cw), F32),
            jax.ShapeDtypeStruct((nt, cw), BF16),
        ],
        scratch_shapes=_weight_scratch(w),
        compiler_params=_cparams(1),
        name="even_proj",
    )(stream[0], stream[1], g, mod, w)


def _na_bias_kernel(rp_ref, o_ref, *, rows, nc):
    pattern = pl.program_id(0)
    lane = lax.broadcasted_iota(jnp.int32, (GRID_W, 2 * GRID_W), 1)
    qc = lax.broadcasted_iota(jnp.int32, (GRID_W, 2 * GRID_W), 0)
    kc = lane % GRID_W
    cs = jnp.clip(qc - NA_WIN_COLS // 2, 0, GRID_W - NA_WIN_COLS)
    col_ok = jnp.logical_and(kc >= cs, kc < cs + NA_WIN_COLS)
    left_half = lane < GRID_W
    masked = jnp.full((GRID_W, 2 * GRID_W), NEG_BIG, F32)

    def toeplitz(dr, right):
        x = jnp.broadcast_to(rp_ref[dr:dr + 1, :], (GRID_W, 2 * GRID_W))
        return pltpu.roll(x, 1 if right else GRID_W + 1, 1, stride=1, stride_axis=0)

    for pat_id, r0 in enumerate((0, 2 * NA_ROWS, rows - NA_ROWS)):
        @pl.when(pattern == pat_id)
        def _(r0=r0):
            ks = int(np.clip(r0 - NA_WIN_ROWS // 2, 0, rows - NA_SLAB))
            for r_local in range(NA_ROWS):
                r = r0 + r_local
                rs = int(np.clip(r - NA_WIN_ROWS // 2, 0, rows - NA_WIN_ROWS))
                for jp in range(NA_SLAB // 2):
                    halves = [toeplitz(kr - r + NA_WIN_ROWS - 1, side == 1) if rs <= kr < rs + NA_WIN_ROWS else None
                              for side, kr in enumerate((ks + 2 * jp, ks + 2 * jp + 1))]
                    if halves[0] is None and halves[1] is None:
                        tile = masked
                    else:
                        left = masked if halves[0] is None else halves[0]
                        right = masked if halves[1] is None else halves[1]
                        tile = jnp.where(col_ok, jnp.where(left_half, left, right) * LOG2E, NEG_BIG)
                    o_ref[r_local * GRID_W:(r_local + 1) * GRID_W, jp * 2 * GRID_W:(jp + 1) * 2 * GRID_W] = tile
    o_ref[:, NA_SLAB * GRID_W:] = jnp.zeros((NA_TQ, nc), F32)


def _na_bias_table(rpb, rows, nc):
    heads = rpb.shape[0]
    assert NA_SLAB % 2 == 0 and 2 * GRID_W == LANES and rpb.shape[1] == 2 * NA_WIN_ROWS - 1
    pad = GRID_W - NA_WIN_COLS
    rp = jnp.pad(rpb.astype(F32), ((0, 0), (0, 1), (pad, LANES - pad - rpb.shape[2])))
    n_keys = NA_SLAB * GRID_W + nc
    kern = functools.partial(_na_bias_kernel, rows=rows, nc=nc)
    return pl.pallas_call(
        kern,
        grid=(3, heads),
        in_specs=[pl.BlockSpec((None, 2 * NA_WIN_ROWS, LANES), lambda p, h: (h, 0, 0))],
        out_specs=pl.BlockSpec((None, None, NA_TQ, n_keys), lambda p, h: (p, h, 0, 0)),
        out_shape=jax.ShapeDtypeStruct((3, heads, NA_TQ, n_keys), F32),
        compiler_params=_cparams(2),
        name="na_bias",
    )(rp)


def _na_kernel(q_ref, k_ref, v_ref, bias_ref, ga_ref, o_ref, *, n_lat, nc, rows):
    b = pl.program_id(1)
    nb_lat = rows // NA_ROWS
    scale = (HEAD_DIM ** -0.5) * LOG2E

    def attend(h, keys, values, bias):
        sl = slice(h * HEAD_DIM, (h + 1) * HEAD_DIM)
        s = _dot_nt(q_ref[:, sl], keys) * scale
        if bias is not None:
            s = s + bias
        p = jnp.exp2(s - jnp.max(s, axis=-1, keepdims=True)).astype(BF16)
        o = _dot(p, jnp.concatenate([values, jnp.ones_like(values)], axis=1))
        gate = ga_ref[:, sl].astype(F32)
        o_ref[:, sl] = (o[:, :HEAD_DIM] / o[:, HEAD_DIM:] * gate).astype(BF16)

    @pl.when(b < nb_lat)
    def _():
        ks = jnp.clip(b * NA_ROWS - NA_WIN_ROWS // 2, 0, rows - NA_SLAB)
        start = pl.multiple_of(ks * GRID_W, GRID_W)
        for h in range(NA_HP):
            sl = slice(h * HEAD_DIM, (h + 1) * HEAD_DIM)
            keys = jnp.concatenate([k_ref[pl.ds(start, NA_SLAB * GRID_W), sl], k_ref[n_lat:n_lat + nc, sl]], axis=0)
            values = jnp.concatenate([v_ref[pl.ds(start, NA_SLAB * GRID_W), sl], v_ref[n_lat:n_lat + nc, sl]], axis=0)
            attend(h, keys, values, bias_ref[h])

    @pl.when(b >= nb_lat)
    def _():
        for h in range(NA_HP):
            sl = slice(h * HEAD_DIM, (h + 1) * HEAD_DIM)
            attend(h, k_ref[n_lat:n_lat + nc, sl], v_ref[n_lat:n_lat + nc, sl], None)


def _na_attention(qkv, ga, bias_tab, n_lat):
    nt = qkv.shape[0]
    nc = nt - n_lat
    rows = n_lat // GRID_W
    assert nc == NA_TQ and rows % NA_ROWS == 0 and rows >= NA_SLAB
    nb = nt // NA_TQ
    nb_lat = rows // NA_ROWS
    n_keys = NA_SLAB * GRID_W + nc
    hw = NA_HP * HEAD_DIM
    groups = NA_HEADS // NA_HP

    def pat(b):
        return jnp.where(b == 0, 0, jnp.where(b >= nb_lat - 1, 2, 1))

    kern = functools.partial(_na_kernel, n_lat=n_lat, nc=nc, rows=rows)
    return pl.pallas_call(
        kern,
        grid=(groups, nb),
        in_specs=[
            pl.BlockSpec((NA_TQ, hw), lambda h, b: (b, h)),
            pl.BlockSpec((nt, hw), lambda h, b: (0, groups + h)),
            pl.BlockSpec((nt, hw), lambda h, b: (0, 2 * groups + h)),
            pl.BlockSpec((None, NA_HP, NA_TQ, n_keys), lambda h, b: (pat(b), h, 0, 0)),
            pl.BlockSpec((NA_TQ, hw), lambda h, b: (b, h)),
        ],
        out_specs=pl.BlockSpec((NA_TQ, hw), lambda h, b: (b, h)),
        out_shape=jax.ShapeDtypeStruct((nt, NA_HEADS * HEAD_DIM), BF16),
        compiler_params=_cparams(2),
        name="na_attention",
    )(qkv, qkv, qkv, bias_tab, ga)


def _conv_kernel(um_ref, up_ref, un_ref, gb_ref, w_ref, b_ref, lg_ref, lb_ref, o_ref, ubuf, shifted, *, n_lat, nt):
    t = pl.program_id(0)
    start = t * CONV_T
    prev_ok = jnp.logical_and(start != 0, start != n_lat)
    next_ok = jnp.logical_and(start + CONV_T != n_lat, start + CONV_T != nt)
    ubuf[0:CONV_HALO, :] = jnp.where(prev_ok, up_ref[...], 0.0)
    ubuf[CONV_HALO:CONV_HALO + CONV_T, :] = um_ref[...]
    ubuf[CONV_HALO + CONV_T:, :] = jnp.where(next_ok, un_ref[...], 0.0)
    span = CONV_T + 2 * CONV_HALO - 8
    for b in range(1, 8):
        shifted[b - 1, :, :] = ubuf[b:b + span, :]
    off = CONV_HALO - CONV_KERNEL // 2
    for r0 in range(0, CONV_T, CONV_CH):
        acc = jnp.zeros((CONV_CH, um_ref.shape[-1]), F32) + b_ref[...]
        for j in range(CONV_KERNEL):
            a, b = divmod(off + j, 8)
            rows = slice(r0 + 8 * a, r0 + 8 * a + CONV_CH)
            tap = ubuf[rows, :] if b == 0 else shifted[b - 1, rows, :]
            acc = acc + w_ref[j:j + 1, :] * tap
        mu = jnp.mean(acc, axis=-1, keepdims=True)
        xc = acc - mu
        var = jnp.mean(xc * xc, axis=-1, keepdims=True)
        y = xc * lax.rsqrt(var + NORM_EPS) * lg_ref[...] + lb_ref[...]
        y = _silu(y) * gb_ref[r0:r0 + CONV_CH, :].astype(F32)
        o_ref[r0:r0 + CONV_CH, :] = y.astype(BF16)


def _conformer_conv(u, gb, dw_w, dw_b, ln_g, ln_b, n_lat):
    nt, cw = u.shape
    assert nt % CONV_T == 0 and n_lat % CONV_T == 0
    hb = CONV_T // CONV_HALO
    n_halo_blocks = nt // CONV_HALO
    w_pad = jnp.zeros((32, cw), F32).at[:CONV_KERNEL].set(dw_w)
    kern = functools.partial(_conv_kernel, n_lat=n_lat, nt=nt)
    vec = lambda: pl.BlockSpec((1, cw), lambda t: (0, 0))
    return pl.pallas_call(
        kern,
        grid=(nt // CONV_T,),
        in_specs=[
            pl.BlockSpec((CONV_T, cw), lambda t: (t, 0)),
            pl.BlockSpec((CONV_HALO, cw), lambda t: (jnp.maximum(t * hb - 1, 0), 0)),
            pl.BlockSpec((CONV_HALO, cw), lambda t: (jnp.minimum((t + 1) * hb, n_halo_blocks - 1), 0)),
            pl.BlockSpec((CONV_T, cw), lambda t: (t, 0)),
            pl.BlockSpec((32, cw), lambda t: (0, 0)),
            vec(), vec(), vec(),
        ],
        out_specs=pl.BlockSpec((CONV_T, cw), lambda t: (t, 0)),
        out_shape=jax.ShapeDtypeStruct((nt, cw), BF16),
        scratch_shapes=[pltpu.VMEM((CONV_T + 2 * CONV_HALO, cw), F32),
                        pltpu.VMEM((7, CONV_T + 2 * CONV_HALO - 8, cw), F32)],
        compiler_params=_cparams(1),
        name="conformer_conv",
    )(u, u, u, gb, w_pad, dw_b.reshape(1, cw), ln_g.reshape(1, cw), ln_b.reshape(1, cw))


def _out_proj_kernel(a_ref, b_ref, w32_ref, x_ref, c_ref, mod_ref, fg_ref, o_ref, w_ref, *, n_lat, tm, final):
    i = pl.program_id(0)

    @pl.when(i == 0)
    def _():
        w_ref[...] = w32_ref[...].astype(BF16)

    half = a_ref.shape[-1]
    y = _dot(a_ref[...], w_ref[0:half, :]) + _dot(b_ref[...], w_ref[half:, :])
    d = y.shape[-1]
    row = (i * tm >= n_lat).astype(jnp.int32)
    gt = mod_ref[pl.ds(row, 1), 2 * d:3 * d]
    xn = _stream_tile(x_ref, c_ref, i, tm, n_lat) + gt * y
    if final:
        ms = jnp.mean(xn * xn, axis=-1, keepdims=True)
        xn = xn * lax.rsqrt(ms + NORM_EPS) * fg_ref[...]
    o_ref[...] = xn


def _out_proj(mix_a, mix_b, cols, w, layer, stream, nt, mod, fg, n_lat, final):
    d = stream[0].shape[1]
    tm = TM_OUT
    half = w.shape[1] // 2
    out_rows = n_lat if final else nt
    assert out_rows % tm == 0
    kern = functools.partial(_out_proj_kernel, n_lat=n_lat, tm=tm, final=final)
    return pl.pallas_call(
        kern,
        grid=(out_rows // tm,),
        in_specs=[
            pl.BlockSpec((tm, half), lambda i: (i, cols[0])),
            pl.BlockSpec((tm, half), lambda i: (i, cols[1])),
            pl.BlockSpec((None, 2 * half, d), lambda i: (layer, 0, 0), pipeline_mode=pl.Buffered(1)),
        ] + _stream_specs(*stream, tm, n_lat) + [
            pl.BlockSpec((8, 3 * d), lambda i: (0, 0)),
            pl.BlockSpec((1, d), lambda i: (0, 0)),
        ],
        out_specs=pl.BlockSpec((tm, d), lambda i: (i, 0)),
        out_shape=jax.ShapeDtypeStruct((out_rows, d), F32),
        scratch_shapes=[pltpu.VMEM((2 * half, d), BF16)],
        compiler_params=_cparams(1),
        name="out_proj_final" if final else "out_proj",
    )(mix_a, mix_b, w, stream[0], stream[1], mod, fg)


def _rope_tables(n_lat, nc):
    rows = n_lat // GRID_W
    n_freq = HEAD_DIM // 4
    inv = ROPE_THETA ** (-jnp.arange(n_freq, dtype=F32) / n_freq)
    row_ang = jnp.arange(rows, dtype=jnp.int32).astype(F32)[:, None] * inv
    col_ang = jnp.arange(GRID_W, dtype=jnp.int32).astype(F32)[:, None] * inv
    per_row = lambda a: jnp.repeat(a, GRID_W, axis=0)
    per_col = lambda a: jnp.tile(a, (rows, 1))
    cr, sr = per_row(jnp.cos(row_ang)), per_row(jnp.sin(row_ang))
    cc, sc = per_col(jnp.cos(col_ang)), per_col(jnp.sin(col_ang))
    cos_t = jnp.concatenate([cr, cr, cc, cc], axis=-1)
    sin_t = jnp.concatenate([-sr, sr, -sc, sc], axis=-1)
    cos_t = jnp.concatenate([cos_t, jnp.ones((nc, HEAD_DIM), F32)], axis=0)
    sin_t = jnp.concatenate([sin_t, jnp.zeros((nc, HEAD_DIM), F32)], axis=0)
    return cos_t, sin_t


def _norm_rope(x, gain, cos_t, sin_t, first_half):
    ms = jnp.mean(x * x, axis=-1, keepdims=True)
    xn = x * lax.rsqrt(ms + NORM_EPS) * gain
    partner = jnp.where(first_half, pltpu.roll(xn, 3 * HEAD_DIM // 4, 1), pltpu.roll(xn, HEAD_DIM // 4, 1))
    return xn * cos_t + partner * sin_t


def _odd_proj_kernel(x_ref, c_ref, g_ref, mod_ref, w_hbm, wvt_ref, cos_ref, sin_ref, qn_ref, kn_ref,
                     q_ref, k_ref, vt_ref, gs_ref, w_ref, stage, sem, *, n_lat, tm, layer):
    @pl.when(pl.program_id(0) == 0)
    def _():
        _load_weights_bf16(w_hbm, layer, w_ref, stage, sem)

    h = _prenorm(x_ref, c_ref, g_ref, mod_ref, pl.program_id(0), tm, n_lat)
    tn = k_ref.shape[-1]
    cos_t = cos_ref[...]
    sin_t = sin_ref[...]
    lane = lax.broadcasted_iota(jnp.int32, (1, HEAD_DIM), 1)
    first_half = (lane % (HEAD_DIM // 2)) < (HEAD_DIM // 4)

    def proj(seg):
        return _dot(h, w_ref[:, seg * tn:(seg + 1) * tn])

    def rotated(seg, gain, out_ref, col0):
        acc = proj(seg)
        for hh in range(tn // HEAD_DIM):
            y = _norm_rope(acc[:, hh * HEAD_DIM:(hh + 1) * HEAD_DIM], gain, cos_t, sin_t, first_half)
            out_ref[:, col0 + hh * HEAD_DIM:col0 + (hh + 1) * HEAD_DIM] = y.astype(BF16)

    n_q = q_ref.shape[-1] // tn
    q_gain = qn_ref[...] * ((HEAD_DIM ** -0.5) * LOG2E)
    for seg in range(n_q):
        rotated(seg, q_gain, q_ref, seg * tn)
    rotated(n_q, kn_ref[...], k_ref, 0)
    v_t = _dot_nt(wvt_ref[...], h)
    vt_ref[...] = v_t.reshape(vt_ref.shape).astype(BF16)
    for seg in range(n_q):
        gs_ref[:, seg * tn:(seg + 1) * tn] = _silu(proj(n_q + 2 + seg)).astype(BF16)


def _odd_proj(stream, nt, g, mod, w, w_vt, layer, cos_t, sin_t, q_norm, k_norm, n_lat):
    d = stream[0].shape[1]
    tm = TM_PROJ
    qw = GQA_HEADS * HEAD_DIM
    kw = GQA_KV_HEADS * HEAD_DIM
    assert w.shape[2] == 2 * qw + 2 * kw and nt % tm == 0 and n_lat % tm == 0
    kern = functools.partial(_odd_proj_kernel, n_lat=n_lat, tm=tm, layer=layer)
    rows = lambda width: pl.BlockSpec((tm, width), lambda i: (i, 0))
    vec = lambda width: pl.BlockSpec((1, width), lambda i: (0, 0))
    return pl.pallas_call(
        kern,
        grid=(nt // tm,),
        in_specs=_stream_specs(*stream, tm, n_lat) + [
            vec(d),
            pl.BlockSpec((8, 3 * d), lambda i: (0, 0)),
            pl.BlockSpec(memory_space=pl.ANY),
            pl.BlockSpec((kw, d), lambda i: (0, 0)),
            rows(HEAD_DIM), rows(HEAD_DIM), vec(HEAD_DIM), vec(HEAD_DIM),
        ],
        out_specs=[rows(qw), rows(kw),
                   pl.BlockSpec((GQA_KV_HEADS, None, HEAD_DIM, tm), lambda i: (0, i, 0, 0)),
                   rows(qw)],
        out_shape=[
            jax.ShapeDtypeStruct((nt, qw), BF16),
            jax.ShapeDtypeStruct((nt, kw), BF16),
            jax.ShapeDtypeStruct((GQA_KV_HEADS, nt // tm, HEAD_DIM, tm), BF16),
            jax.ShapeDtypeStruct((nt, qw), BF16),
        ],
        scratch_shapes=_weight_scratch(w),
        compiler_params=_cparams(1),
        name="odd_proj",
    )(stream[0], stream[1], g, mod, w, w_vt, cos_t, sin_t,
      q_norm.reshape(1, HEAD_DIM), k_norm.reshape(1, HEAD_DIM))


def _gqa_kernel(bound_ref, q_ref, k_ref, vt_ref, g_ref, o_ref, *, n_lat, nc):
    b = pl.program_id(1)
    tq = q_ref.shape[0]
    vb = vt_ref.shape[-1]
    bound = bound_ref[0]
    is_lat = b * tq < n_lat
    q = jnp.concatenate([q_ref[:, h * HEAD_DIM:(h + 1) * HEAD_DIM] for h in range(GQA_GROUP)], axis=0)

    def vt_chunk(first_block, n_blocks):
        return jnp.concatenate([vt_ref[first_block + j] for j in range(n_blocks)], axis=1)

    def write_out(o_t):
        o = o_t.T
        for h in range(GQA_GROUP):
            sl = slice(h * HEAD_DIM, (h + 1) * HEAD_DIM)
            o_ref[:, sl] = (o[h * tq:(h + 1) * tq, :] * g_ref[:, sl].astype(F32)).astype(BF16)

    def fixed_shift_chunk(row0, n_rows):
        p = jnp.exp2(_dot_nt(k_ref[row0:row0 + n_rows, :], q) - bound)
        partial = jnp.sum(p.reshape(n_rows // 8, 8, p.shape[-1]), axis=0)
        return _dot(vt_chunk(row0 // vb, n_rows // vb), p.astype(BF16)), partial

    def finish(acc, partial):
        write_out(acc / jnp.sum(partial, axis=0, keepdims=True))

    small = bound <= GQA_SAFE_BOUND

    @pl.when(jnp.logical_and(small, is_lat))
    def _():
        acc, partial = fixed_shift_chunk(n_lat, nc)
        for c in range(n_lat // GQA_TK):
            acc_c, partial_c = fixed_shift_chunk(c * GQA_TK, GQA_TK)
            acc, partial = acc + acc_c, partial + partial_c
        finish(acc, partial)

    @pl.when(jnp.logical_and(small, jnp.logical_not(is_lat)))
    def _():
        finish(*fixed_shift_chunk(n_lat, nc))

    @pl.when(bound > GQA_SAFE_BOUND)
    def _():
        cols = GQA_GROUP * tq
        per = GQA_TK_ONLINE // vb

        def step(carry, k_chunk, vt):
            m, l, acc = carry
            s = _dot_nt(k_chunk, q)
            m_new = jnp.maximum(m, jnp.max(s, axis=0, keepdims=True))
            alpha = jnp.exp2(m - m_new)
            p = jnp.exp2(s - m_new)
            l = alpha * l + jnp.sum(p, axis=0, keepdims=True)
            acc = alpha * acc + _dot(vt, p.astype(BF16))
            return m_new, l, acc

        init = (jnp.full((1, cols), NEG_BIG, F32), jnp.zeros((1, cols), F32), jnp.zeros((HEAD_DIM, cols), F32))
        carry = step(init, k_ref[n_lat:n_lat + nc, :], vt_chunk(n_lat // vb, nc // vb))

        def body(c, carry):
            start = pl.multiple_of(c * GQA_TK_ONLINE, GQA_TK_ONLINE)
            vt = jnp.concatenate([vt_ref[c * per + j] for j in range(per)], axis=1)
            return step(carry, k_ref[pl.ds(start, GQA_TK_ONLINE), :], vt)

        m, l, acc = lax.fori_loop(0, jnp.where(is_lat, n_lat // GQA_TK_ONLINE, 0), body, carry)
        write_out(acc / l)


def _gqa_attention(q, k, vt, gs, q_norm, k_norm, n_lat):
    nt = q.shape[0]
    nc = nt - n_lat
    tq = GQA_TQ
    gw = GQA_GROUP * HEAD_DIM
    n_vb, vb = vt.shape[1], vt.shape[3]
    assert nt % tq == 0 and n_lat % tq == 0 and n_lat % GQA_TK == 0 and n_lat % GQA_TK_ONLINE == 0
    assert n_vb * vb == nt and n_lat % vb == 0 and nc % vb == 0 and GQA_TK_ONLINE % vb == 0
    bound = (1.01 * HEAD_DIM * (HEAD_DIM ** -0.5) * LOG2E) * jnp.max(jnp.abs(q_norm)) * jnp.max(jnp.abs(k_norm))
    kern = functools.partial(_gqa_kernel, n_lat=n_lat, nc=nc)
    return pl.pallas_call(
        kern,
        grid=(GQA_KV_HEADS, nt // tq),
        in_specs=[
            pl.BlockSpec(memory_space=pltpu.SMEM),
            pl.BlockSpec((tq, gw), lambda h, b: (b, h)),
            pl.BlockSpec((nt, HEAD_DIM), lambda h, b: (0, h)),
            pl.BlockSpec((None, n_vb, HEAD_DIM, vb), lambda h, b: (h, 0, 0, 0)),
            pl.BlockSpec((tq, gw), lambda h, b: (b, h)),
        ],
        out_specs=pl.BlockSpec((tq, gw), lambda h, b: (b, h)),
        out_shape=jax.ShapeDtypeStruct((nt, GQA_HEADS * HEAD_DIM), BF16),
        compiler_params=_cparams(2),
        name="gqa_attention",
    )(bound.reshape(1).astype(F32), q, k, vt, gs)


def kernel(x, c, ctx, c_ctx, norm_g, w_mod, b_mod, e_w_in, e_rpb, e_dw_w, e_dw_b, e_ln_g, e_ln_b, e_w_out,
           o_w_in, o_q_norm, o_k_norm, o_w_out, final_norm_g):
    batch, n_lat, d = x.shape
    nc = ctx.shape[1]
    depth = norm_g.shape[0]
    assert batch == 1
    rows = n_lat // GRID_W

    nt = n_lat + nc
    stream = (x[0], ctx[0], 0)
    cvec = jnp.zeros((8, d), F32).at[0].set(c[0]).at[1].set(c_ctx)
    mods = _modulation(cvec, w_mod, b_mod)
    cos_t, sin_t = _rope_tables(n_lat, nc)
    fg = final_norm_g.reshape(1, d)

    for l in range(depth):
        i = l // 2
        g = norm_g[l].reshape(1, d)
        mod = mods[l]
        final = l == depth - 1
        if l % 2 == 0:
            qkv, ga, u, gb = _even_proj(stream, nt, g, mod, e_w_in, i, n_lat)
            mix_a = _na_attention(qkv, ga, _na_bias_table(e_rpb[i], rows, nc), n_lat)
            mix_b = _conformer_conv(u, gb, e_dw_w[i], e_dw_b[i], e_ln_g[i], e_ln_b[i], n_lat)
            xs = _out_proj(mix_a, mix_b, (0, 0), e_w_out, i, stream, nt, mod, fg, n_lat, final)
        else:
            v0 = (GQA_HEADS + GQA_KV_HEADS) * HEAD_DIM
            w_v = lax.optimization_barrier(o_w_in[i, :, v0:v0 + GQA_KV_HEADS * HEAD_DIM])
            w_vt = w_v.T.astype(BF16)
            q, k, vt, gs = _odd_proj(stream, nt, g, mod, o_w_in, w_vt, i, cos_t, sin_t,
                                     o_q_norm[i], o_k_norm[i], n_lat)
            mix = _gqa_attention(q, k, vt, gs, o_q_norm[i], o_k_norm[i], n_lat)
            xs = _out_proj(mix, mix, (0, 1), o_w_out, i, stream, nt, mod, fg, n_lat, final)
        stream = (xs, xs, n_lat)
    return xs[None]
```

```python
import functools

import numpy as np
import jax
import jax.numpy as jnp
from jax import lax
from jax.experimental import pallas as pl
from jax.experimental.pallas import tpu as pltpu

F32 = jnp.float32
BF16 = jnp.bfloat16

GRID_W = 64
HEAD_DIM = 128
NORM_EPS = 1e-6
NA_HEADS = 8
NA_WIN_ROWS = 8
NA_WIN_COLS = 16
CONV_KERNEL = 31
GQA_HEADS = 16
GQA_KV_HEADS = 4
GQA_GROUP = GQA_HEADS // GQA_KV_HEADS
ROPE_THETA = 10000.0
LOG2E = 1.4426950408889634

LANES = 128
V7X_VMEM_BYTES = 64 * 1024 * 1024
VMEM_LIMIT = V7X_VMEM_BYTES - 8 * 1024 * 1024

TM_PROJ = 256
TM_OUT = 256
NA_ROWS = 4
NA_SLAB = NA_WIN_ROWS + NA_ROWS
NA_TQ = NA_ROWS * GRID_W
NA_HP = 4
CONV_T = 256
CONV_HALO = 16
CONV_CH = 32
GQA_TQ = 256
GQA_TK = 2048
GQA_TK_ONLINE = 512
GQA_SAFE_BOUND = 60.0
NEG_BIG = -1e30


def _cparams(n_axes):
    return pltpu.CompilerParams(dimension_semantics=("arbitrary",) * n_axes, vmem_limit_bytes=VMEM_LIMIT)


def _sigmoid(x):
    return 1.0 / (1.0 + jnp.exp(-x))


def _silu(x):
    return x * _sigmoid(x)


def _dot(a, b):
    return jnp.dot(a, b, preferred_element_type=F32)


def _dot_nt(a, b):
    return lax.dot_general(a, b, (((1,), (1,)), ((), ())), preferred_element_type=F32)


def _stream_specs(lat_src, ctx_src, ctx_row0, tm, n_lat):
    d = lat_src.shape[1]
    nlt = n_lat // tm
    n_ctx_tiles = (ctx_src.shape[0] - ctx_row0) // tm
    assert n_lat % tm == 0 and ctx_row0 % tm == 0 and n_ctx_tiles >= 1
    ctx_mode = dict(pipeline_mode=pl.Buffered(1)) if n_ctx_tiles == 1 else {}
    return [pl.BlockSpec((tm, d), lambda i: (jnp.minimum(i, nlt - 1), 0)),
            pl.BlockSpec((tm, d), lambda i: (ctx_row0 // tm + jnp.clip(i - nlt, 0, n_ctx_tiles - 1), 0), **ctx_mode)]


def _stream_tile(x_ref, c_ref, tile_idx, tm, n_lat):
    return jnp.where(tile_idx * tm >= n_lat, c_ref[...], x_ref[...])


def _prenorm(x_ref, c_ref, g_ref, mod_ref, tile_idx, tm, n_lat):
    d = x_ref.shape[-1]
    x = _stream_tile(x_ref, c_ref, tile_idx, tm, n_lat)
    ms = jnp.mean(x * x, axis=-1, keepdims=True)
    row = (tile_idx * tm >= n_lat).astype(jnp.int32)
    mod = mod_ref[pl.ds(row, 1), :]
    a = g_ref[...] * (1.0 + mod[:, d:2 * d])
    return (x * lax.rsqrt(ms + NORM_EPS) * a + mod[:, 0:d]).astype(BF16)


W_STAGE_ROWS = 64
W_STAGE_SLOTS = 4


def _load_weights_bf16(w_hbm, layer, w_bf, stage, sem):
    slots, rows = stage.shape[0], stage.shape[1]
    n_chunks = w_bf.shape[0] // rows

    def copy(c):
        return pltpu.make_async_copy(w_hbm.at[layer, pl.ds(c * rows, rows), :],
                                     stage.at[c % slots], sem.at[c % slots])

    for c in range(min(slots - 1, n_chunks)):
        copy(c).start()
    for c in range(n_chunks):
        copy(c).wait()
        if c + slots - 1 < n_chunks:
            copy(c + slots - 1).start()
        w_bf[c * rows:(c + 1) * rows, :] = stage[c % slots].astype(BF16)


def _weight_scratch(w):
    d, n = w.shape[1:]
    assert d % W_STAGE_ROWS == 0
    return [pltpu.VMEM((d, n), BF16), pltpu.VMEM((W_STAGE_SLOTS, W_STAGE_ROWS, n), F32),
            pltpu.SemaphoreType.DMA((W_STAGE_SLOTS,))]


def _mod_kernel(c_ref, w_ref, b_ref, o_ref):
    s = _silu(c_ref[...]).astype(BF16)
    o_ref[...] = _dot(s, w_ref[...].astype(BF16)) + b_ref[...]


def _modulation(cvec, w_mod, b_mod):
    depth, d, d3 = w_mod.shape
    tn = 768
    return pl.pallas_call(
        _mod_kernel,
        grid=(depth, d3 // tn),
        in_specs=[
            pl.BlockSpec((8, d), lambda l, j: (0, 0)),
            pl.BlockSpec((None, d, tn), lambda l, j: (l, 0, j)),
            pl.BlockSpec((None, 1, tn), lambda l, j: (l, 0, j)),
        ],
        out_specs=pl.BlockSpec((None, 8, tn), lambda l, j: (l, 0, j)),
        out_shape=jax.ShapeDtypeStruct((depth, 8, d3), F32),
        compiler_params=_cparams(2),
        name="modulation",
    )(cvec, w_mod, b_mod.reshape(depth, 1, d3))


def _even_proj_kernel(x_ref, c_ref, g_ref, mod_ref, w_hbm, qkv_ref, ga_ref, u_ref, gb_ref, w_ref, stage, sem,
                      *, n_lat, tm, layer):
    @pl.when(pl.program_id(0) == 0)
    def _():
        _load_weights_bf16(w_hbm, layer, w_ref, stage, sem)

    h = _prenorm(x_ref, c_ref, g_ref, mod_ref, pl.program_id(0), tm, n_lat)
    cw = ga_ref.shape[-1]

    def proj(seg):
        return _dot(h, w_ref[:, seg * cw:(seg + 1) * cw])

    for seg in range(3):
        qkv_ref[:, seg * cw:(seg + 1) * cw] = proj(seg).astype(BF16)
    ga_ref[...] = _silu(proj(3)).astype(BF16)
    u_ref[...] = proj(4) * _sigmoid(proj(5))
    gb_ref[...] = _silu(proj(6)).astype(BF16)


def _even_proj(stream, nt, g, mod, w, layer, n_lat):
    d = stream[0].shape[1]
    tm = TM_PROJ
    cw = w.shape[2] // 7
    assert nt % tm == 0 and n_lat % tm == 0
    kern = functools.partial(_even_proj_kernel, n_lat=n_lat, tm=tm, layer=layer)
    rows = lambda width: pl.BlockSpec((tm, width), lambda i: (i, 0))
    return pl.pallas_call(
        kern,
        grid=(nt // tm,),
        in_specs=_stream_specs(*stream, tm, n_lat) + [
            pl.BlockSpec((1, d), lambda i: (0, 0)),
            pl.BlockSpec((8, 3 * d), lambda i: (0, 0)),
            pl.BlockSpec(memory_space=pl.ANY),
        ],
        out_specs=[rows(3 * cw), rows(cw), rows(cw), rows(cw)],
        out_shape=[
            jax.ShapeDtypeStruct((nt, 3 * cw), BF16),
            jax.ShapeDtypeStruct((nt, cw), BF16),
            jax.ShapeDtypeStruct((nt, cw), F32),
            jax.ShapeDtypeStruct((nt, cw), BF16),
        ],
        scratch_shapes=_weight_scratch(w),
        compiler_params=_cparams(1),
        name="even_proj",
    )(stream[0], stream[1], g, mod, w)


def _na_bias_kernel(rp_ref, o_ref, *, rows, nc):
    pattern = pl.program_id(0)
    lane = lax.broadcasted_iota(jnp.int32, (GRID_W, 2 * GRID_W), 1)
    qc = lax.broadcasted_iota(jnp.int32, (GRID_W, 2 * GRID_W), 0)
    kc = lane % GRID_W
    cs = jnp.clip(qc - NA_WIN_COLS // 2, 0, GRID_W - NA_WIN_COLS)
    col_ok = jnp.logical_and(kc >= cs, kc < cs + NA_WIN_COLS)
    left_half = lane < GRID_W
    masked = jnp.full((GRID_W, 2 * GRID_W), NEG_BIG, F32)

    def toeplitz(dr, right):
        x = jnp.broadcast_to(rp_ref[dr:dr + 1, :], (GRID_W, 2 * GRID_W))
        return pltpu.roll(x, 1 if right else GRID_W + 1, 1, stride=1, stride_axis=0)

    for pat_id, r0 in enumerate((0, 2 * NA_ROWS, rows - NA_ROWS)):
        @pl.when(pattern == pat_id)
        def _(r0=r0):
            ks = int(np.clip(r0 - NA_WIN_ROWS // 2, 0, rows - NA_SLAB))
            for r_local in range(NA_ROWS):
                r = r0 + r_local
                rs = int(np.clip(r - NA_WIN_ROWS // 2, 0, rows - NA_WIN_ROWS))
                for jp in range(NA_SLAB // 2):
                    halves = [toeplitz(kr - r + NA_WIN_ROWS - 1, side == 1) if rs <= kr < rs + NA_WIN_ROWS else None
                              for side, kr in enumerate((ks + 2 * jp, ks + 2 * jp + 1))]
                    if halves[0] is None and halves[1] is None:
                        tile = masked
                    else:
                        left = masked if halves[0] is None else halves[0]
                        right = masked if halves[1] is None else halves[1]
                        tile = jnp.where(col_ok, jnp.where(left_half, left, right) * LOG2E, NEG_BIG)
                    o_ref[r_local * GRID_W:(r_local + 1) * GRID_W, jp * 2 * GRID_W:(jp + 1) * 2 * GRID_W] = tile
    o_ref[:, NA_SLAB * GRID_W:] = jnp.zeros((NA_TQ, nc), F32)


def _na_bias_table(rpb, rows, nc):
    heads = rpb.shape[0]
    assert NA_SLAB % 2 == 0 and 2 * GRID_W == LANES and rpb.shape[1] == 2 * NA_WIN_ROWS - 1
    pad = GRID_W - NA_WIN_COLS
    rp = jnp.pad(rpb.astype(F32), ((0, 0), (0, 1), (pad, LANES - pad - rpb.shape[2])))
    n_keys = NA_SLAB * GRID_W + nc
    kern = functools.partial(_na_bias_kernel, rows=rows, nc=nc)
    return pl.pallas_call(
        kern,
        grid=(3, heads),
        in_specs=[pl.BlockSpec((None, 2 * NA_WIN_ROWS, LANES), lambda p, h: (h, 0, 0))],
        out_specs=pl.BlockSpec((None, None, NA_TQ, n_keys), lambda p, h: (p, h, 0, 0)),
        out_shape=jax.ShapeDtypeStruct((3, heads, NA_TQ, n_keys), F32),
        compiler_params=_cparams(2),
        name="na_bias",
    )(rp)


def _na_kernel(q_ref, k_ref, v_ref, bias_ref, ga_ref, o_ref, *, n_lat, nc, rows):
    b = pl.program_id(1)
    nb_lat = rows // NA_ROWS
    scale = (HEAD_DIM ** -0.5) * LOG2E

    def attend(h, keys, values, bias):
        sl = slice(h * HEAD_DIM, (h + 1) * HEAD_DIM)
        s = _dot_nt(q_ref[:, sl], keys) * scale
        if bias is not None:
            s = s + bias
        p = jnp.exp2(s - jnp.max(s, axis=-1, keepdims=True)).astype(BF16)
        o = _dot(p, jnp.concatenate([values, jnp.ones_like(values)], axis=1))
        gate = ga_ref[:, sl].astype(F32)
        o_ref[:, sl] = (o[:, :HEAD_DIM] / o[:, HEAD_DIM:] * gate).astype(BF16)

    @pl.when(b < nb_lat)
    def _():
        ks = jnp.clip(b * NA_ROWS - NA_WIN_ROWS // 2, 0, rows - NA_SLAB)
        start = pl.multiple_of(ks * GRID_W, GRID_W)
        for h in range(NA_HP):
            sl = slice(h * HEAD_DIM, (h + 1) * HEAD_DIM)
            keys = jnp.concatenate([k_ref[pl.ds(start, NA_SLAB * GRID_W), sl], k_ref[n_lat:n_lat + nc, sl]], axis=0)
            values = jnp.concatenate([v_ref[pl.ds(start, NA_SLAB * GRID_W), sl], v_ref[n_lat:n_lat + nc, sl]], axis=0)
            attend(h, keys, values, bias_ref[h])

    @pl.when(b >= nb_lat)
    def _():
        for h in range(NA_HP):
            sl = slice(h * HEAD_DIM, (h + 1) * HEAD_DIM)
            attend(h, k_ref[n_lat:n_lat + nc, sl], v_ref[n_lat:n_lat + nc, sl], None)


def _na_attention(qkv, ga, bias_tab, n_lat):
    nt = qkv.shape[0]
    nc = nt - n_lat
    rows = n_lat // GRID_W
    assert nc == NA_TQ and rows % NA_ROWS == 0 and rows >= NA_SLAB
    nb = nt // NA_TQ
    nb_lat = rows // NA_ROWS
    n_keys = NA_SLAB * GRID_W + nc
    hw = NA_HP * HEAD_DIM
    groups = NA_HEADS // NA_HP

    def pat(b):
        return jnp.where(b == 0, 0, jnp.where(b >= nb_lat - 1, 2, 1))

    kern = functools.partial(_na_kernel, n_lat=n_lat, nc=nc, rows=rows)
    return pl.pallas_call(
        kern,
        grid=(groups, nb),
        in_specs=[
            pl.BlockSpec((NA_TQ, hw), lambda h, b: (b, h)),
            pl.BlockSpec((nt, hw), lambda h, b: (0, groups + h)),
            pl.BlockSpec((nt, hw), lambda h, b: (0, 2 * groups + h)),
            pl.BlockSpec((None, NA_HP, NA_TQ, n_keys), lambda h, b: (pat(b), h, 0, 0)),
            pl.BlockSpec((NA_TQ, hw), lambda h, b: (b, h)),
        ],
        out_specs=pl.BlockSpec((NA_TQ, hw), lambda h, b: (b, h)),
        out_shape=jax.ShapeDtypeStruct((nt, NA_HEADS * HEAD_DIM), BF16),
        compiler_params=_cparams(2),
        name="na_attention",
    )(qkv, qkv, qkv, bias_tab, ga)


def _conv_kernel(um_ref, up_ref, un_ref, gb_ref, w_ref, b_ref, lg_ref, lb_ref, o_ref, ubuf, shifted, *, n_lat, nt):
    t = pl.program_id(0)
    start = t * CONV_T
    prev_ok = jnp.logical_and(start != 0, start != n_lat)
    next_ok = jnp.logical_and(start + CONV_T != n_lat, start + CONV_T != nt)
    ubuf[0:CONV_HALO, :] = jnp.where(prev_ok, up_ref[...], 0.0)
    ubuf[CONV_HALO:CONV_HALO + CONV_T, :] = um_ref[...]
    ubuf[CONV_HALO + CONV_T:, :] = jnp.where(next_ok, un_ref[...], 0.0)
    span = CONV_T + 2 * CONV_HALO - 8
    for b in range(1, 8):
        shifted[b - 1, :, :] = ubuf[b:b + span, :]
    off = CONV_HALO - CONV_KERNEL // 2
    for r0 in range(0, CONV_T, CONV_CH):
        acc = jnp.zeros((CONV_CH, um_ref.shape[-1]), F32) + b_ref[...]
        for j in range(CONV_KERNEL):
            a, b = divmod(off + j, 8)
            rows = slice(r0 + 8 * a, r0 + 8 * a + CONV_CH)
            tap = ubuf[rows, :] if b == 0 else shifted[b - 1, rows, :]
            acc = acc + w_ref[j:j + 1, :] * tap
        mu = jnp.mean(acc, axis=-1, keepdims=True)
        xc = acc - mu
        var = jnp.mean(xc * xc, axis=-1, keepdims=True)
        y = xc * lax.rsqrt(var + NORM_EPS) * lg_ref[...] + lb_ref[...]
        y = _silu(y) * gb_ref[r0:r0 + CONV_CH, :].astype(F32)
        o_ref[r0:r0 + CONV_CH, :] = y.astype(BF16)


def _conformer_conv(u, gb, dw_w, dw_b, ln_g, ln_b, n_lat):
    nt, cw = u.shape
    assert nt % CONV_T == 0 and n_lat % CONV_T == 0
    hb = CONV_T // CONV_HALO
    n_halo_blocks = nt // CONV_HALO
    w_pad = jnp.zeros((32, cw), F32).at[:CONV_KERNEL].set(dw_w)
    kern = functools.partial(_conv_kernel, n_lat=n_lat, nt=nt)
    vec = lambda: pl.BlockSpec((1, cw), lambda t: (0, 0))
    return pl.pallas_call(
        kern,
        grid=(nt // CONV_T,),
        in_specs=[
            pl.BlockSpec((CONV_T, cw), lambda t: (t, 0)),
            pl.BlockSpec((CONV_HALO, cw), lambda t: (jnp.maximum(t * hb - 1, 0), 0)),
            pl.BlockSpec((CONV_HALO, cw), lambda t: (jnp.minimum((t + 1) * hb, n_halo_blocks - 1), 0)),
            pl.BlockSpec((CONV_T, cw), lambda t: (t, 0)),
            pl.BlockSpec((32, cw), lambda t: (0, 0)),
            vec(), vec(), vec(),
        ],
        out_specs=pl.BlockSpec((CONV_T, cw), lambda t: (t, 0)),
        out_shape=jax.ShapeDtypeStruct((nt, cw), BF16),
        scratch_shapes=[pltpu.VMEM((CONV_T + 2 * CONV_HALO, cw), F32),
                        pltpu.VMEM((7, CONV_T + 2 * CONV_HALO - 8, cw), F32)],
        compiler_params=_cparams(1),
        name="conformer_conv",
    )(u, u, u, gb, w_pad, dw_b.reshape(1, cw), ln_g.reshape(1, cw), ln_b.reshape(1, cw))


def _out_proj_kernel(a_ref, b_ref, w32_ref, x_ref, c_ref, mod_ref, fg_ref, o_ref, w_ref, *, n_lat, tm, final):
    i = pl.program_id(0)

    @pl.when(i == 0)
    def _():
        w_ref[...] = w32_ref[...].astype(BF16)

    half = a_ref.shape[-1]
    y = _dot(a_ref[...], w_ref[0:half, :]) + _dot(b_ref[...], w_ref[half:, :])
    d = y.shape[-1]
    row = (i * tm >= n_lat).astype(jnp.int32)
    gt = mod_ref[pl.ds(row, 1), 2 * d:3 * d]
    xn = _stream_tile(x_ref, c_ref, i, tm, n_lat) + gt * y
    if final:
        ms = jnp.mean(xn * xn, axis=-1, keepdims=True)
        xn = xn * lax.rsqrt(ms + NORM_EPS) * fg_ref[...]
    o_ref[...] = xn


def _out_proj(mix_a, mix_b, cols, w, layer, stream, nt, mod, fg, n_lat, final):
    d = stream[0].shape[1]
    tm = TM_OUT
    half = w.shape[1] // 2
    out_rows = n_lat if final else nt
    assert out_rows % tm == 0
    kern = functools.partial(_out_proj_kernel, n_lat=n_lat, tm=tm, final=final)
    return pl.pallas_call(
        kern,
        grid=(out_rows // tm,),
        in_specs=[
            pl.BlockSpec((tm, half), lambda i: (i, cols[0])),
            pl.BlockSpec((tm, half), lambda i: (i, cols[1])),
            pl.BlockSpec((None, 2 * half, d), lambda i: (layer, 0, 0), pipeline_mode=pl.Buffered(1)),
        ] + _stream_specs(*stream, tm, n_lat) + [
            pl.BlockSpec((8, 3 * d), lambda i: (0, 0)),
            pl.BlockSpec((1, d), lambda i: (0, 0)),
        ],
        out_specs=pl.BlockSpec((tm, d), lambda i: (i, 0)),
        out_shape=jax.ShapeDtypeStruct((out_rows, d), F32),
        scratch_shapes=[pltpu.VMEM((2 * half, d), BF16)],
        compiler_params=_cparams(1),
        name="out_proj_final" if final else "out_proj",
    )(mix_a, mix_b, w, stream[0], stream[1], mod, fg)


def _rope_tables(n_lat, nc):
    rows = n_lat // GRID_W
    n_freq = HEAD_DIM // 4
    inv = ROPE_THETA ** (-jnp.arange(n_freq, dtype=F32) / n_freq)
    row_ang = jnp.arange(rows, dtype=jnp.int32).astype(F32)[:, None] * inv
    col_ang = jnp.arange(GRID_W, dtype=jnp.int32).astype(F32)[:, None] * inv
    per_row = lambda a: jnp.repeat(a, GRID_W, axis=0)
    per_col = lambda a: jnp.tile(a, (rows, 1))
    cr, sr = per_row(jnp.cos(row_ang)), per_row(jnp.sin(row_ang))
    cc, sc = per_col(jnp.cos(col_ang)), per_col(jnp.sin(col_ang))
    cos_t = jnp.concatenate([cr, cr, cc, cc], axis=-1)
    sin_t = jnp.concatenate([-sr, sr, -sc, sc], axis=-1)
    cos_t = jnp.concatenate([cos_t, jnp.ones((nc, HEAD_DIM), F32)], axis=0)
    sin_t = jnp.concatenate([sin_t, jnp.zeros((nc, HEAD_DIM), F32)], axis=0)
    return cos_t, sin_t


def _norm_rope(x, gain, cos_t, sin_t, first_half):
    ms = jnp.mean(x * x, axis=-1, keepdims=True)
    xn = x * lax.rsqrt(ms + NORM_EPS) * gain
    partner = jnp.where(first_half, pltpu.roll(xn, 3 * HEAD_DIM // 4, 1), pltpu.roll(xn, HEAD_DIM // 4, 1))
    return xn * cos_t + partner * sin_t


def _odd_proj_kernel(x_ref, c_ref, g_ref, mod_ref, w_hbm, wvt_ref, cos_ref, sin_ref, qn_ref, kn_ref,
                     q_ref, k_ref, vt_ref, gs_ref, w_ref, stage, sem, *, n_lat, tm, layer):
    @pl.when(pl.program_id(0) == 0)
    def _():
        _load_weights_bf16(w_hbm, layer, w_ref, stage, sem)

    h = _prenorm(x_ref, c_ref, g_ref, mod_ref, pl.program_id(0), tm, n_lat)
    tn = k_ref.shape[-1]
    cos_t = cos_ref[...]
    sin_t = sin_ref[...]
    lane = lax.broadcasted_iota(jnp.int32, (1, HEAD_DIM), 1)
    first_half = (lane % (HEAD_DIM // 2)) < (HEAD_DIM // 4)

    def proj(seg):
        return _dot(h, w_ref[:, seg * tn:(seg + 1) * tn])

    def rotated(seg, gain, out_ref, col0):
        acc = proj(seg)
        for hh in range(tn // HEAD_DIM):
            y = _norm_rope(acc[:, hh * HEAD_DIM:(hh + 1) * HEAD_DIM], gain, cos_t, sin_t, first_half)
            out_ref[:, col0 + hh * HEAD_DIM:col0 + (hh + 1) * HEAD_DIM] = y.astype(BF16)

    n_q = q_ref.shape[-1] // tn
    q_gain = qn_ref[...] * ((HEAD_DIM ** -0.5) * LOG2E)
    for seg in range(n_q):
        rotated(seg, q_gain, q_ref, seg * tn)
    rotated(n_q, kn_ref[...], k_ref, 0)
    v_t = _dot_nt(wvt_ref[...], h)
    vt_ref[...] = v_t.reshape(vt_ref.shape).astype(BF16)
    for seg in range(n_q):
        gs_ref[:, seg * tn:(seg + 1) * tn] = _silu(proj(n_q + 2 + seg)).astype(BF16)


def _odd_proj(stream, nt, g, mod, w, w_vt, layer, cos_t, sin_t, q_norm, k_norm, n_lat):
    d = stream[0].shape[1]
    tm = TM_PROJ
    qw = GQA_HEADS * HEAD_DIM
    kw = GQA_KV_HEADS * HEAD_DIM
    assert w.shape[2] == 2 * qw + 2 * kw and nt % tm == 0 and n_lat % tm == 0
    kern = functools.partial(_odd_proj_kernel, n_lat=n_lat, tm=tm, layer=layer)
    rows = lambda width: pl.BlockSpec((tm, width), lambda i: (i, 0))
    vec = lambda width: pl.BlockSpec((1, width), lambda i: (0, 0))
    return pl.pallas_call(
        kern,
        grid=(nt // tm,),
        in_specs=_stream_specs(*stream, tm, n_lat) + [
            vec(d),
            pl.BlockSpec((8, 3 * d), lambda i: (0, 0)),
            pl.BlockSpec(memory_space=pl.ANY),
            pl.BlockSpec((kw, d), lambda i: (0, 0)),
            rows(HEAD_DIM), rows(HEAD_DIM), vec(HEAD_DIM), vec(HEAD_DIM),
        ],
        out_specs=[rows(qw), rows(kw),
                   pl.BlockSpec((GQA_KV_HEADS, None, HEAD_DIM, tm), lambda i: (0, i, 0, 0)),
                   rows(qw)],
        out_shape=[
            jax.ShapeDtypeStruct((nt, qw), BF16),
            jax.ShapeDtypeStruct((nt, kw), BF16),
            jax.ShapeDtypeStruct((GQA_KV_HEADS, nt // tm, HEAD_DIM, tm), BF16),
            jax.ShapeDtypeStruct((nt, qw), BF16),
        ],
        scratch_shapes=_weight_scratch(w),
        compiler_params=_cparams(1),
        name="odd_proj",
    )(stream[0], stream[1], g, mod, w, w_vt, cos_t, sin_t,
      q_norm.reshape(1, HEAD_DIM), k_norm.reshape(1, HEAD_DIM))


def _gqa_kernel(bound_ref, q_ref, k_ref, vt_ref, g_ref, o_ref, *, n_lat, nc):
    b = pl.program_id(1)
    tq = q_ref.shape[0]
    vb = vt_ref.shape[-1]
    bound = bound_ref[0]
    is_lat = b * tq < n_lat
    q = jnp.concatenate([q_ref[:, h * HEAD_DIM:(h + 1) * HEAD_DIM] for h in range(GQA_GROUP)], axis=0)

    def vt_chunk(first_block, n_blocks):
        return jnp.concatenate([vt_ref[first_block + j] for j in range(n_blocks)], axis=1)

    def write_out(o_t):
        o = o_t.T
        for h in range(GQA_GROUP):
            sl = slice(h * HEAD_DIM, (h + 1) * HEAD_DIM)
            o_ref[:, sl] = (o[h * tq:(h + 1) * tq, :] * g_ref[:, sl].astype(F32)).astype(BF16)

    def fixed_shift_chunk(row0, n_rows):
        p = jnp.exp2(_dot_nt(k_ref[row0:row0 + n_rows, :], q) - bound)
        partial = jnp.sum(p.reshape(n_rows // 8, 8, p.shape[-1]), axis=0)
        return _dot(vt_chunk(row0 // vb, n_rows // vb), p.astype(BF16)), partial

    def finish(acc, partial):
        write_out(acc / jnp.sum(partial, axis=0, keepdims=True))

    small = bound <= GQA_SAFE_BOUND

    @pl.when(jnp.logical_and(small, is_lat))
    def _():
        acc, partial = fixed_shift_chunk(n_lat, nc)
        for c in range(n_lat // GQA_TK):
            acc_c, partial_c = fixed_shift_chunk(c * GQA_TK, GQA_TK)
            acc, partial = acc + acc_c, partial + partial_c
        finish(acc, partial)

    @pl.when(jnp.logical_and(small, jnp.logical_not(is_lat)))
    def _():
        finish(*fixed_shift_chunk(n_lat, nc))

    @pl.when(bound > GQA_SAFE_BOUND)
    def _():
        cols = GQA_GROUP * tq
        per = GQA_TK_ONLINE // vb

        def step(carry, k_chunk, vt):
            m, l, acc = carry
            s = _dot_nt(k_chunk, q)
            m_new = jnp.maximum(m, jnp.max(s, axis=0, keepdims=True))
            alpha = jnp.exp2(m - m_new)
            p = jnp.exp2(s - m_new)
            l = alpha * l + jnp.sum(p, axis=0, keepdims=True)
            acc = alpha * acc + _dot(vt, p.astype(BF16))
            return m_new, l, acc

        init = (jnp.full((1, cols), NEG_BIG, F32), jnp.zeros((1, cols), F32), jnp.zeros((HEAD_DIM, cols), F32))
        carry = step(init, k_ref[n_lat:n_lat + nc, :], vt_chunk(n_lat // vb, nc // vb))

        def body(c, carry):
            start = pl.multiple_of(c * GQA_TK_ONLINE, GQA_TK_ONLINE)
            vt = jnp.concatenate([vt_ref[c * per + j] for j in range(per)], axis=1)
            return step(carry, k_ref[pl.ds(start, GQA_TK_ONLINE), :], vt)

        m, l, acc = lax.fori_loop(0, jnp.where(is_lat, n_lat // GQA_TK_ONLINE, 0), body, carry)
        write_out(acc / l)


def _gqa_attention(q, k, vt, gs, q_norm, k_norm, n_lat):
    nt = q.shape[0]
    nc = nt - n_lat
    tq = GQA_TQ
    gw = GQA_GROUP * HEAD_DIM
    n_vb, vb = vt.shape[1], vt.shape[3]
    assert nt % tq == 0 and n_lat % tq == 0 and n_lat % GQA_TK == 0 and n_lat % GQA_TK_ONLINE == 0
    assert n_vb * vb == nt and n_lat % vb == 0 and nc % vb == 0 and GQA_TK_ONLINE % vb == 0
    bound = (1.01 * HEAD_DIM * (HEAD_DIM ** -0.5) * LOG2E) * jnp.max(jnp.abs(q_norm)) * jnp.max(jnp.abs(k_norm))
    kern = functools.partial(_gqa_kernel, n_lat=n_lat, nc=nc)
    return pl.pallas_call(
        kern,
        grid=(GQA_KV_HEADS, nt // tq),
        in_specs=[
            pl.BlockSpec(memory_space=pltpu.SMEM),
            pl.BlockSpec((tq, gw), lambda h, b: (b, h)),
            pl.BlockSpec((nt, HEAD_DIM), lambda h, b: (0, h)),
            pl.BlockSpec((None, n_vb, HEAD_DIM, vb), lambda h, b: (h, 0, 0, 0)),
            pl.BlockSpec((tq, gw), lambda h, b: (b, h)),
        ],
        out_specs=pl.BlockSpec((tq, gw), lambda h, b: (b, h)),
        out_shape=jax.ShapeDtypeStruct((nt, GQA_HEADS * HEAD_DIM), BF16),
        compiler_params=_cparams(2),
        name="gqa_attention",
    )(bound.reshape(1).astype(F32), q, k, vt, gs)


def kernel(x, c, ctx, c_ctx, norm_g, w_mod, b_mod, e_w_in, e_rpb, e_dw_w, e_dw_b, e_ln_g, e_ln_b, e_w_out,
           o_w_in, o_q_norm, o_k_norm, o_w_out, final_norm_g):
    batch, n_lat, d = x.shape
    nc = ctx.shape[1]
    depth = norm_g.shape[0]
    assert batch == 1
    rows = n_lat // GRID_W

    nt = n_lat + nc
    stream = (x[0], ctx[0], 0)
    cvec = jnp.zeros((8, d), F32).at[0].set(c[0]).at[1].set(c_ctx)
    mods = _modulation(cvec, w_mod, b_mod)
    cos_t, sin_t = _rope_tables(n_lat, nc)
    fg = final_norm_g.reshape(1, d)

    for l in range(depth):
        i = l // 2
        g = norm_g[l].reshape(1, d)
        mod = mods[l]
        final = l == depth - 1
        if l % 2 == 0:
            qkv, ga, u, gb = _even_proj(stream, nt, g, mod, e_w_in, i, n_lat)
            mix_a = _na_attention(qkv, ga, _na_bias_table(e_rpb[i], rows, nc), n_lat)
            mix_b = _conformer_conv(u, gb, e_dw_w[i], e_dw_b[i], e_ln_g[i], e_ln_b[i], n_lat)
            xs = _out_proj(mix_a, mix_b, (0, 0), e_w_out, i, stream, nt, mod, fg, n_lat, final)
        else:
            v0 = (GQA_HEADS + GQA_KV_HEADS) * HEAD_DIM
            w_v = lax.optimization_barrier(o_w_in[i, :, v0:v0 + GQA_KV_HEADS * HEAD_DIM])
            w_vt = w_v.T.astype(BF16)
            q, k, vt, gs = _odd_proj(stream, nt, g, mod, o_w_in, w_vt, i, cos_t, sin_t,
                                     o_q_norm[i], o_k_norm[i], n_lat)
            mix = _gqa_attention(q, k, vt, gs, o_q_norm[i], o_k_norm[i], n_lat)
            xs = _out_proj(mix, mix, (0, 1), o_w_out, i, stream, nt, mod, fg, n_lat, final)
        stream = (xs, xs, n_lat)
    return xs[None]
```

```python
import functools

import numpy as np
import jax
import jax.numpy as jnp
from jax import lax
from jax.experimental import pallas as pl
from jax.experimental.pallas import tpu as pltpu

F32 = jnp.float32
BF16 = jnp.bfloat16

GRID_W = 64
HEAD_DIM = 128
NORM_EPS = 1e-6
NA_HEADS = 8
NA_WIN_ROWS = 8
NA_WIN_COLS = 16
CONV_KERNEL = 31
GQA_HEADS = 16
GQA_KV_HEADS = 4
GQA_GROUP = GQA_HEADS // GQA_KV_HEADS
ROPE_THETA = 10000.0
LOG2E = 1.4426950408889634

LANES = 128
V7X_VMEM_BYTES = 64 * 1024 * 1024
VMEM_LIMIT = V7X_VMEM_BYTES - 8 * 1024 * 1024

TM_PROJ = 256
TM_OUT = 256
NA_ROWS = 4
NA_SLAB = NA_WIN_ROWS + NA_ROWS
NA_TQ = NA_ROWS * GRID_W
NA_HP = 4
CONV_T = 256
CONV_HALO = 16
CONV_CH = 32
GQA_TQ = 256
GQA_TK = 2048
GQA_TK_ONLINE = 512
GQA_SAFE_BOUND = 60.0
NEG_BIG = -1e30


def _cparams(n_axes):
    return pltpu.CompilerParams(dimension_semantics=("arbitrary",) * n_axes, vmem_limit_bytes=VMEM_LIMIT)


def _sigmoid(x):
    return 1.0 / (1.0 + jnp.exp(-x))


def _silu(x):
    return x * _sigmoid(x)


def _dot(a, b):
    return jnp.dot(a, b, preferred_element_type=F32)


def _dot_nt(a, b):
    return lax.dot_general(a, b, (((1,), (1,)), ((), ())), preferred_element_type=F32)


def _stream_specs(lat_src, ctx_src, ctx_row0, tm, n_lat):
    d = lat_src.shape[1]
    nlt = n_lat // tm
    n_ctx_tiles = (ctx_src.shape[0] - ctx_row0) // tm
    assert n_lat % tm == 0 and ctx_row0 % tm == 0 and n_ctx_tiles >= 1
    ctx_mode = dict(pipeline_mode=pl.Buffered(1)) if n_ctx_tiles == 1 else {}
    return [pl.BlockSpec((tm, d), lambda i: (jnp.minimum(i, nlt - 1), 0)),
            pl.BlockSpec((tm, d), lambda i: (ctx_row0 // tm + jnp.clip(i - nlt, 0, n_ctx_tiles - 1), 0), **ctx_mode)]


def _stream_tile(x_ref, c_ref, tile_idx, tm, n_lat):
    return jnp.where(tile_idx * tm >= n_lat, c_ref[...], x_ref[...])


def _prenorm(x_ref, c_ref, g_ref, mod_ref, tile_idx, tm, n_lat):
    d = x_ref.shape[-1]
    x = _stream_tile(x_ref, c_ref, tile_idx, tm, n_lat)
    ms = jnp.mean(x * x, axis=-1, keepdims=True)
    row = (tile_idx * tm >= n_lat).astype(jnp.int32)
    mod = mod_ref[pl.ds(row, 1), :]
    a = g_ref[...] * (1.0 + mod[:, d:2 * d])
    return (x * lax.rsqrt(ms + NORM_EPS) * a + mod[:, 0:d]).astype(BF16)


W_STAGE_ROWS = 64
W_STAGE_SLOTS = 4


def _load_weights_bf16(w_hbm, layer, w_bf, stage, sem):
    slots, rows = stage.shape[0], stage.shape[1]
    n_chunks = w_bf.shape[0] // rows

    def copy(c):
        return pltpu.make_async_copy(w_hbm.at[layer, pl.ds(c * rows, rows), :],
                                     stage.at[c % slots], sem.at[c % slots])

    for c in range(min(slots - 1, n_chunks)):
        copy(c).start()
    for c in range(n_chunks):
        copy(c).wait()
        if c + slots - 1 < n_chunks:
            copy(c + slots - 1).start()
        w_bf[c * rows:(c + 1) * rows, :] = stage[c % slots].astype(BF16)


def _weight_scratch(w):
    d, n = w.shape[1:]
    assert d % W_STAGE_ROWS == 0
    return [pltpu.VMEM((d, n), BF16), pltpu.VMEM((W_STAGE_SLOTS, W_STAGE_ROWS, n), F32),
            pltpu.SemaphoreType.DMA((W_STAGE_SLOTS,))]


def _mod_kernel(c_ref, w_ref, b_ref, o_ref):
    s = _silu(c_ref[...]).astype(BF16)
    o_ref[...] = _dot(s, w_ref[...].astype(BF16)) + b_ref[...]


def _modulation(cvec, w_mod, b_mod):
    depth, d, d3 = w_mod.shape
    tn = 768
    return pl.pallas_call(
        _mod_kernel,
        grid=(depth, d3 // tn),
        in_specs=[
            pl.BlockSpec((8, d), lambda l, j: (0, 0)),
            pl.BlockSpec((None, d, tn), lambda l, j: (l, 0, j)),
            pl.BlockSpec((None, 1, tn), lambda l, j: (l, 0, j)),
        ],
        out_specs=pl.BlockSpec((None, 8, tn), lambda l, j: (l, 0, j)),
        out_shape=jax.ShapeDtypeStruct((depth, 8, d3), F32),
        compiler_params=_cparams(2),
        name="modulation",
    )(cvec, w_mod, b_mod.reshape(depth, 1, d3))


def _even_proj_kernel(x_ref, c_ref, g_ref, mod_ref, w_hbm, qkv_ref, ga_ref, u_ref, gb_ref, w_ref, stage, sem,
                      *, n_lat, tm, layer):
    @pl.when(pl.program_id(0) == 0)
    def _():
        _load_weights_bf16(w_hbm, layer, w_ref, stage, sem)

    h = _prenorm(x_ref, c_ref, g_ref, mod_ref, pl.program_id(0), tm, n_lat)
    cw = ga_ref.shape[-1]

    def proj(seg):
        return _dot(h, w_ref[:, seg * cw:(seg + 1) * cw])

    for seg in range(3):
        qkv_ref[:, seg * cw:(seg + 1) * cw] = proj(seg).astype(BF16)
    ga_ref[...] = _silu(proj(3)).astype(BF16)
    u_ref[...] = proj(4) * _sigmoid(proj(5))
    gb_ref[...] = _silu(proj(6)).astype(BF16)


def _even_proj(stream, nt, g, mod, w, layer, n_lat):
    d = stream[0].shape[1]
    tm = TM_PROJ
    cw = w.shape[2] // 7
    assert nt % tm == 0 and n_lat % tm == 0
    kern = functools.partial(_even_proj_kernel, n_lat=n_lat, tm=tm, layer=layer)
    rows = lambda width: pl.BlockSpec((tm, width), lambda i: (i, 0))
    return pl.pallas_call(
        kern,
        grid=(nt // tm,),
        in_specs=_stream_specs(*stream, tm, n_lat) + [
            pl.BlockSpec((1, d), lambda i: (0, 0)),
            pl.BlockSpec((8, 3 * d), lambda i: (0, 0)),
            pl.BlockSpec(memory_space=pl.ANY),
        ],
        out_specs=[rows(3 * cw), rows(cw), rows(cw), rows(cw)],
        out_shape=[
            jax.ShapeDtypeStruct((nt, 3 * cw), BF16),
            jax.ShapeDtypeStruct((nt, cw), BF16),
            jax.ShapeDtypeStruct((nt, cw), F32),
            jax.ShapeDtypeStruct((nt, cw), BF16),
        ],
        scratch_shapes=_weight_scratch(w),
        compiler_params=_cparams(1),
        name="even_proj",
    )(stream[0], stream[1], g, mod, w)


def _na_bias_kernel(rp_ref, o_ref, *, rows, nc):
    pattern = pl.program_id(0)
    lane = lax.broadcasted_iota(jnp.int32, (GRID_W, 2 * GRID_W), 1)
    qc = lax.broadcasted_iota(jnp.int32, (GRID_W, 2 * GRID_W), 0)
    kc = lane % GRID_W
    cs = jnp.clip(qc - NA_WIN_COLS // 2, 0, GRID_W - NA_WIN_COLS)
    col_ok = jnp.logical_and(kc >= cs, kc < cs + NA_WIN_COLS)
    left_half = lane < GRID_W
    masked = jnp.full((GRID_W, 2 * GRID_W), NEG_BIG, F32)

    def toeplitz(dr, right):
        x = jnp.broadcast_to(rp_ref[dr:dr + 1, :], (GRID_W, 2 * GRID_W))
        return pltpu.roll(x, 1 if right else GRID_W + 1, 1, stride=1, stride_axis=0)

    for pat_id, r0 in enumerate((0, 2 * NA_ROWS, rows - NA_ROWS)):
        @pl.when(pattern == pat_id)
        def _(r0=r0):
            ks = int(np.clip(r0 - NA_WIN_ROWS // 2, 0, rows - NA_SLAB))
            for r_local in range(NA_ROWS):
                r = r0 + r_local
                rs = int(np.clip(r - NA_WIN_ROWS // 2, 0, rows - NA_WIN_ROWS))
                for jp in range(NA_SLAB // 2):
                    halves = [toeplitz(kr - r + NA_WIN_ROWS - 1, side == 1) if rs <= kr < rs + NA_WIN_ROWS else None
                              for side, kr in enumerate((ks + 2 * jp, ks + 2 * jp + 1))]
                    if halves[0] is None and halves[1] is None:
                        tile = masked
                    else:
                        left = masked if halves[0] is None else halves[0]
                        right = masked if halves[1] is None else halves[1]
                        tile = jnp.where(col_ok, jnp.where(left_half, left, right) * LOG2E, NEG_BIG)
                    o_ref[r_local * GRID_W:(r_local + 1) * GRID_W, jp * 2 * GRID_W:(jp + 1) * 2 * GRID_W] = tile
    o_ref[:, NA_SLAB * GRID_W:] = jnp.zeros((NA_TQ, nc), F32)


def _na_bias_table(rpb, rows, nc):
    heads = rpb.shape[0]
    assert NA_SLAB % 2 == 0 and 2 * GRID_W == LANES and rpb.shape[1] == 2 * NA_WIN_ROWS - 1
    pad = GRID_W - NA_WIN_COLS
    rp = jnp.pad(rpb.astype(F32), ((0, 0), (0, 1), (pad, LANES - pad - rpb.shape[2])))
    n_keys = NA_SLAB * GRID_W + nc
    kern = functools.partial(_na_bias_kernel, rows=rows, nc=nc)
    return pl.pallas_call(
        kern,
        grid=(3, heads),
        in_specs=[pl.BlockSpec((None, 2 * NA_WIN_ROWS, LANES), lambda p, h: (h, 0, 0))],
        out_specs=pl.BlockSpec((None, None, NA_TQ, n_keys), lambda p, h: (p, h, 0, 0)),
        out_shape=jax.ShapeDtypeStruct((3, heads, NA_TQ, n_keys), F32),
        compiler_params=_cparams(2),
        name="na_bias",
    )(rp)


def _na_kernel(q_ref, k_ref, v_ref, bias_ref, ga_ref, o_ref, *, n_lat, nc, rows):
    b = pl.program_id(1)
    nb_lat = rows // NA_ROWS
    scale = (HEAD_DIM ** -0.5) * LOG2E

    def attend(h, keys, values, bias):
        sl = slice(h * HEAD_DIM, (h + 1) * HEAD_DIM)
        s = _dot_nt(q_ref[:, sl], keys) * scale
        if bias is not None:
            s = s + bias
        p = jnp.exp2(s - jnp.max(s, axis=-1, keepdims=True)).astype(BF16)
        o = _dot(p, jnp.concatenate([values, jnp.ones_like(values)], axis=1))
        gate = ga_ref[:, sl].astype(F32)
        o_ref[:, sl] = (o[:, :HEAD_DIM] / o[:, HEAD_DIM:] * gate).astype(BF16)

    @pl.when(b < nb_lat)
    def _():
        ks = jnp.clip(b * NA_ROWS - NA_WIN_ROWS // 2, 0, rows - NA_SLAB)
        start = pl.multiple_of(ks * GRID_W, GRID_W)
        for h in range(NA_HP):
            sl = slice(h * HEAD_DIM, (h + 1) * HEAD_DIM)
            keys = jnp.concatenate([k_ref[pl.ds(start, NA_SLAB * GRID_W), sl], k_ref[n_lat:n_lat + nc, sl]], axis=0)
            values = jnp.concatenate([v_ref[pl.ds(start, NA_SLAB * GRID_W), sl], v_ref[n_lat:n_lat + nc, sl]], axis=0)
            attend(h, keys, values, bias_ref[h])

    @pl.when(b >= nb_lat)
    def _():
        for h in range(NA_HP):
            sl = slice(h * HEAD_DIM, (h + 1) * HEAD_DIM)
            attend(h, k_ref[n_lat:n_lat + nc, sl], v_ref[n_lat:n_lat + nc, sl], None)


def _na_attention(qkv, ga, bias_tab, n_lat):
    nt = qkv.shape[0]
    nc = nt - n_lat
    rows = n_lat // GRID_W
    assert nc == NA_TQ and rows % NA_ROWS == 0 and rows >= NA_SLAB
    nb = nt // NA_TQ
    nb_lat = rows // NA_ROWS
    n_keys = NA_SLAB * GRID_W + nc
    hw = NA_HP * HEAD_DIM
    groups = NA_HEADS // NA_HP

    def pat(b):
        return jnp.where(b == 0, 0, jnp.where(b >= nb_lat - 1, 2, 1))

    kern = functools.partial(_na_kernel, n_lat=n_lat, nc=nc, rows=rows)
    return pl.pallas_call(
        kern,
        grid=(groups, nb),
        in_specs=[
            pl.BlockSpec((NA_TQ, hw), lambda h, b: (b, h)),
            pl.BlockSpec((nt, hw), lambda h, b: (0, groups + h)),
            pl.BlockSpec((nt, hw), lambda h, b: (0, 2 * groups + h)),
            pl.BlockSpec((None, NA_HP, NA_TQ, n_keys), lambda h, b: (pat(b), h, 0, 0)),
            pl.BlockSpec((NA_TQ, hw), lambda h, b: (b, h)),
        ],
        out_specs=pl.BlockSpec((NA_TQ, hw), lambda h, b: (b, h)),
        out_shape=jax.ShapeDtypeStruct((nt, NA_HEADS * HEAD_DIM), BF16),
        compiler_params=_cparams(2),
        name="na_attention",
    )(qkv, qkv, qkv, bias_tab, ga)


def _conv_kernel(um_ref, up_ref, un_ref, gb_ref, w_ref, b_ref, lg_ref, lb_ref, o_ref, ubuf, shifted, *, n_lat, nt):
    t = pl.program_id(0)
    start = t * CONV_T
    prev_ok = jnp.logical_and(start != 0, start != n_lat)
    next_ok = jnp.logical_and(start + CONV_T != n_lat, start + CONV_T != nt)
    ubuf[0:CONV_HALO, :] = jnp.where(prev_ok, up_ref[...], 0.0)
    ubuf[CONV_HALO:CONV_HALO + CONV_T, :] = um_ref[...]
    ubuf[CONV_HALO + CONV_T:, :] = jnp.where(next_ok, un_ref[...], 0.0)
    span = CONV_T + 2 * CONV_HALO - 8
    for b in range(1, 8):
        shifted[b - 1, :, :] = ubuf[b:b + span, :]
    off = CONV_HALO - CONV_KERNEL // 2
    for r0 in range(0, CONV_T, CONV_CH):
        cw = um_ref.shape[-1]
        acc = jnp.zeros((CONV_CH // 8, 8, cw), F32) + b_ref[...]
        for j in range(CONV_KERNEL):
            a, b = divmod(off + j, 8)
            rows = slice(r0 + 8 * a, r0 + 8 * a + CONV_CH)
            tap = ubuf[rows, :] if b == 0 else shifted[b - 1, rows, :]
            acc = acc + w_ref[j][None] * tap.reshape(CONV_CH // 8, 8, cw)
        acc = acc.reshape(CONV_CH, cw)
        mu = jnp.mean(acc, axis=-1, keepdims=True)
        xc = acc - mu
        var = jnp.mean(xc * xc, axis=-1, keepdims=True)
        y = xc * lax.rsqrt(var + NORM_EPS) * lg_ref[...] + lb_ref[...]
        y = _silu(y) * gb_ref[r0:r0 + CONV_CH, :].astype(F32)
        o_ref[r0:r0 + CONV_CH, :] = y.astype(BF16)


def _conformer_conv(u, gb, dw_w, dw_b, ln_g, ln_b, n_lat):
    nt, cw = u.shape
    assert nt % CONV_T == 0 and n_lat % CONV_T == 0
    hb = CONV_T // CONV_HALO
    n_halo_blocks = nt // CONV_HALO
    w_pad = jnp.broadcast_to(jnp.zeros((32, cw), F32).at[:CONV_KERNEL].set(dw_w)[:, None, :], (32, 8, cw))
    kern = functools.partial(_conv_kernel, n_lat=n_lat, nt=nt)
    vec = lambda: pl.BlockSpec((1, cw), lambda t: (0, 0))
    return pl.pallas_call(
        kern,
        grid=(nt // CONV_T,),
        in_specs=[
            pl.BlockSpec((CONV_T, cw), lambda t: (t, 0)),
            pl.BlockSpec((CONV_HALO, cw), lambda t: (jnp.maximum(t * hb - 1, 0), 0)),
            pl.BlockSpec((CONV_HALO, cw), lambda t: (jnp.minimum((t + 1) * hb, n_halo_blocks - 1), 0)),
            pl.BlockSpec((CONV_T, cw), lambda t: (t, 0)),
            pl.BlockSpec((32, 8, cw), lambda t: (0, 0, 0)),
            vec(), vec(), vec(),
        ],
        out_specs=pl.BlockSpec((CONV_T, cw), lambda t: (t, 0)),
        out_shape=jax.ShapeDtypeStruct((nt, cw), BF16),
        scratch_shapes=[pltpu.VMEM((CONV_T + 2 * CONV_HALO, cw), F32),
                        pltpu.VMEM((7, CONV_T + 2 * CONV_HALO - 8, cw), F32)],
        compiler_params=_cparams(1),
        name="conformer_conv",
    )(u, u, u, gb, w_pad, dw_b.reshape(1, cw), ln_g.reshape(1, cw), ln_b.reshape(1, cw))


def _out_proj_kernel(a_ref, b_ref, w32_ref, x_ref, c_ref, mod_ref, fg_ref, o_ref, w_ref, *, n_lat, tm, final):
    i = pl.program_id(0)

    @pl.when(i == 0)
    def _():
        w_ref[...] = w32_ref[...].astype(BF16)

    half = a_ref.shape[-1]
    y = _dot(a_ref[...], w_ref[0:half, :]) + _dot(b_ref[...], w_ref[half:, :])
    d = y.shape[-1]
    row = (i * tm >= n_lat).astype(jnp.int32)
    gt = mod_ref[pl.ds(row, 1), 2 * d:3 * d]
    xn = _stream_tile(x_ref, c_ref, i, tm, n_lat) + gt * y
    if final:
        ms = jnp.mean(xn * xn, axis=-1, keepdims=True)
        xn = xn * lax.rsqrt(ms + NORM_EPS) * fg_ref[...]
    o_ref[...] = xn


def _out_proj(mix_a, mix_b, cols, w, layer, stream, nt, mod, fg, n_lat, final):
    d = stream[0].shape[1]
    tm = TM_OUT
    half = w.shape[1] // 2
    out_rows = n_lat if final else nt
    assert out_rows % tm == 0
    kern = functools.partial(_out_proj_kernel, n_lat=n_lat, tm=tm, final=final)
    return pl.pallas_call(
        kern,
        grid=(out_rows // tm,),
        in_specs=[
            pl.BlockSpec((tm, half), lambda i: (i, cols[0])),
            pl.BlockSpec((tm, half), lambda i: (i, cols[1])),
            pl.BlockSpec((None, 2 * half, d), lambda i: (layer, 0, 0), pipeline_mode=pl.Buffered(1)),
        ] + _stream_specs(*stream, tm, n_lat) + [
            pl.BlockSpec((8, 3 * d), lambda i: (0, 0)),
            pl.BlockSpec((1, d), lambda i: (0, 0)),
        ],
        out_specs=pl.BlockSpec((tm, d), lambda i: (i, 0)),
        out_shape=jax.ShapeDtypeStruct((out_rows, d), F32),
        scratch_shapes=[pltpu.VMEM((2 * half, d), BF16)],
        compiler_params=_cparams(1),
        name="out_proj_final" if final else "out_proj",
    )(mix_a, mix_b, w, stream[0], stream[1], mod, fg)


def _rope_tables(n_lat, nc):
    rows = n_lat // GRID_W
    n_freq = HEAD_DIM // 4
    inv = ROPE_THETA ** (-jnp.arange(n_freq, dtype=F32) / n_freq)
    row_ang = jnp.arange(rows, dtype=jnp.int32).astype(F32)[:, None] * inv
    col_ang = jnp.arange(GRID_W, dtype=jnp.int32).astype(F32)[:, None] * inv
    per_row = lambda a: jnp.repeat(a, GRID_W, axis=0)
    per_col = lambda a: jnp.tile(a, (rows, 1))
    cr, sr = per_row(jnp.cos(row_ang)), per_row(jnp.sin(row_ang))
    cc, sc = per_col(jnp.cos(col_ang)), per_col(jnp.sin(col_ang))
    cos_t = jnp.concatenate([cr, cr, cc, cc], axis=-1)
    sin_t = jnp.concatenate([-sr, sr, -sc, sc], axis=-1)
    cos_t = jnp.concatenate([cos_t, jnp.ones((nc, HEAD_DIM), F32)], axis=0)
    sin_t = jnp.concatenate([sin_t, jnp.zeros((nc, HEAD_DIM), F32)], axis=0)
    return cos_t, sin_t


def _norm_rope(x, gain, cos_t, sin_t, first_half):
    ms = jnp.mean(x * x, axis=-1, keepdims=True)
    xn = x * lax.rsqrt(ms + NORM_EPS) * gain
    partner = jnp.where(first_half, pltpu.roll(xn, 3 * HEAD_DIM // 4, 1), pltpu.roll(xn, HEAD_DIM // 4, 1))
    return xn * cos_t + partner * sin_t


def _odd_proj_kernel(x_ref, c_ref, g_ref, mod_ref, w_hbm, wvt_ref, cos_ref, sin_ref, qn_ref, kn_ref,
                     q_ref, k_ref, vt_ref, gs_ref, w_ref, stage, sem, *, n_lat, tm, layer):
    @pl.when(pl.program_id(0) == 0)
    def _():
        _load_weights_bf16(w_hbm, layer, w_ref, stage, sem)

    h = _prenorm(x_ref, c_ref, g_ref, mod_ref, pl.program_id(0), tm, n_lat)
    tn = k_ref.shape[-1]
    cos_t = cos_ref[...]
    sin_t = sin_ref[...]
    lane = lax.broadcasted_iota(jnp.int32, (1, HEAD_DIM), 1)
    first_half = (lane % (HEAD_DIM // 2)) < (HEAD_DIM // 4)

    def proj(seg):
        return _dot(h, w_ref[:, seg * tn:(seg + 1) * tn])

    def rotated(seg, gain, out_ref, col0):
        acc = proj(seg)
        for hh in range(tn // HEAD_DIM):
            y = _norm_rope(acc[:, hh * HEAD_DIM:(hh + 1) * HEAD_DIM], gain, cos_t, sin_t, first_half)
            out_ref[:, col0 + hh * HEAD_DIM:col0 + (hh + 1) * HEAD_DIM] = y.astype(BF16)

    n_q = q_ref.shape[-1] // tn
    q_gain = qn_ref[...] * ((HEAD_DIM ** -0.5) * LOG2E)
    for seg in range(n_q):
        rotated(seg, q_gain, q_ref, seg * tn)
    rotated(n_q, kn_ref[...], k_ref, 0)
    v_t = _dot_nt(wvt_ref[...], h)
    vt_ref[...] = v_t.reshape(vt_ref.shape).astype(BF16)
    for seg in range(n_q):
        gs_ref[:, seg * tn:(seg + 1) * tn] = _silu(proj(n_q + 2 + seg)).astype(BF16)


def _odd_proj(stream, nt, g, mod, w, w_vt, layer, cos_t, sin_t, q_norm, k_norm, n_lat):
    d = stream[0].shape[1]
    tm = TM_PROJ
    qw = GQA_HEADS * HEAD_DIM
    kw = GQA_KV_HEADS * HEAD_DIM
    assert w.shape[2] == 2 * qw + 2 * kw and nt % tm == 0 and n_lat % tm == 0
    kern = functools.partial(_odd_proj_kernel, n_lat=n_lat, tm=tm, layer=layer)
    rows = lambda width: pl.BlockSpec((tm, width), lambda i: (i, 0))
    vec = lambda width: pl.BlockSpec((1, width), lambda i: (0, 0))
    return pl.pallas_call(
        kern,
        grid=(nt // tm,),
        in_specs=_stream_specs(*stream, tm, n_lat) + [
            vec(d),
            pl.BlockSpec((8, 3 * d), lambda i: (0, 0)),
            pl.BlockSpec(memory_space=pl.ANY),
            pl.BlockSpec((kw, d), lambda i: (0, 0)),
            rows(HEAD_DIM), rows(HEAD_DIM), vec(HEAD_DIM), vec(HEAD_DIM),
        ],
        out_specs=[rows(qw), rows(kw),
                   pl.BlockSpec((GQA_KV_HEADS, None, HEAD_DIM, tm), lambda i: (0, i, 0, 0)),
                   rows(qw)],
        out_shape=[
            jax.ShapeDtypeStruct((nt, qw), BF16),
            jax.ShapeDtypeStruct((nt, kw), BF16),
            jax.ShapeDtypeStruct((GQA_KV_HEADS, nt // tm, HEAD_DIM, tm), BF16),
            jax.ShapeDtypeStruct((nt, qw), BF16),
        ],
        scratch_shapes=_weight_scratch(w),
        compiler_params=_cparams(1),
        name="odd_proj",
    )(stream[0], stream[1], g, mod, w, w_vt, cos_t, sin_t,
      q_norm.reshape(1, HEAD_DIM), k_norm.reshape(1, HEAD_DIM))


def _gqa_kernel(bound_ref, q_ref, k_ref, vt_ref, g_ref, o_ref, *, n_lat, nc):
    b = pl.program_id(1)
    tq = q_ref.shape[0]
    vb = vt_ref.shape[-1]
    bound = bound_ref[0]
    is_lat = b * tq < n_lat
    q = jnp.concatenate([q_ref[:, h * HEAD_DIM:(h + 1) * HEAD_DIM] for h in range(GQA_GROUP)], axis=0)

    def vt_chunk(first_block, n_blocks):
        return jnp.concatenate([vt_ref[first_block + j] for j in range(n_blocks)], axis=1)

    def write_out(o_t):
        o = o_t.T
        for h in range(GQA_GROUP):
            sl = slice(h * HEAD_DIM, (h + 1) * HEAD_DIM)
            o_ref[:, sl] = (o[h * tq:(h + 1) * tq, :] * g_ref[:, sl].astype(F32)).astype(BF16)

    def fixed_shift_chunk(row0, n_rows):
        p = jnp.exp2(_dot_nt(k_ref[row0:row0 + n_rows, :], q) - bound)
        partial = jnp.sum(p.reshape(n_rows // 8, 8, p.shape[-1]), axis=0)
        return _dot(vt_chunk(row0 // vb, n_rows // vb), p.astype(BF16)), partial

    def finish(acc, partial):
        write_out(acc / jnp.sum(partial, axis=0, keepdims=True))

    small = bound <= GQA_SAFE_BOUND

    @pl.when(jnp.logical_and(small, is_lat))
    def _():
        acc, partial = fixed_shift_chunk(n_lat, nc)
        for c in range(n_lat // GQA_TK):
            acc_c, partial_c = fixed_shift_chunk(c * GQA_TK, GQA_TK)
            acc, partial = acc + acc_c, partial + partial_c
        finish(acc, partial)

    @pl.when(jnp.logical_and(small, jnp.logical_not(is_lat)))
    def _():
        finish(*fixed_shift_chunk(n_lat, nc))

    @pl.when(bound > GQA_SAFE_BOUND)
    def _():
        cols = GQA_GROUP * tq
        per = GQA_TK_ONLINE // vb

        def step(carry, k_chunk, vt):
            m, l, acc = carry
            s = _dot_nt(k_chunk, q)
            m_new = jnp.maximum(m, jnp.max(s, axis=0, keepdims=True))
            alpha = jnp.exp2(m - m_new)
            p = jnp.exp2(s - m_new)
            l = alpha * l + jnp.sum(p, axis=0, keepdims=True)
            acc = alpha * acc + _dot(vt, p.astype(BF16))
            return m_new, l, acc

        init = (jnp.full((1, cols), NEG_BIG, F32), jnp.zeros((1, cols), F32), jnp.zeros((HEAD_DIM, cols), F32))
        carry = step(init, k_ref[n_lat:n_lat + nc, :], vt_chunk(n_lat // vb, nc // vb))

        def body(c, carry):
            start = pl.multiple_of(c * GQA_TK_ONLINE, GQA_TK_ONLINE)
            vt = jnp.concatenate([vt_ref[c * per + j] for j in range(per)], axis=1)
            return step(carry, k_ref[pl.ds(start, GQA_TK_ONLINE), :], vt)

        m, l, acc = lax.fori_loop(0, jnp.where(is_lat, n_lat // GQA_TK_ONLINE, 0), body, carry)
        write_out(acc / l)


def _gqa_attention(q, k, vt, gs, q_norm, k_norm, n_lat):
    nt = q.shape[0]
    nc = nt - n_lat
    tq = GQA_TQ
    gw = GQA_GROUP * HEAD_DIM
    n_vb, vb = vt.shape[1], vt.shape[3]
    assert nt % tq == 0 and n_lat % tq == 0 and n_lat % GQA_TK == 0 and n_lat % GQA_TK_ONLINE == 0
    assert n_vb * vb == nt and n_lat % vb == 0 and nc % vb == 0 and GQA_TK_ONLINE % vb == 0
    bound = (1.01 * HEAD_DIM * (HEAD_DIM ** -0.5) * LOG2E) * jnp.max(jnp.abs(q_norm)) * jnp.max(jnp.abs(k_norm))
    kern = functools.partial(_gqa_kernel, n_lat=n_lat, nc=nc)
    return pl.pallas_call(
        kern,
        grid=(GQA_KV_HEADS, nt // tq),
        in_specs=[
            pl.BlockSpec(memory_space=pltpu.SMEM),
            pl.BlockSpec((tq, gw), lambda h, b: (b, h)),
            pl.BlockSpec((nt, HEAD_DIM), lambda h, b: (0, h)),
            pl.BlockSpec((None, n_vb, HEAD_DIM, vb), lambda h, b: (h, 0, 0, 0)),
            pl.BlockSpec((tq, gw), lambda h, b: (b, h)),
        ],
        out_specs=pl.BlockSpec((tq, gw), lambda h, b: (b, h)),
        out_shape=jax.ShapeDtypeStruct((nt, GQA_HEADS * HEAD_DIM), BF16),
        compiler_params=_cparams(2),
        name="gqa_attention",
    )(bound.reshape(1).astype(F32), q, k, vt, gs)


def kernel(x, c, ctx, c_ctx, norm_g, w_mod, b_mod, e_w_in, e_rpb, e_dw_w, e_dw_b, e_ln_g, e_ln_b, e_w_out,
           o_w_in, o_q_norm, o_k_norm, o_w_out, final_norm_g):
    batch, n_lat, d = x.shape
    nc = ctx.shape[1]
    depth = norm_g.shape[0]
    assert batch == 1
    rows = n_lat // GRID_W

    nt = n_lat + nc
    stream = (x[0], ctx[0], 0)
    cvec = jnp.zeros((8, d), F32).at[0].set(c[0]).at[1].set(c_ctx)
    mods = _modulation(cvec, w_mod, b_mod)
    cos_t, sin_t = _rope_tables(n_lat, nc)
    fg = final_norm_g.reshape(1, d)

    for l in range(depth):
        i = l // 2
        g = norm_g[l].reshape(1, d)
        mod = mods[l]
        final = l == depth - 1
        if l % 2 == 0:
            qkv, ga, u, gb = _even_proj(stream, nt, g, mod, e_w_in, i, n_lat)
            mix_a = _na_attention(qkv, ga, _na_bias_table(e_rpb[i], rows, nc), n_lat)
            mix_b = _conformer_conv(u, gb, e_dw_w[i], e_dw_b[i], e_ln_g[i], e_ln_b[i], n_lat)
            xs = _out_proj(mix_a, mix_b, (0, 0), e_w_out, i, stream, nt, mod, fg, n_lat, final)
        else:
            v0 = (GQA_HEADS + GQA_KV_HEADS) * HEAD_DIM
            w_v = lax.optimization_barrier(o_w_in[i, :, v0:v0 + GQA_KV_HEADS * HEAD_DIM])
            w_vt = w_v.T.astype(BF16)
            q, k, vt, gs = _odd_proj(stream, nt, g, mod, o_w_in, w_vt, i, cos_t, sin_t,
                                     o_q_norm[i], o_k_norm[i], n_lat)
            mix = _gqa_attention(q, k, vt, gs, o_q_norm[i], o_k_norm[i], n_lat)
            xs = _out_proj(mix, mix, (0, 1), o_w_out, i, stream, nt, mod, fg, n_lat, final)
        stream = (xs, xs, n_lat)
    return xs[None]
```

```python
import functools

import numpy as np
import jax
import jax.numpy as jnp
from jax import lax
from jax.experimental import pallas as pl
from jax.experimental.pallas import tpu as pltpu

F32 = jnp.float32
BF16 = jnp.bfloat16

GRID_W = 64
HEAD_DIM = 128
NORM_EPS = 1e-6
NA_HEADS = 8
NA_WIN_ROWS = 8
NA_WIN_COLS = 16
CONV_KERNEL = 31
GQA_HEADS = 16
GQA_KV_HEADS = 4
GQA_GROUP = GQA_HEADS // GQA_KV_HEADS
ROPE_THETA = 10000.0
LOG2E = 1.4426950408889634

LANES = 128
SUBLANES = 8
V7X_VMEM_BYTES = 64 * 1024 * 1024
VMEM_LIMIT = V7X_VMEM_BYTES - 8 * 1024 * 1024

MOD_TN = 768
TM_PROJ = 256
TM_OUT = 256
NA_ROWS = 4
NA_SLAB = NA_WIN_ROWS + NA_ROWS
NA_TQ = NA_ROWS * GRID_W
NA_HP = 4
CONV_T = 256
CONV_HALO = 16
CONV_CH = 32
GQA_TQ = 256
GQA_TK = 2048
GQA_TK_ONLINE = 512
GQA_SAFE_BOUND = 60.0
NEG_BIG = -1e30


def _cparams(n_axes):
    return pltpu.CompilerParams(dimension_semantics=("arbitrary",) * n_axes, vmem_limit_bytes=VMEM_LIMIT)


def _sigmoid(x):
    return 1.0 / (1.0 + jnp.exp(-x))


def _silu(x):
    return x * _sigmoid(x)


def _dot(a, b):
    return jnp.dot(a, b, preferred_element_type=F32)


def _dot_nt(a, b):
    return lax.dot_general(a, b, (((1,), (1,)), ((), ())), preferred_element_type=F32)


def _stream_specs(lat_src, ctx_src, ctx_row0, tm, n_lat):
    d = lat_src.shape[1]
    nlt = n_lat // tm
    n_ctx_tiles = (ctx_src.shape[0] - ctx_row0) // tm
    assert n_lat % tm == 0 and ctx_row0 % tm == 0 and n_ctx_tiles >= 1
    ctx_mode = dict(pipeline_mode=pl.Buffered(1)) if n_ctx_tiles == 1 else {}
    return [pl.BlockSpec((tm, d), lambda i: (jnp.minimum(i, nlt - 1), 0)),
            pl.BlockSpec((tm, d), lambda i: (ctx_row0 // tm + jnp.clip(i - nlt, 0, n_ctx_tiles - 1), 0), **ctx_mode)]


def _stream_tile(x_ref, c_ref, tile_idx, tm, n_lat):
    return jnp.where(tile_idx * tm >= n_lat, c_ref[...], x_ref[...])


def _prenorm(x_ref, c_ref, g_ref, mod_ref, tile_idx, tm, n_lat):
    d = x_ref.shape[-1]
    x = _stream_tile(x_ref, c_ref, tile_idx, tm, n_lat)
    ms = jnp.mean(x * x, axis=-1, keepdims=True)
    row = (tile_idx * tm >= n_lat).astype(jnp.int32)
    mod = mod_ref[pl.ds(row, 1), :]
    a = g_ref[...] * (1.0 + mod[:, d:2 * d])
    return (x * lax.rsqrt(ms + NORM_EPS) * a + mod[:, 0:d]).astype(BF16)


W_STAGE_ROWS = 64
W_STAGE_SLOTS = 4


def _load_weights_bf16(w_hbm, layer, w_bf, stage, sem):
    slots, rows = stage.shape[0], stage.shape[1]
    n_chunks = w_bf.shape[0] // rows

    def copy(c):
        return pltpu.make_async_copy(w_hbm.at[layer, pl.ds(c * rows, rows), :],
                                     stage.at[c % slots], sem.at[c % slots])

    for c in range(min(slots - 1, n_chunks)):
        copy(c).start()
    for c in range(n_chunks):
        copy(c).wait()
        if c + slots - 1 < n_chunks:
            copy(c + slots - 1).start()
        w_bf[c * rows:(c + 1) * rows, :] = stage[c % slots].astype(BF16)


def _weight_scratch(w):
    d, n = w.shape[1:]
    assert d % W_STAGE_ROWS == 0
    return [pltpu.VMEM((d, n), BF16), pltpu.VMEM((W_STAGE_SLOTS, W_STAGE_ROWS, n), F32),
            pltpu.SemaphoreType.DMA((W_STAGE_SLOTS,))]


def _mod_kernel(c_ref, w_ref, b_ref, o_ref):
    s = _silu(c_ref[...]).astype(BF16)
    o_ref[...] = _dot(s, w_ref[...].astype(BF16)) + b_ref[...]


def _modulation(cvec, w_mod, b_mod):
    depth, d, d3 = w_mod.shape
    tn = MOD_TN
    assert d3 % tn == 0 and cvec.shape == (SUBLANES, d)
    return pl.pallas_call(
        _mod_kernel,
        grid=(depth, d3 // tn),
        in_specs=[
            pl.BlockSpec((SUBLANES, d), lambda l, j: (0, 0)),
            pl.BlockSpec((None, d, tn), lambda l, j: (l, 0, j)),
            pl.BlockSpec((None, 1, tn), lambda l, j: (l, 0, j)),
        ],
        out_specs=pl.BlockSpec((None, SUBLANES, tn), lambda l, j: (l, 0, j)),
        out_shape=jax.ShapeDtypeStruct((depth, SUBLANES, d3), F32),
        compiler_params=_cparams(2),
        name="modulation",
    )(cvec, w_mod, b_mod.reshape(depth, 1, d3))


def _even_proj_kernel(x_ref, c_ref, g_ref, mod_ref, w_hbm, qkv_ref, ga_ref, u_ref, gb_ref, w_ref, stage, sem,
                      *, n_lat, tm, layer):
    @pl.when(pl.program_id(0) == 0)
    def _():
        _load_weights_bf16(w_hbm, layer, w_ref, stage, sem)

    h = _prenorm(x_ref, c_ref, g_ref, mod_ref, pl.program_id(0), tm, n_lat)
    cw = ga_ref.shape[-1]

    def proj(seg):
        return _dot(h, w_ref[:, seg * cw:(seg + 1) * cw])

    for seg in range(3):
        qkv_ref[:, seg * cw:(seg + 1) * cw] = proj(seg).astype(BF16)
    ga_ref[...] = _silu(proj(3)).astype(BF16)
    u_ref[...] = proj(4) * _sigmoid(proj(5))
    gb_ref[...] = _silu(proj(6)).astype(BF16)


def _even_proj(stream, nt, g, mod, w, layer, n_lat):
    d = stream[0].shape[1]
    tm = TM_PROJ
    cw = w.shape[2] // 7
    assert nt % tm == 0 and n_lat % tm == 0
    kern = functools.partial(_even_proj_kernel, n_lat=n_lat, tm=tm, layer=layer)
    rows = lambda width: pl.BlockSpec((tm, width), lambda i: (i, 0))
    return pl.pallas_call(
        kern,
        grid=(nt // tm,),
        in_specs=_stream_specs(*stream, tm, n_lat) + [
            pl.BlockSpec((1, d), lambda i: (0, 0)),
            pl.BlockSpec((SUBLANES, 3 * d), lambda i: (0, 0)),
            pl.BlockSpec(memory_space=pl.ANY),
        ],
        out_specs=[rows(3 * cw), rows(cw), rows(cw), rows(cw)],
        out_shape=[
            jax.ShapeDtypeStruct((nt, 3 * cw), BF16),
            jax.ShapeDtypeStruct((nt, cw), BF16),
            jax.ShapeDtypeStruct((nt, cw), F32),
            jax.ShapeDtypeStruct((nt, cw), BF16),
        ],
        scratch_shapes=_weight_scratch(w),
        compiler_params=_cparams(1),
        name="even_proj",
    )(stream[0], stream[1], g, mod, w)


def _na_bias_kernel(rp_ref, o_ref, *, rows, nc):
    pattern = pl.program_id(0)
    lane = lax.broadcasted_iota(jnp.int32, (GRID_W, 2 * GRID_W), 1)
    qc = lax.broadcasted_iota(jnp.int32, (GRID_W, 2 * GRID_W), 0)
    kc = lane % GRID_W
    cs = jnp.clip(qc - NA_WIN_COLS // 2, 0, GRID_W - NA_WIN_COLS)
    col_ok = jnp.logical_and(kc >= cs, kc < cs + NA_WIN_COLS)
    left_half = lane < GRID_W
    masked = jnp.full((GRID_W, 2 * GRID_W), NEG_BIG, F32)

    def toeplitz(dr, right):
        x = jnp.broadcast_to(rp_ref[dr:dr + 1, :], (GRID_W, 2 * GRID_W))
        return pltpu.roll(x, 1 if right else GRID_W + 1, 1, stride=1, stride_axis=0)

    for pat_id, r0 in enumerate((0, 2 * NA_ROWS, rows - NA_ROWS)):
        @pl.when(pattern == pat_id)
        def _(r0=r0):
            ks = int(np.clip(r0 - NA_WIN_ROWS // 2, 0, rows - NA_SLAB))
            for r_local in range(NA_ROWS):
                r = r0 + r_local
                rs = int(np.clip(r - NA_WIN_ROWS // 2, 0, rows - NA_WIN_ROWS))
                for jp in range(NA_SLAB // 2):
                    halves = [toeplitz(kr - r + NA_WIN_ROWS - 1, side == 1) if rs <= kr < rs + NA_WIN_ROWS else None
                              for side, kr in enumerate((ks + 2 * jp, ks + 2 * jp + 1))]
                    if halves[0] is None and halves[1] is None:
                        tile = masked
                    else:
                        left = masked if halves[0] is None else halves[0]
                        right = masked if halves[1] is None else halves[1]
                        tile = jnp.where(col_ok, jnp.where(left_half, left, right) * LOG2E, NEG_BIG)
                    o_ref[r_local * GRID_W:(r_local + 1) * GRID_W, jp * 2 * GRID_W:(jp + 1) * 2 * GRID_W] = tile
    o_ref[:, NA_SLAB * GRID_W:] = jnp.zeros((NA_TQ, nc), F32)


def _na_bias_table(rpb, rows, nc):
    heads = rpb.shape[0]
    assert NA_SLAB % 2 == 0 and 2 * GRID_W == LANES and rpb.shape[1] == 2 * NA_WIN_ROWS - 1
    pad = GRID_W - NA_WIN_COLS
    rp = jnp.pad(rpb.astype(F32), ((0, 0), (0, 1), (pad, LANES - pad - rpb.shape[2])))
    n_keys = NA_SLAB * GRID_W + nc
    kern = functools.partial(_na_bias_kernel, rows=rows, nc=nc)
    return pl.pallas_call(
        kern,
        grid=(3, heads),
        in_specs=[pl.BlockSpec((None, 2 * NA_WIN_ROWS, LANES), lambda p, h: (h, 0, 0))],
        out_specs=pl.BlockSpec((None, None, NA_TQ, n_keys), lambda p, h: (p, h, 0, 0)),
        out_shape=jax.ShapeDtypeStruct((3, heads, NA_TQ, n_keys), F32),
        compiler_params=_cparams(2),
        name="na_bias",
    )(rp)


def _na_kernel(q_ref, k_ref, v_ref, bias_ref, ga_ref, o_ref, *, n_lat, nc, rows):
    b = pl.program_id(1)
    nb_lat = rows // NA_ROWS
    scale = (HEAD_DIM ** -0.5) * LOG2E

    def attend(h, keys, values, bias):
        sl = slice(h * HEAD_DIM, (h + 1) * HEAD_DIM)
        s = _dot_nt(q_ref[:, sl], keys) * scale
        if bias is not None:
            s = s + bias
        p = jnp.exp2(s - jnp.max(s, axis=-1, keepdims=True)).astype(BF16)
        o = _dot(p, jnp.concatenate([values, jnp.ones_like(values)], axis=1))
        gate = ga_ref[:, sl].astype(F32)
        o_ref[:, sl] = (o[:, :HEAD_DIM] / o[:, HEAD_DIM:] * gate).astype(BF16)

    @pl.when(b < nb_lat)
    def _():
        ks = jnp.clip(b * NA_ROWS - NA_WIN_ROWS // 2, 0, rows - NA_SLAB)
        start = pl.multiple_of(ks * GRID_W, GRID_W)
        for h in range(NA_HP):
            sl = slice(h * HEAD_DIM, (h + 1) * HEAD_DIM)
            keys = jnp.concatenate([k_ref[pl.ds(start, NA_SLAB * GRID_W), sl], k_ref[n_lat:n_lat + nc, sl]], axis=0)
            values = jnp.concatenate([v_ref[pl.ds(start, NA_SLAB * GRID_W), sl], v_ref[n_lat:n_lat + nc, sl]], axis=0)
            attend(h, keys, values, bias_ref[h])

    @pl.when(b >= nb_lat)
    def _():
        for h in range(NA_HP):
            sl = slice(h * HEAD_DIM, (h + 1) * HEAD_DIM)
            attend(h, k_ref[n_lat:n_lat + nc, sl], v_ref[n_lat:n_lat + nc, sl], None)


def _na_attention(qkv, ga, bias_tab, n_lat):
    nt = qkv.shape[0]
    nc = nt - n_lat
    rows = n_lat // GRID_W
    assert nc == NA_TQ and rows % NA_ROWS == 0 and rows >= NA_SLAB
    nb = nt // NA_TQ
    nb_lat = rows // NA_ROWS
    n_keys = NA_SLAB * GRID_W + nc
    hw = NA_HP * HEAD_DIM
    groups = NA_HEADS // NA_HP

    def pat(b):
        return jnp.where(b == 0, 0, jnp.where(b >= nb_lat - 1, 2, 1))

    kern = functools.partial(_na_kernel, n_lat=n_lat, nc=nc, rows=rows)
    return pl.pallas_call(
        kern,
        grid=(groups, nb),
        in_specs=[
            pl.BlockSpec((NA_TQ, hw), lambda h, b: (b, h)),
            pl.BlockSpec((nt, hw), lambda h, b: (0, groups + h)),
            pl.BlockSpec((nt, hw), lambda h, b: (0, 2 * groups + h)),
            pl.BlockSpec((None, NA_HP, NA_TQ, n_keys), lambda h, b: (pat(b), h, 0, 0)),
            pl.BlockSpec((NA_TQ, hw), lambda h, b: (b, h)),
        ],
        out_specs=pl.BlockSpec((NA_TQ, hw), lambda h, b: (b, h)),
        out_shape=jax.ShapeDtypeStruct((nt, NA_HEADS * HEAD_DIM), BF16),
        compiler_params=_cparams(2),
        name="na_attention",
    )(qkv, qkv, qkv, bias_tab, ga)


def _conv_kernel(um_ref, up_ref, un_ref, gb_ref, w_ref, b_ref, lg_ref, lb_ref, o_ref, ubuf, shifted, *, n_lat, nt):
    t = pl.program_id(0)
    start = t * CONV_T
    prev_ok = jnp.logical_and(start != 0, start != n_lat)
    next_ok = jnp.logical_and(start + CONV_T != n_lat, start + CONV_T != nt)
    ubuf[0:CONV_HALO, :] = jnp.where(prev_ok, up_ref[...], 0.0)
    ubuf[CONV_HALO:CONV_HALO + CONV_T, :] = um_ref[...]
    ubuf[CONV_HALO + CONV_T:, :] = jnp.where(next_ok, un_ref[...], 0.0)
    span = CONV_T + 2 * CONV_HALO - SUBLANES
    for b in range(1, SUBLANES):
        shifted[b - 1, :, :] = ubuf[b:b + span, :]
    off = CONV_HALO - CONV_KERNEL // 2
    for r0 in range(0, CONV_T, CONV_CH):
        cw = um_ref.shape[-1]
        acc = jnp.zeros((CONV_CH // SUBLANES, SUBLANES, cw), F32) + b_ref[...]
        for j in range(CONV_KERNEL):
            a, b = divmod(off + j, SUBLANES)
            rows = slice(r0 + SUBLANES * a, r0 + SUBLANES * a + CONV_CH)
            tap = ubuf[rows, :] if b == 0 else shifted[b - 1, rows, :]
            acc = acc + w_ref[j][None] * tap.reshape(CONV_CH // SUBLANES, SUBLANES, cw)
        acc = acc.reshape(CONV_CH, cw)
        mu = jnp.mean(acc, axis=-1, keepdims=True)
        xc = acc - mu
        var = jnp.mean(xc * xc, axis=-1, keepdims=True)
        y = xc * lax.rsqrt(var + NORM_EPS) * lg_ref[...] + lb_ref[...]
        y = _silu(y) * gb_ref[r0:r0 + CONV_CH, :].astype(F32)
        o_ref[r0:r0 + CONV_CH, :] = y.astype(BF16)


def _conformer_conv(u, gb, dw_w, dw_b, ln_g, ln_b, n_lat):
    nt, cw = u.shape
    assert nt % CONV_T == 0 and n_lat % CONV_T == 0
    hb = CONV_T // CONV_HALO
    n_halo_blocks = nt // CONV_HALO
    assert dw_w.shape == (CONV_KERNEL, cw)
    w_rep = jnp.broadcast_to(dw_w[:, None, :], (CONV_KERNEL, SUBLANES, cw))
    kern = functools.partial(_conv_kernel, n_lat=n_lat, nt=nt)
    vec = lambda: pl.BlockSpec((1, cw), lambda t: (0, 0))
    return pl.pallas_call(
        kern,
        grid=(nt // CONV_T,),
        in_specs=[
            pl.BlockSpec((CONV_T, cw), lambda t: (t, 0)),
            pl.BlockSpec((CONV_HALO, cw), lambda t: (jnp.maximum(t * hb - 1, 0), 0)),
            pl.BlockSpec((CONV_HALO, cw), lambda t: (jnp.minimum((t + 1) * hb, n_halo_blocks - 1), 0)),
            pl.BlockSpec((CONV_T, cw), lambda t: (t, 0)),
            pl.BlockSpec((CONV_KERNEL, SUBLANES, cw), lambda t: (0, 0, 0)),
            vec(), vec(), vec(),
        ],
        out_specs=pl.BlockSpec((CONV_T, cw), lambda t: (t, 0)),
        out_shape=jax.ShapeDtypeStruct((nt, cw), BF16),
        scratch_shapes=[pltpu.VMEM((CONV_T + 2 * CONV_HALO, cw), F32),
                        pltpu.VMEM((SUBLANES - 1, CONV_T + 2 * CONV_HALO - SUBLANES, cw), F32)],
        compiler_params=_cparams(1),
        name="conformer_conv",
    )(u, u, u, gb, w_rep, dw_b.reshape(1, cw), ln_g.reshape(1, cw), ln_b.reshape(1, cw))


def _out_proj_kernel(a_ref, b_ref, w32_ref, x_ref, c_ref, mod_ref, fg_ref, o_ref, w_ref, *, n_lat, tm, final):
    i = pl.program_id(0)

    @pl.when(i == 0)
    def _():
        w_ref[...] = w32_ref[...].astype(BF16)

    half = a_ref.shape[-1]
    y = _dot(a_ref[...], w_ref[0:half, :]) + _dot(b_ref[...], w_ref[half:, :])
    d = y.shape[-1]
    row = (i * tm >= n_lat).astype(jnp.int32)
    gt = mod_ref[pl.ds(row, 1), 2 * d:3 * d]
    xn = _stream_tile(x_ref, c_ref, i, tm, n_lat) + gt * y
    if final:
        ms = jnp.mean(xn * xn, axis=-1, keepdims=True)
        xn = xn * lax.rsqrt(ms + NORM_EPS) * fg_ref[...]
    o_ref[...] = xn


def _out_proj(mix_a, mix_b, cols, w, layer, stream, nt, mod, fg, n_lat, final):
    d = stream[0].shape[1]
    tm = TM_OUT
    half = w.shape[1] // 2
    out_rows = n_lat if final else nt
    assert out_rows % tm == 0
    kern = functools.partial(_out_proj_kernel, n_lat=n_lat, tm=tm, final=final)
    return pl.pallas_call(
        kern,
        grid=(out_rows // tm,),
        in_specs=[
            pl.BlockSpec((tm, half), lambda i: (i, cols[0])),
            pl.BlockSpec((tm, half), lambda i: (i, cols[1])),
            pl.BlockSpec((None, 2 * half, d), lambda i: (layer, 0, 0), pipeline_mode=pl.Buffered(1)),
        ] + _stream_specs(*stream, tm, n_lat) + [
            pl.BlockSpec((SUBLANES, 3 * d), lambda i: (0, 0)),
            pl.BlockSpec((1, d), lambda i: (0, 0)),
        ],
        out_specs=pl.BlockSpec((tm, d), lambda i: (i, 0)),
        out_shape=jax.ShapeDtypeStruct((out_rows, d), F32),
        scratch_shapes=[pltpu.VMEM((2 * half, d), BF16)],
        compiler_params=_cparams(1),
        name="out_proj_final" if final else "out_proj",
    )(mix_a, mix_b, w, stream[0], stream[1], mod, fg)


def _rope_tables(n_lat, nc):
    rows = n_lat // GRID_W
    n_freq = HEAD_DIM // 4
    inv = ROPE_THETA ** (-jnp.arange(n_freq, dtype=F32) / n_freq)
    row_ang = jnp.arange(rows, dtype=jnp.int32).astype(F32)[:, None] * inv
    col_ang = jnp.arange(GRID_W, dtype=jnp.int32).astype(F32)[:, None] * inv
    per_row = lambda a: jnp.repeat(a, GRID_W, axis=0)
    per_col = lambda a: jnp.tile(a, (rows, 1))
    cr, sr = per_row(jnp.cos(row_ang)), per_row(jnp.sin(row_ang))
    cc, sc = per_col(jnp.cos(col_ang)), per_col(jnp.sin(col_ang))
    cos_t = jnp.concatenate([cr, cr, cc, cc], axis=-1)
    sin_t = jnp.concatenate([-sr, sr, -sc, sc], axis=-1)
    cos_t = jnp.concatenate([cos_t, jnp.ones((nc, HEAD_DIM), F32)], axis=0)
    sin_t = jnp.concatenate([sin_t, jnp.zeros((nc, HEAD_DIM), F32)], axis=0)
    return cos_t, sin_t


def _norm_rope(x, gain, cos_t, sin_t, first_half):
    ms = jnp.mean(x * x, axis=-1, keepdims=True)
    xn = x * lax.rsqrt(ms + NORM_EPS) * gain
    partner = jnp.where(first_half, pltpu.roll(xn, 3 * HEAD_DIM // 4, 1), pltpu.roll(xn, HEAD_DIM // 4, 1))
    return xn * cos_t + partner * sin_t


def _odd_proj_kernel(x_ref, c_ref, g_ref, mod_ref, w_hbm, wvt_ref, cos_ref, sin_ref, qn_ref, kn_ref,
                     q_ref, k_ref, vt_ref, gs_ref, w_ref, stage, sem, *, n_lat, tm, layer):
    @pl.when(pl.program_id(0) == 0)
    def _():
        _load_weights_bf16(w_hbm, layer, w_ref, stage, sem)

    h = _prenorm(x_ref, c_ref, g_ref, mod_ref, pl.program_id(0), tm, n_lat)
    tn = k_ref.shape[-1]
    cos_t = cos_ref[...]
    sin_t = sin_ref[...]
    lane = lax.broadcasted_iota(jnp.int32, (1, HEAD_DIM), 1)
    first_half = (lane % (HEAD_DIM // 2)) < (HEAD_DIM // 4)

    def proj(seg):
        return _dot(h, w_ref[:, seg * tn:(seg + 1) * tn])

    def rotated(seg, gain, out_ref, col0):
        acc = proj(seg)
        for hh in range(tn // HEAD_DIM):
            y = _norm_rope(acc[:, hh * HEAD_DIM:(hh + 1) * HEAD_DIM], gain, cos_t, sin_t, first_half)
            out_ref[:, col0 + hh * HEAD_DIM:col0 + (hh + 1) * HEAD_DIM] = y.astype(BF16)

    n_q = q_ref.shape[-1] // tn
    q_gain = qn_ref[...] * ((HEAD_DIM ** -0.5) * LOG2E)
    for seg in range(n_q):
        rotated(seg, q_gain, q_ref, seg * tn)
    rotated(n_q, kn_ref[...], k_ref, 0)
    v_t = _dot_nt(wvt_ref[...], h)
    vt_ref[...] = v_t.reshape(vt_ref.shape).astype(BF16)
    for seg in range(n_q):
        gs_ref[:, seg * tn:(seg + 1) * tn] = _silu(proj(n_q + 2 + seg)).astype(BF16)


def _odd_proj(stream, nt, g, mod, w, w_vt, layer, cos_t, sin_t, q_norm, k_norm, n_lat):
    d = stream[0].shape[1]
    tm = TM_PROJ
    qw = GQA_HEADS * HEAD_DIM
    kw = GQA_KV_HEADS * HEAD_DIM
    assert w.shape[2] == 2 * qw + 2 * kw and nt % tm == 0 and n_lat % tm == 0
    kern = functools.partial(_odd_proj_kernel, n_lat=n_lat, tm=tm, layer=layer)
    rows = lambda width: pl.BlockSpec((tm, width), lambda i: (i, 0))
    vec = lambda width: pl.BlockSpec((1, width), lambda i: (0, 0))
    return pl.pallas_call(
        kern,
        grid=(nt // tm,),
        in_specs=_stream_specs(*stream, tm, n_lat) + [
            vec(d),
            pl.BlockSpec((SUBLANES, 3 * d), lambda i: (0, 0)),
            pl.BlockSpec(memory_space=pl.ANY),
            pl.BlockSpec((kw, d), lambda i: (0, 0)),
            rows(HEAD_DIM), rows(HEAD_DIM), vec(HEAD_DIM), vec(HEAD_DIM),
        ],
        out_specs=[rows(qw), rows(kw),
                   pl.BlockSpec((GQA_KV_HEADS, None, HEAD_DIM, tm), lambda i: (0, i, 0, 0)),
                   rows(qw)],
        out_shape=[
            jax.ShapeDtypeStruct((nt, qw), BF16),
            jax.ShapeDtypeStruct((nt, kw), BF16),
            jax.ShapeDtypeStruct((GQA_KV_HEADS, nt // tm, HEAD_DIM, tm), BF16),
            jax.ShapeDtypeStruct((nt, qw), BF16),
        ],
        scratch_shapes=_weight_scratch(w),
        compiler_params=_cparams(1),
        name="odd_proj",
    )(stream[0], stream[1], g, mod, w, w_vt, cos_t, sin_t,
      q_norm.reshape(1, HEAD_DIM), k_norm.reshape(1, HEAD_DIM))


def _gqa_kernel(bound_ref, q_ref, k_ref, vt_ref, g_ref, o_ref, *, n_lat, nc):
    b = pl.program_id(1)
    tq = q_ref.shape[0]
    vb = vt_ref.shape[-1]
    bound = bound_ref[0]
    is_lat = b * tq < n_lat
    q = jnp.concatenate([q_ref[:, h * HEAD_DIM:(h + 1) * HEAD_DIM] for h in range(GQA_GROUP)], axis=0)

    def vt_chunk(first_block, n_blocks):
        return jnp.concatenate([vt_ref[first_block + j] for j in range(n_blocks)], axis=1)

    def write_out(o_t):
        o = o_t.T
        for h in range(GQA_GROUP):
            sl = slice(h * HEAD_DIM, (h + 1) * HEAD_DIM)
            o_ref[:, sl] = (o[h * tq:(h + 1) * tq, :] * g_ref[:, sl].astype(F32)).astype(BF16)

    def fixed_shift_chunk(row0, n_rows):
        p = jnp.exp2(_dot_nt(k_ref[row0:row0 + n_rows, :], q) - bound)
        partial = jnp.sum(p.reshape(n_rows // SUBLANES, SUBLANES, p.shape[-1]), axis=0)
        return _dot(vt_chunk(row0 // vb, n_rows // vb), p.astype(BF16)), partial

    def finish(acc, partial):
        write_out(acc / jnp.sum(partial, axis=0, keepdims=True))

    small = bound <= GQA_SAFE_BOUND

    @pl.when(jnp.logical_and(small, is_lat))
    def _():
        acc, partial = fixed_shift_chunk(n_lat, nc)
        for c in range(n_lat // GQA_TK):
            acc_c, partial_c = fixed_shift_chunk(c * GQA_TK, GQA_TK)
            acc, partial = acc + acc_c, partial + partial_c
        finish(acc, partial)

    @pl.when(jnp.logical_and(small, jnp.logical_not(is_lat)))
    def _():
        finish(*fixed_shift_chunk(n_lat, nc))

    @pl.when(bound > GQA_SAFE_BOUND)
    def _():
        cols = GQA_GROUP * tq
        per = GQA_TK_ONLINE // vb

        def step(carry, k_chunk, vt):
            m, l, acc = carry
            s = _dot_nt(k_chunk, q)
            m_new = jnp.maximum(m, jnp.max(s, axis=0, keepdims=True))
            alpha = jnp.exp2(m - m_new)
            p = jnp.exp2(s - m_new)
            l = alpha * l + jnp.sum(p, axis=0, keepdims=True)
            acc = alpha * acc + _dot(vt, p.astype(BF16))
            return m_new, l, acc

        init = (jnp.full((1, cols), NEG_BIG, F32), jnp.zeros((1, cols), F32), jnp.zeros((HEAD_DIM, cols), F32))
        carry = step(init, k_ref[n_lat:n_lat + nc, :], vt_chunk(n_lat // vb, nc // vb))

        def body(c, carry):
            start = pl.multiple_of(c * GQA_TK_ONLINE, GQA_TK_ONLINE)
            vt = jnp.concatenate([vt_ref[c * per + j] for j in range(per)], axis=1)
            return step(carry, k_ref[pl.ds(start, GQA_TK_ONLINE), :], vt)

        m, l, acc = lax.fori_loop(0, jnp.where(is_lat, n_lat // GQA_TK_ONLINE, 0), body, carry)
        write_out(acc / l)


def _gqa_attention(q, k, vt, gs, q_norm, k_norm, n_lat):
    nt = q.shape[0]
    nc = nt - n_lat
    tq = GQA_TQ
    gw = GQA_GROUP * HEAD_DIM
    n_vb, vb = vt.shape[1], vt.shape[3]
    assert nt % tq == 0 and n_lat % tq == 0 and n_lat % GQA_TK == 0 and n_lat % GQA_TK_ONLINE == 0
    assert n_vb * vb == nt and n_lat % vb == 0 and nc % vb == 0 and GQA_TK_ONLINE % vb == 0
    bound = (1.01 * HEAD_DIM * (HEAD_DIM ** -0.5) * LOG2E) * jnp.max(jnp.abs(q_norm)) * jnp.max(jnp.abs(k_norm))
    kern = functools.partial(_gqa_kernel, n_lat=n_lat, nc=nc)
    return pl.pallas_call(
        kern,
        grid=(GQA_KV_HEADS, nt // tq),
        in_specs=[
            pl.BlockSpec(memory_space=pltpu.SMEM),
            pl.BlockSpec((tq, gw), lambda h, b: (b, h)),
            pl.BlockSpec((nt, HEAD_DIM), lambda h, b: (0, h)),
            pl.BlockSpec((None, n_vb, HEAD_DIM, vb), lambda h, b: (h, 0, 0, 0)),
            pl.BlockSpec((tq, gw), lambda h, b: (b, h)),
        ],
        out_specs=pl.BlockSpec((tq, gw), lambda h, b: (b, h)),
        out_shape=jax.ShapeDtypeStruct((nt, GQA_HEADS * HEAD_DIM), BF16),
        compiler_params=_cparams(2),
        name="gqa_attention",
    )(bound.reshape(1).astype(F32), q, k, vt, gs)


def kernel(x, c, ctx, c_ctx, norm_g, w_mod, b_mod, e_w_in, e_rpb, e_dw_w, e_dw_b, e_ln_g, e_ln_b, e_w_out,
           o_w_in, o_q_norm, o_k_norm, o_w_out, final_norm_g):
    batch, n_lat, d = x.shape
    nc = ctx.shape[1]
    depth = norm_g.shape[0]
    assert batch == 1
    rows = n_lat // GRID_W

    nt = n_lat + nc
    stream = (x[0], ctx[0], 0)
    cvec = jnp.zeros((SUBLANES, d), F32).at[0].set(c[0]).at[1].set(c_ctx)
    mods = _modulation(cvec, w_mod, b_mod)
    cos_t, sin_t = _rope_tables(n_lat, nc)
    fg = final_norm_g.reshape(1, d)

    for l in range(depth):
        i = l // 2
        g = norm_g[l].reshape(1, d)
        mod = mods[l]
        final = l == depth - 1
        if l % 2 == 0:
            qkv, ga, u, gb = _even_proj(stream, nt, g, mod, e_w_in, i, n_lat)
            mix_a = _na_attention(qkv, ga, _na_bias_table(e_rpb[i], rows, nc), n_lat)
            mix_b = _conformer_conv(u, gb, e_dw_w[i], e_dw_b[i], e_ln_g[i], e_ln_b[i], n_lat)
            xs = _out_proj(mix_a, mix_b, (0, 0), e_w_out, i, stream, nt, mod, fg, n_lat, final)
        else:
            v0 = (GQA_HEADS + GQA_KV_HEADS) * HEAD_DIM
            w_v = lax.optimization_barrier(o_w_in[i, :, v0:v0 + GQA_KV_HEADS * HEAD_DIM])
            w_vt = w_v.T.astype(BF16)
            q, k, vt, gs = _odd_proj(stream, nt, g, mod, o_w_in, w_vt, i, cos_t, sin_t,
                                     o_q_norm[i], o_k_norm[i], n_lat)
            mix = _gqa_attention(q, k, vt, gs, o_q_norm[i], o_k_norm[i], n_lat)
            xs = _out_proj(mix, mix, (0, 1), o_w_out, i, stream, nt, mod, fg, n_lat, final)
        stream = (xs, xs, n_lat)
    return xs[None]
```

```python
import functools

import numpy as np
import jax
import jax.numpy as jnp
from jax import lax
from jax.experimental import pallas as pl
from jax.experimental.pallas import tpu as pltpu

F32 = jnp.float32
BF16 = jnp.bfloat16

GRID_W = 64
HEAD_DIM = 128
NORM_EPS = 1e-6
NA_HEADS = 8
NA_WIN_ROWS = 8
NA_WIN_COLS = 16
CONV_KERNEL = 31
GQA_HEADS = 16
GQA_KV_HEADS = 4
GQA_GROUP = GQA_HEADS // GQA_KV_HEADS
ROPE_THETA = 10000.0
LOG2E = 1.4426950408889634

LANES = 128
SUBLANES = 8
V7X_VMEM_BYTES = 64 * 1024 * 1024
VMEM_LIMIT = V7X_VMEM_BYTES - 8 * 1024 * 1024

MOD_TN = 768
TM_PROJ = 256
TM_OUT = 256
NA_ROWS = 4
NA_SLAB = NA_WIN_ROWS + NA_ROWS
NA_TQ = NA_ROWS * GRID_W
NA_HP = 4
CONV_T = 256
CONV_HALO = 16
CONV_CH = 32
GQA_TQ = 256
GQA_TK = 2048
GQA_TK_ONLINE = 512
GQA_SAFE_BOUND = 60.0
NEG_BIG = -1e30


def _cparams(n_axes):
    return pltpu.CompilerParams(dimension_semantics=("arbitrary",) * n_axes, vmem_limit_bytes=VMEM_LIMIT)


def _sigmoid(x):
    return 1.0 / (1.0 + jnp.exp(-x))


def _silu(x):
    return x * _sigmoid(x)


def _dot(a, b):
    return jnp.dot(a, b, preferred_element_type=F32)


def _dot_nt(a, b):
    return lax.dot_general(a, b, (((1,), (1,)), ((), ())), preferred_element_type=F32)


def _stream_specs(lat_src, ctx_src, ctx_row0, tm, n_lat):
    d = lat_src.shape[1]
    nlt = n_lat // tm
    n_ctx_tiles = (ctx_src.shape[0] - ctx_row0) // tm
    assert n_lat % tm == 0 and ctx_row0 % tm == 0 and n_ctx_tiles >= 1
    ctx_mode = dict(pipeline_mode=pl.Buffered(1)) if n_ctx_tiles == 1 else {}
    return [pl.BlockSpec((tm, d), lambda i: (jnp.minimum(i, nlt - 1), 0)),
            pl.BlockSpec((tm, d), lambda i: (ctx_row0 // tm + jnp.clip(i - nlt, 0, n_ctx_tiles - 1), 0), **ctx_mode)]


def _stream_tile(x_ref, c_ref, tile_idx, tm, n_lat):
    return jnp.where(tile_idx * tm >= n_lat, c_ref[...], x_ref[...])


def _prenorm(x_ref, c_ref, g_ref, mod_ref, tile_idx, tm, n_lat):
    d = x_ref.shape[-1]
    x = _stream_tile(x_ref, c_ref, tile_idx, tm, n_lat)
    ms = jnp.mean(x * x, axis=-1, keepdims=True)
    row = (tile_idx * tm >= n_lat).astype(jnp.int32)
    mod = mod_ref[pl.ds(row, 1), :]
    a = g_ref[...] * (1.0 + mod[:, d:2 * d])
    return (x * lax.rsqrt(ms + NORM_EPS) * a + mod[:, 0:d]).astype(BF16)


W_STAGE_ROWS = 64
W_STAGE_SLOTS = 4


def _load_weights_bf16(w_hbm, layer, w_bf, stage, sem):
    slots, rows = stage.shape[0], stage.shape[1]
    n_chunks = w_bf.shape[0] // rows

    def copy(c):
        return pltpu.make_async_copy(w_hbm.at[layer, pl.ds(c * rows, rows), :],
                                     stage.at[c % slots], sem.at[c % slots])

    for c in range(min(slots - 1, n_chunks)):
        copy(c).start()
    for c in range(n_chunks):
        copy(c).wait()
        if c + slots - 1 < n_chunks:
            copy(c + slots - 1).start()
        w_bf[c * rows:(c + 1) * rows, :] = stage[c % slots].astype(BF16)


def _weight_scratch(w):
    d, n = w.shape[1:]
    assert d % W_STAGE_ROWS == 0
    return [pltpu.VMEM((d, n), BF16), pltpu.VMEM((W_STAGE_SLOTS, W_STAGE_ROWS, n), F32),
            pltpu.SemaphoreType.DMA((W_STAGE_SLOTS,))]


def _mod_kernel(c_ref, w_ref, b_ref, o_ref):
    s = _silu(c_ref[...]).astype(BF16)
    o_ref[...] = _dot(s, w_ref[...].astype(BF16)) + b_ref[...]


def _modulation(cvec, w_mod, b_mod):
    depth, d, d3 = w_mod.shape
    tn = MOD_TN
    assert d3 % tn == 0 and cvec.shape == (SUBLANES, d)
    return pl.pallas_call(
        _mod_kernel,
        grid=(depth, d3 // tn),
        in_specs=[
            pl.BlockSpec((SUBLANES, d), lambda l, j: (0, 0)),
            pl.BlockSpec((None, d, tn), lambda l, j: (l, 0, j)),
            pl.BlockSpec((None, 1, tn), lambda l, j: (l, 0, j)),
        ],
        out_specs=pl.BlockSpec((None, SUBLANES, tn), lambda l, j: (l, 0, j)),
        out_shape=jax.ShapeDtypeStruct((depth, SUBLANES, d3), F32),
        compiler_params=_cparams(2),
        name="modulation",
    )(cvec, w_mod, b_mod.reshape(depth, 1, d3))


def _even_proj_kernel(x_ref, c_ref, g_ref, mod_ref, w_hbm, qkv_ref, ga_ref, u_ref, gb_ref, w_ref, stage, sem,
                      *, n_lat, tm, layer):
    @pl.when(pl.program_id(0) == 0)
    def _():
        _load_weights_bf16(w_hbm, layer, w_ref, stage, sem)

    h = _prenorm(x_ref, c_ref, g_ref, mod_ref, pl.program_id(0), tm, n_lat)
    cw = ga_ref.shape[-1]

    def proj(seg):
        return _dot(h, w_ref[:, seg * cw:(seg + 1) * cw])

    for seg in range(3):
        qkv_ref[:, seg * cw:(seg + 1) * cw] = proj(seg).astype(BF16)
    ga_ref[...] = _silu(proj(3)).astype(BF16)
    u_ref[...] = proj(4) * _sigmoid(proj(5))
    gb_ref[...] = _silu(proj(6)).astype(BF16)


def _even_proj(stream, nt, g, mod, w, layer, n_lat):
    d = stream[0].shape[1]
    tm = TM_PROJ
    cw = w.shape[2] // 7
    assert nt % tm == 0 and n_lat % tm == 0
    kern = functools.partial(_even_proj_kernel, n_lat=n_lat, tm=tm, layer=layer)
    rows = lambda width: pl.BlockSpec((tm, width), lambda i: (i, 0))
    return pl.pallas_call(
        kern,
        grid=(nt // tm,),
        in_specs=_stream_specs(*stream, tm, n_lat) + [
            pl.BlockSpec((1, d), lambda i: (0, 0)),
            pl.BlockSpec((SUBLANES, 3 * d), lambda i: (0, 0)),
            pl.BlockSpec(memory_space=pl.ANY),
        ],
        out_specs=[rows(3 * cw), rows(cw), rows(cw), rows(cw)],
        out_shape=[
            jax.ShapeDtypeStruct((nt, 3 * cw), BF16),
            jax.ShapeDtypeStruct((nt, cw), BF16),
            jax.ShapeDtypeStruct((nt, cw), F32),
            jax.ShapeDtypeStruct((nt, cw), BF16),
        ],
        scratch_shapes=_weight_scratch(w),
        compiler_params=_cparams(1),
        name="even_proj",
    )(stream[0], stream[1], g, mod, w)


def _na_bias_kernel(rp_ref, o_ref, *, rows, nc):
    pattern = pl.program_id(0)
    lane = lax.broadcasted_iota(jnp.int32, (GRID_W, 2 * GRID_W), 1)
    qc = lax.broadcasted_iota(jnp.int32, (GRID_W, 2 * GRID_W), 0)
    kc = lane % GRID_W
    cs = jnp.clip(qc - NA_WIN_COLS // 2, 0, GRID_W - NA_WIN_COLS)
    col_ok = jnp.logical_and(kc >= cs, kc < cs + NA_WIN_COLS)
    left_half = lane < GRID_W
    masked = jnp.full((GRID_W, 2 * GRID_W), NEG_BIG, F32)

    def toeplitz(dr, right):
        x = jnp.broadcast_to(rp_ref[dr:dr + 1, :], (GRID_W, 2 * GRID_W))
        return pltpu.roll(x, 1 if right else GRID_W + 1, 1, stride=1, stride_axis=0)

    for pat_id, r0 in enumerate((0, 2 * NA_ROWS, rows - NA_ROWS)):
        @pl.when(pattern == pat_id)
        def _(r0=r0):
            ks = int(np.clip(r0 - NA_WIN_ROWS // 2, 0, rows - NA_SLAB))
            for r_local in range(NA_ROWS):
                r = r0 + r_local
                rs = int(np.clip(r - NA_WIN_ROWS // 2, 0, rows - NA_WIN_ROWS))
                for jp in range(NA_SLAB // 2):
                    halves = [toeplitz(kr - r + NA_WIN_ROWS - 1, side == 1) if rs <= kr < rs + NA_WIN_ROWS else None
                              for side, kr in enumerate((ks + 2 * jp, ks + 2 * jp + 1))]
                    if halves[0] is None and halves[1] is None:
                        tile = masked
                    else:
                        left = masked if halves[0] is None else halves[0]
                        right = masked if halves[1] is None else halves[1]
                        tile = jnp.where(col_ok, jnp.where(left_half, left, right) * LOG2E, NEG_BIG)
                    o_ref[r_local * GRID_W:(r_local + 1) * GRID_W, jp * 2 * GRID_W:(jp + 1) * 2 * GRID_W] = tile
    o_ref[:, NA_SLAB * GRID_W:] = jnp.zeros((NA_TQ, nc), F32)


def _na_bias_table(rpb, rows, nc):
    heads = rpb.shape[0]
    assert NA_SLAB % 2 == 0 and 2 * GRID_W == LANES and rpb.shape[1] == 2 * NA_WIN_ROWS - 1
    pad = GRID_W - NA_WIN_COLS
    rp = jnp.pad(rpb.astype(F32), ((0, 0), (0, 1), (pad, LANES - pad - rpb.shape[2])))
    n_keys = NA_SLAB * GRID_W + nc
    kern = functools.partial(_na_bias_kernel, rows=rows, nc=nc)
    return pl.pallas_call(
        kern,
        grid=(3, heads),
        in_specs=[pl.BlockSpec((None, 2 * NA_WIN_ROWS, LANES), lambda p, h: (h, 0, 0))],
        out_specs=pl.BlockSpec((None, None, NA_TQ, n_keys), lambda p, h: (p, h, 0, 0)),
        out_shape=jax.ShapeDtypeStruct((3, heads, NA_TQ, n_keys), F32),
        compiler_params=_cparams(2),
        name="na_bias",
    )(rp)


def _na_kernel(q_ref, k_ref, v_ref, bias_ref, ga_ref, o_ref, *, n_lat, nc, rows):
    b = pl.program_id(1)
    nb_lat = rows // NA_ROWS
    scale = (HEAD_DIM ** -0.5) * LOG2E

    def attend(h, keys, values, bias):
        sl = slice(h * HEAD_DIM, (h + 1) * HEAD_DIM)
        s = _dot_nt(q_ref[:, sl], keys) * scale
        if bias is not None:
            s = s + bias
        p = jnp.exp2(s - jnp.max(s, axis=-1, keepdims=True)).astype(BF16)
        o = _dot(p, jnp.concatenate([values, jnp.ones_like(values)], axis=1))
        gate = ga_ref[:, sl].astype(F32)
        o_ref[:, sl] = (o[:, :HEAD_DIM] / o[:, HEAD_DIM:] * gate).astype(BF16)

    @pl.when(b < nb_lat)
    def _():
        ks = jnp.clip(b * NA_ROWS - NA_WIN_ROWS // 2, 0, rows - NA_SLAB)
        start = pl.multiple_of(ks * GRID_W, GRID_W)
        for h in range(NA_HP):
            sl = slice(h * HEAD_DIM, (h + 1) * HEAD_DIM)
            keys = jnp.concatenate([k_ref[pl.ds(start, NA_SLAB * GRID_W), sl], k_ref[n_lat:n_lat + nc, sl]], axis=0)
            values = jnp.concatenate([v_ref[pl.ds(start, NA_SLAB * GRID_W), sl], v_ref[n_lat:n_lat + nc, sl]], axis=0)
            attend(h, keys, values, bias_ref[h])

    @pl.when(b >= nb_lat)
    def _():
        for h in range(NA_HP):
            sl = slice(h * HEAD_DIM, (h + 1) * HEAD_DIM)
            attend(h, k_ref[n_lat:n_lat + nc, sl], v_ref[n_lat:n_lat + nc, sl], None)


def _na_attention(qkv, ga, bias_tab, n_lat):
    nt = qkv.shape[0]
    nc = nt - n_lat
    rows = n_lat // GRID_W
    assert nc == NA_TQ and rows % NA_ROWS == 0 and rows >= NA_SLAB
    nb = nt // NA_TQ
    nb_lat = rows // NA_ROWS
    n_keys = NA_SLAB * GRID_W + nc
    hw = NA_HP * HEAD_DIM
    groups = NA_HEADS // NA_HP

    def pat(b):
        return jnp.where(b == 0, 0, jnp.where(b >= nb_lat - 1, 2, 1))

    kern = functools.partial(_na_kernel, n_lat=n_lat, nc=nc, rows=rows)
    return pl.pallas_call(
        kern,
        grid=(groups, nb),
        in_specs=[
            pl.BlockSpec((NA_TQ, hw), lambda h, b: (b, h)),
            pl.BlockSpec((nt, hw), lambda h, b: (0, groups + h)),
            pl.BlockSpec((nt, hw), lambda h, b: (0, 2 * groups + h)),
            pl.BlockSpec((None, NA_HP, NA_TQ, n_keys), lambda h, b: (pat(b), h, 0, 0)),
            pl.BlockSpec((NA_TQ, hw), lambda h, b: (b, h)),
        ],
        out_specs=pl.BlockSpec((NA_TQ, hw), lambda h, b: (b, h)),
        out_shape=jax.ShapeDtypeStruct((nt, NA_HEADS * HEAD_DIM), BF16),
        compiler_params=_cparams(2),
        name="na_attention",
    )(qkv, qkv, qkv, bias_tab, ga)


def _conv_kernel(um_ref, up_ref, un_ref, gb_ref, w_ref, b_ref, lg_ref, lb_ref, o_ref, ubuf, shifted, *, n_lat, nt):
    t = pl.program_id(0)
    start = t * CONV_T
    prev_ok = jnp.logical_and(start != 0, start != n_lat)
    next_ok = jnp.logical_and(start + CONV_T != n_lat, start + CONV_T != nt)
    ubuf[0:CONV_HALO, :] = jnp.where(prev_ok, up_ref[...], 0.0)
    ubuf[CONV_HALO:CONV_HALO + CONV_T, :] = um_ref[...]
    ubuf[CONV_HALO + CONV_T:, :] = jnp.where(next_ok, un_ref[...], 0.0)
    span = CONV_T + 2 * CONV_HALO - SUBLANES
    for b in range(1, SUBLANES):
        shifted[b - 1, :, :] = ubuf[b:b + span, :]
    off = CONV_HALO - CONV_KERNEL // 2
    for r0 in range(0, CONV_T, CONV_CH):
        cw = um_ref.shape[-1]
        acc = jnp.zeros((CONV_CH // SUBLANES, SUBLANES, cw), F32) + b_ref[...]
        for j in range(CONV_KERNEL):
            a, b = divmod(off + j, SUBLANES)
            rows = slice(r0 + SUBLANES * a, r0 + SUBLANES * a + CONV_CH)
            tap = ubuf[rows, :] if b == 0 else shifted[b - 1, rows, :]
            acc = acc + w_ref[j][None] * tap.reshape(CONV_CH // SUBLANES, SUBLANES, cw)
        acc = acc.reshape(CONV_CH, cw)
        mu = jnp.mean(acc, axis=-1, keepdims=True)
        xc = acc - mu
        var = jnp.mean(xc * xc, axis=-1, keepdims=True)
        y = xc * lax.rsqrt(var + NORM_EPS) * lg_ref[...] + lb_ref[...]
        y = _silu(y) * gb_ref[r0:r0 + CONV_CH, :].astype(F32)
        o_ref[r0:r0 + CONV_CH, :] = y.astype(BF16)


def _conformer_conv(u, gb, dw_w, dw_b, ln_g, ln_b, n_lat):
    nt, cw = u.shape
    assert nt % CONV_T == 0 and n_lat % CONV_T == 0
    hb = CONV_T // CONV_HALO
    n_halo_blocks = nt // CONV_HALO
    assert dw_w.shape == (CONV_KERNEL, cw)
    w_rep = jnp.broadcast_to(dw_w[:, None, :], (CONV_KERNEL, SUBLANES, cw))
    kern = functools.partial(_conv_kernel, n_lat=n_lat, nt=nt)
    vec = lambda: pl.BlockSpec((1, cw), lambda t: (0, 0))
    return pl.pallas_call(
        kern,
        grid=(nt // CONV_T,),
        in_specs=[
            pl.BlockSpec((CONV_T, cw), lambda t: (t, 0)),
            pl.BlockSpec((CONV_HALO, cw), lambda t: (jnp.maximum(t * hb - 1, 0), 0)),
            pl.BlockSpec((CONV_HALO, cw), lambda t: (jnp.minimum((t + 1) * hb, n_halo_blocks - 1), 0)),
            pl.BlockSpec((CONV_T, cw), lambda t: (t, 0)),
            pl.BlockSpec((CONV_KERNEL, SUBLANES, cw), lambda t: (0, 0, 0)),
            vec(), vec(), vec(),
        ],
        out_specs=pl.BlockSpec((CONV_T, cw), lambda t: (t, 0)),
        out_shape=jax.ShapeDtypeStruct((nt, cw), BF16),
        scratch_shapes=[pltpu.VMEM((CONV_T + 2 * CONV_HALO, cw), F32),
                        pltpu.VMEM((SUBLANES - 1, CONV_T + 2 * CONV_HALO - SUBLANES, cw), F32)],
        compiler_params=_cparams(1),
        name="conformer_conv",
    )(u, u, u, gb, w_rep, dw_b.reshape(1, cw), ln_g.reshape(1, cw), ln_b.reshape(1, cw))


def _out_proj_kernel(a_ref, b_ref, w32_ref, x_ref, c_ref, mod_ref, fg_ref, o_ref, w_ref, *, n_lat, tm, final):
    i = pl.program_id(0)

    @pl.when(i == 0)
    def _():
        w_ref[...] = w32_ref[...].astype(BF16)

    half = a_ref.shape[-1]
    d = o_ref.shape[-1]
    is_ctx = i * tm >= n_lat
    row = is_ctx.astype(jnp.int32)
    n_parts = 2
    pw = d // n_parts
    parts = []
    for c in range(n_parts):
        sl = slice(c * pw, (c + 1) * pw)
        y = _dot(a_ref[...], w_ref[0:half, sl]) + _dot(b_ref[...], w_ref[half:, sl])
        gt = mod_ref[pl.ds(row, 1), 2 * d + c * pw:2 * d + (c + 1) * pw]
        xn = jnp.where(is_ctx, c_ref[:, sl], x_ref[:, sl]) + gt * y
        if final:
            parts.append(xn)
        else:
            o_ref[:, sl] = xn
    if final:
        ms = sum(jnp.sum(p * p, axis=-1, keepdims=True) for p in parts) * (1.0 / d)
        r = lax.rsqrt(ms + NORM_EPS)
        for c, p in enumerate(parts):
            sl = slice(c * pw, (c + 1) * pw)
            o_ref[:, sl] = p * r * fg_ref[:, sl]


def _out_proj(mix_a, mix_b, cols, w, layer, stream, nt, mod, fg, n_lat, final):
    d = stream[0].shape[1]
    tm = TM_OUT
    half = w.shape[1] // 2
    out_rows = n_lat if final else nt
    assert out_rows % tm == 0
    kern = functools.partial(_out_proj_kernel, n_lat=n_lat, tm=tm, final=final)
    return pl.pallas_call(
        kern,
        grid=(out_rows // tm,),
        in_specs=[
            pl.BlockSpec((tm, half), lambda i: (i, cols[0])),
            pl.BlockSpec((tm, half), lambda i: (i, cols[1])),
            pl.BlockSpec((None, 2 * half, d), lambda i: (layer, 0, 0), pipeline_mode=pl.Buffered(1)),
        ] + _stream_specs(*stream, tm, n_lat) + [
            pl.BlockSpec((SUBLANES, 3 * d), lambda i: (0, 0)),
            pl.BlockSpec((1, d), lambda i: (0, 0)),
        ],
        out_specs=pl.BlockSpec((tm, d), lambda i: (i, 0)),
        out_shape=jax.ShapeDtypeStruct((out_rows, d), F32),
        scratch_shapes=[pltpu.VMEM((2 * half, d), BF16)],
        compiler_params=_cparams(1),
        name="out_proj_final" if final else "out_proj",
    )(mix_a, mix_b, w, stream[0], stream[1], mod, fg)


def _rope_tables(n_lat, nc):
    rows = n_lat // GRID_W
    n_freq = HEAD_DIM // 4
    inv = ROPE_THETA ** (-jnp.arange(n_freq, dtype=F32) / n_freq)
    row_ang = jnp.arange(rows, dtype=jnp.int32).astype(F32)[:, None] * inv
    col_ang = jnp.arange(GRID_W, dtype=jnp.int32).astype(F32)[:, None] * inv
    per_row = lambda a: jnp.repeat(a, GRID_W, axis=0)
    per_col = lambda a: jnp.tile(a, (rows, 1))
    cr, sr = per_row(jnp.cos(row_ang)), per_row(jnp.sin(row_ang))
    cc, sc = per_col(jnp.cos(col_ang)), per_col(jnp.sin(col_ang))
    cos_t = jnp.concatenate([cr, cr, cc, cc], axis=-1)
    sin_t = jnp.concatenate([-sr, sr, -sc, sc], axis=-1)
    cos_t = jnp.concatenate([cos_t, jnp.ones((nc, HEAD_DIM), F32)], axis=0)
    sin_t = jnp.concatenate([sin_t, jnp.zeros((nc, HEAD_DIM), F32)], axis=0)
    return cos_t, sin_t


def _norm_rope(x, gain, cos_t, sin_t, first_half):
    ms = jnp.mean(x * x, axis=-1, keepdims=True)
    xn = x * lax.rsqrt(ms + NORM_EPS) * gain
    partner = jnp.where(first_half, pltpu.roll(xn, 3 * HEAD_DIM // 4, 1), pltpu.roll(xn, HEAD_DIM // 4, 1))
    return xn * cos_t + partner * sin_t


def _odd_proj_kernel(x_ref, c_ref, g_ref, mod_ref, w_hbm, wvt_ref, cos_ref, sin_ref, qn_ref, kn_ref,
                     q_ref, k_ref, vt_ref, gs_ref, w_ref, stage, sem, *, n_lat, tm, layer):
    @pl.when(pl.program_id(0) == 0)
    def _():
        _load_weights_bf16(w_hbm, layer, w_ref, stage, sem)

    h = _prenorm(x_ref, c_ref, g_ref, mod_ref, pl.program_id(0), tm, n_lat)
    tn = k_ref.shape[-1]
    cos_t = cos_ref[...]
    sin_t = sin_ref[...]
    lane = lax.broadcasted_iota(jnp.int32, (1, HEAD_DIM), 1)
    first_half = (lane % (HEAD_DIM // 2)) < (HEAD_DIM // 4)

    def proj(seg):
        return _dot(h, w_ref[:, seg * tn:(seg + 1) * tn])

    def rotated(seg, gain, out_ref, col0):
        acc = proj(seg)
        for hh in range(tn // HEAD_DIM):
            y = _norm_rope(acc[:, hh * HEAD_DIM:(hh + 1) * HEAD_DIM], gain, cos_t, sin_t, first_half)
            out_ref[:, col0 + hh * HEAD_DIM:col0 + (hh + 1) * HEAD_DIM] = y.astype(BF16)

    n_q = q_ref.shape[-1] // tn
    q_gain = qn_ref[...] * ((HEAD_DIM ** -0.5) * LOG2E)
    for seg in range(n_q):
        rotated(seg, q_gain, q_ref, seg * tn)
    rotated(n_q, kn_ref[...], k_ref, 0)
    v_t = _dot_nt(wvt_ref[...], h)
    vt_ref[...] = v_t.reshape(vt_ref.shape).astype(BF16)
    for seg in range(n_q):
        gs_ref[:, seg * tn:(seg + 1) * tn] = _silu(proj(n_q + 2 + seg)).astype(BF16)


def _odd_proj(stream, nt, g, mod, w, w_vt, layer, cos_t, sin_t, q_norm, k_norm, n_lat):
    d = stream[0].shape[1]
    tm = TM_PROJ
    qw = GQA_HEADS * HEAD_DIM
    kw = GQA_KV_HEADS * HEAD_DIM
    assert w.shape[2] == 2 * qw + 2 * kw and nt % tm == 0 and n_lat % tm == 0
    kern = functools.partial(_odd_proj_kernel, n_lat=n_lat, tm=tm, layer=layer)
    rows = lambda width: pl.BlockSpec((tm, width), lambda i: (i, 0))
    vec = lambda width: pl.BlockSpec((1, width), lambda i: (0, 0))
    return pl.pallas_call(
        kern,
        grid=(nt // tm,),
        in_specs=_stream_specs(*stream, tm, n_lat) + [
            vec(d),
            pl.BlockSpec((SUBLANES, 3 * d), lambda i: (0, 0)),
            pl.BlockSpec(memory_space=pl.ANY),
            pl.BlockSpec((kw, d), lambda i: (0, 0)),
            rows(HEAD_DIM), rows(HEAD_DIM), vec(HEAD_DIM), vec(HEAD_DIM),
        ],
        out_specs=[rows(qw), rows(kw),
                   pl.BlockSpec((GQA_KV_HEADS, None, HEAD_DIM, tm), lambda i: (0, i, 0, 0)),
                   rows(qw)],
        out_shape=[
            jax.ShapeDtypeStruct((nt, qw), BF16),
            jax.ShapeDtypeStruct((nt, kw), BF16),
            jax.ShapeDtypeStruct((GQA_KV_HEADS, nt // tm, HEAD_DIM, tm), BF16),
            jax.ShapeDtypeStruct((nt, qw), BF16),
        ],
        scratch_shapes=_weight_scratch(w),
        compiler_params=_cparams(1),
        name="odd_proj",
    )(stream[0], stream[1], g, mod, w, w_vt, cos_t, sin_t,
      q_norm.reshape(1, HEAD_DIM), k_norm.reshape(1, HEAD_DIM))


def _gqa_kernel(bound_ref, q_ref, k_ref, vt_ref, g_ref, o_ref, *, n_lat, nc):
    b = pl.program_id(1)
    tq = q_ref.shape[0]
    vb = vt_ref.shape[-1]
    bound = bound_ref[0]
    is_lat = b * tq < n_lat
    q = jnp.concatenate([q_ref[:, h * HEAD_DIM:(h + 1) * HEAD_DIM] for h in range(GQA_GROUP)], axis=0)

    def vt_chunk(first_block, n_blocks):
        return jnp.concatenate([vt_ref[first_block + j] for j in range(n_blocks)], axis=1)

    def write_out(o_t):
        o = o_t.T
        for h in range(GQA_GROUP):
            sl = slice(h * HEAD_DIM, (h + 1) * HEAD_DIM)
            o_ref[:, sl] = (o[h * tq:(h + 1) * tq, :] * g_ref[:, sl].astype(F32)).astype(BF16)

    def fixed_shift_chunk(row0, n_rows):
        p = jnp.exp2(_dot_nt(k_ref[row0:row0 + n_rows, :], q) - bound)
        partial = jnp.sum(p.reshape(n_rows // SUBLANES, SUBLANES, p.shape[-1]), axis=0)
        return _dot(vt_chunk(row0 // vb, n_rows // vb), p.astype(BF16)), partial

    def finish(acc, partial):
        write_out(acc / jnp.sum(partial, axis=0, keepdims=True))

    small = bound <= GQA_SAFE_BOUND

    @pl.when(jnp.logical_and(small, is_lat))
    def _():
        acc, partial = fixed_shift_chunk(n_lat, nc)
        for c in range(n_lat // GQA_TK):
            acc_c, partial_c = fixed_shift_chunk(c * GQA_TK, GQA_TK)
            acc, partial = acc + acc_c, partial + partial_c
        finish(acc, partial)

    @pl.when(jnp.logical_and(small, jnp.logical_not(is_lat)))
    def _():
        finish(*fixed_shift_chunk(n_lat, nc))

    @pl.when(bound > GQA_SAFE_BOUND)
    def _():
        cols = GQA_GROUP * tq
        per = GQA_TK_ONLINE // vb

        def step(carry, k_chunk, vt):
            m, l, acc = carry
            s = _dot_nt(k_chunk, q)
            m_new = jnp.maximum(m, jnp.max(s, axis=0, keepdims=True))
            alpha = jnp.exp2(m - m_new)
            p = jnp.exp2(s - m_new)
            l = alpha * l + jnp.sum(p, axis=0, keepdims=True)
            acc = alpha * acc + _dot(vt, p.astype(BF16))
            return m_new, l, acc

        init = (jnp.full((1, cols), NEG_BIG, F32), jnp.zeros((1, cols), F32), jnp.zeros((HEAD_DIM, cols), F32))
        carry = step(init, k_ref[n_lat:n_lat + nc, :], vt_chunk(n_lat // vb, nc // vb))

        def body(c, carry):
            start = pl.multiple_of(c * GQA_TK_ONLINE, GQA_TK_ONLINE)
            vt = jnp.concatenate([vt_ref[c * per + j] for j in range(per)], axis=1)
            return step(carry, k_ref[pl.ds(start, GQA_TK_ONLINE), :], vt)

        m, l, acc = lax.fori_loop(0, jnp.where(is_lat, n_lat // GQA_TK_ONLINE, 0), body, carry)
        write_out(acc / l)


def _gqa_attention(q, k, vt, gs, q_norm, k_norm, n_lat):
    nt = q.shape[0]
    nc = nt - n_lat
    tq = GQA_TQ
    gw = GQA_GROUP * HEAD_DIM
    n_vb, vb = vt.shape[1], vt.shape[3]
    assert nt % tq == 0 and n_lat % tq == 0 and n_lat % GQA_TK == 0 and n_lat % GQA_TK_ONLINE == 0
    assert n_vb * vb == nt and n_lat % vb == 0 and nc % vb == 0 and GQA_TK_ONLINE % vb == 0
    bound = (1.01 * HEAD_DIM * (HEAD_DIM ** -0.5) * LOG2E) * jnp.max(jnp.abs(q_norm)) * jnp.max(jnp.abs(k_norm))
    kern = functools.partial(_gqa_kernel, n_lat=n_lat, nc=nc)
    return pl.pallas_call(
        kern,
        grid=(GQA_KV_HEADS, nt // tq),
        in_specs=[
            pl.BlockSpec(memory_space=pltpu.SMEM),
            pl.BlockSpec((tq, gw), lambda h, b: (b, h)),
            pl.BlockSpec((nt, HEAD_DIM), lambda h, b: (0, h)),
            pl.BlockSpec((None, n_vb, HEAD_DIM, vb), lambda h, b: (h, 0, 0, 0)),
            pl.BlockSpec((tq, gw), lambda h, b: (b, h)),
        ],
        out_specs=pl.BlockSpec((tq, gw), lambda h, b: (b, h)),
        out_shape=jax.ShapeDtypeStruct((nt, GQA_HEADS * HEAD_DIM), BF16),
        compiler_params=_cparams(2),
        name="gqa_attention",
    )(bound.reshape(1).astype(F32), q, k, vt, gs)


def kernel(x, c, ctx, c_ctx, norm_g, w_mod, b_mod, e_w_in, e_rpb, e_dw_w, e_dw_b, e_ln_g, e_ln_b, e_w_out,
           o_w_in, o_q_norm, o_k_norm, o_w_out, final_norm_g):
    batch, n_lat, d = x.shape
    nc = ctx.shape[1]
    depth = norm_g.shape[0]
    assert batch == 1
    rows = n_lat // GRID_W

    nt = n_lat + nc
    stream = (x[0], ctx[0], 0)
    cvec = jnp.zeros((SUBLANES, d), F32).at[0].set(c[0]).at[1].set(c_ctx)
    mods = _modulation(cvec, w_mod, b_mod)
    cos_t, sin_t = _rope_tables(n_lat, nc)
    fg = final_norm_g.reshape(1, d)

    for l in range(depth):
        i = l // 2
        g = norm_g[l].reshape(1, d)
        mod = mods[l]
        final = l == depth - 1
        if l % 2 == 0:
            qkv, ga, u, gb = _even_proj(stream, nt, g, mod, e_w_in, i, n_lat)
            mix_a = _na_attention(qkv, ga, _na_bias_table(e_rpb[i], rows, nc), n_lat)
            mix_b = _conformer_conv(u, gb, e_dw_w[i], e_dw_b[i], e_ln_g[i], e_ln_b[i], n_lat)
            xs = _out_proj(mix_a, mix_b, (0, 0), e_w_out, i, stream, nt, mod, fg, n_lat, final)
        else:
            v0 = (GQA_HEADS + GQA_KV_HEADS) * HEAD_DIM
            w_v = lax.optimization_barrier(o_w_in[i, :, v0:v0 + GQA_KV_HEADS * HEAD_DIM])
            w_vt = w_v.T.astype(BF16)
            q, k, vt, gs = _odd_proj(stream, nt, g, mod, o_w_in, w_vt, i, cos_t, sin_t,
                                     o_q_norm[i], o_k_norm[i], n_lat)
            mix = _gqa_attention(q, k, vt, gs, o_q_norm[i], o_k_norm[i], n_lat)
            xs = _out_proj(mix, mix, (0, 1), o_w_out, i, stream, nt, mod, fg, n_lat, final)
        stream = (xs, xs, n_lat)
    return xs[None]
```
